```python
import math
import jax, jax.numpy as jnp
from jax import lax
import numpy as np

D_MODEL = 2048
BATCH = 8
SEQ = 2048
DEPTH = 2

HEAD_DIM = 64
GROUP_WIDTH = D_MODEL // 4
MIX_WIDTH = 4 * GROUP_WIDTH
H_A = GROUP_WIDTH // HEAD_DIM
H_B = GROUP_WIDTH // HEAD_DIM
H_C = GROUP_WIDTH // HEAD_DIM
DIFF_HALF = HEAD_DIM
DIFF_VDIM = 2 * DIFF_HALF
H_D = GROUP_WIDTH // DIFF_VDIM
W_LORA = max(32, int(round(D_MODEL ** 0.5 * 1.8 / 32)) * 32)
A_LORA = max(32, int(round(D_MODEL ** 0.5 * 1.8 / 32)) * 32)
V_LORA = max(32, int(round(D_MODEL ** 0.5 * 1.3 / 32)) * 32)
G_LORA = max(32, int(round(D_MODEL ** 0.8 / 32)) * 32)
N_A_COLS = 3 * GROUP_WIDTH + W_LORA + A_LORA + G_LORA
N_IN = N_A_COLS + 9 * GROUP_WIDTH
A_SPLITS = [GROUP_WIDTH, 2 * GROUP_WIDTH, 3 * GROUP_WIDTH,
            3 * GROUP_WIDTH + W_LORA, 3 * GROUP_WIDTH + W_LORA + A_LORA]
D_FF = ((8 * D_MODEL + 3 * 256 - 1) // (3 * 256)) * 256
DILATIONS = ((128, 1), (512, 4), (2048, 16))
BLOCK = 128
NUM_BUCKETS = 32
MAX_DISTANCE = 2048
NORM_EPS = 1e-6
RWKV_LN_EPS = 64e-5
SUBLN_EPS = 1e-5

kernel_name = "hybrid_parallel_heads_rwkv7_stickbreak_dilated_diffattn"


def rms_norm(x, gain, eps=NORM_EPS):
    xf = x.astype(jnp.float32)
    y = xf * lax.rsqrt(jnp.mean(xf * xf, axis=-1, keepdims=True) + eps)
    return y * gain.astype(jnp.float32)


def rel_bucket(dist):
    max_exact = NUM_BUCKETS // 2
    d_f = jnp.maximum(dist, 1).astype(jnp.float32)
    large = max_exact + (jnp.log(d_f / max_exact) / math.log(MAX_DISTANCE / max_exact)
                         * (NUM_BUCKETS - max_exact)).astype(jnp.int32)
    large = jnp.minimum(large, NUM_BUCKETS - 1)
    return jnp.where(dist < max_exact, dist, large)


def token_shift(p, mu):
    prev = jnp.pad(p, ((0, 0), (1, 0), (0, 0)))[:, :-1]
    return p + (prev - p) * mu


def rwkv7_time_mix(r, k, v, w_lo, a_lo, g_lo, w0, w_up, a0, a_up, g_up,
                   k_k, k_a, r_k, ln_w, ln_b):
    f32 = jnp.float32
    B, S, C = r.shape
    r, k, v = r.astype(f32), k.astype(f32), v.astype(f32)
    w_log = -jax.nn.softplus(-(w0 + jnp.tanh(w_lo.astype(f32)) @ w_up)) - 0.5
    decay = jnp.exp(-jnp.exp(w_log))
    a = jax.nn.sigmoid(a0 + a_lo.astype(f32) @ a_up)
    g = jax.nn.sigmoid(g_lo.astype(f32)) @ g_up
    heads = lambda t: t.reshape(B, S, H_A, HEAD_DIM)
    kk = heads(k * k_k)
    kk = kk * lax.rsqrt(jnp.maximum(jnp.sum(kk * kk, axis=-1, keepdims=True), 1e-24))
    k = k * (1.0 + (a - 1.0) * k_a)
    rh, kh, vh, wh, ah = heads(r), heads(k), heads(v), heads(decay), heads(a)

    def step(state, inp):
        r_t, w_t, k_t, v_t, kk_t, b_t = inp
        sa = jnp.einsum('bhvk,bhk->bhv', state, -kk_t)
        state = (state * w_t[:, :, None, :] + sa[..., None] * b_t[:, :, None, :]
                 + v_t[..., None] * k_t[:, :, None, :])
        return state, jnp.einsum('bhvk,bhk->bhv', state, r_t)

    tm = lambda t: jnp.moveaxis(t, 1, 0)
    state0 = jnp.zeros((B, H_A, HEAD_DIM, HEAD_DIM), f32)
    _, y = lax.scan(step, state0, (tm(rh), tm(wh), tm(kh), tm(vh), tm(kk), tm(kk * ah)))
    y = jnp.moveaxis(y, 0, 1)
    mu = jnp.mean(y, axis=-1, keepdims=True)
    var = jnp.mean(jnp.square(y - mu), axis=-1, keepdims=True)
    y = ((y - mu) * lax.rsqrt(var + RWKV_LN_EPS)).reshape(B, S, C) * ln_w + ln_b
    bonus = jnp.sum(rh * kh * r_k, axis=-1, keepdims=True) * vh
    return (y + bonus.reshape(B, S, C)) * g


def stick_breaking_attention(q, k, v):
    f32 = jnp.float32
    B, S, H, Dh = q.shape
    nb = S // BLOCK
    scale = Dh ** -0.5
    kf, vf = k.astype(f32), v.astype(f32)
    qb = jnp.moveaxis(q.astype(f32).reshape(B, nb, BLOCK, H, Dh), 1, 0)
    key_pos = jnp.arange(S)

    def block(args):
        q_blk, i = args
        z = jnp.einsum('bqhd,bkhd->bhqk', q_blk, kf) * scale
        qpos = i * BLOCK + jnp.arange(BLOCK)
        past = key_pos[None, :] < qpos[:, None]
        log_1m = jnp.where(past, jax.nn.log_sigmoid(-z), 0.0)
        between = lax.cumsum(log_1m, axis=3, reverse=True) - log_1m
        att = jnp.where(past, jnp.exp(jax.nn.log_sigmoid(z) + between), 0.0)
        return jnp.einsum('bhqk,bkhd->bqhd', att, vf)

    o = lax.map(block, (qb, jnp.arange(nb)))
    return jnp.moveaxis(o, 0, 1).reshape(B, S, H * Dh)


def dilated_window_pattern(q, k, v, rel_tab, window, dil):
    f32 = jnp.float32
    B, S, H, Dh = q.shape
    n_back = window // dil
    assert n_back <= BLOCK
    L = S // dil
    nc = -(-L // BLOCK)
    Lp = nc * BLOCK
    scale = Dh ** -0.5

    def gather_stride(t):
        t = t.astype(f32).reshape(B, L, dil, H, Dh).transpose(0, 2, 1, 3, 4).reshape(B * dil, L, H, Dh)
        t = jnp.pad(t, ((0, 0), (0, Lp - L), (0, 0), (0, 0)))
        return t.reshape(B * dil, nc, BLOCK, H, Dh)

    def with_prev(t):
        prev = jnp.pad(t, ((0, 0), (1, 0), (0, 0), (0, 0), (0, 0)))[:, :-1]
        return jnp.concatenate([prev, t], axis=2)

    qc = gather_stride(q)
    kb, vb = with_prev(gather_stride(k)), with_prev(gather_stride(v))
    qi = jnp.arange(BLOCK)[:, None]
    ki = jnp.arange(2 * BLOCK)[None, :]
    steps = qi + BLOCK - ki
    band = (steps >= 0) & (steps <= n_back)
    chunk = jnp.arange(nc)[:, None, None]
    valid = band[None] & (chunk * BLOCK + ki[None] - BLOCK >= 0)
    bias = jnp.transpose(rel_tab[rel_bucket(jnp.maximum(steps, 0) * dil)], (2, 0, 1))
    s = jnp.einsum('gnqhd,gnkhd->gnhqk', qc, kb) * scale + bias[None, None]
    s = jnp.where(valid[None, :, None], s, -jnp.inf)
    lse = jax.nn.logsumexp(s, axis=-1)
    p = jnp.exp(s - lse[..., None])
    o = jnp.einsum('gnhqk,gnkhd->gnqhd', p, vb).reshape(B * dil, Lp, H, Dh)[:, :L]
    o = o.reshape(B, dil, L, H, Dh).transpose(0, 2, 1, 3, 4).reshape(B, S, H, Dh)
    lse = jnp.transpose(lse, (0, 1, 3, 2)).reshape(B * dil, Lp, H)[:, :L]
    lse = lse.reshape(B, dil, L, H).transpose(0, 2, 1, 3).reshape(B, S, H)
    return o, lse


def dilated_attention(q, k, v, rel_tab):
    B, S, H, Dh = q.shape
    outs, lses = [], []
    for window, dil in DILATIONS:
        o, lse = dilated_window_pattern(q, k, v, rel_tab, window, dil)
        outs.append(o)
        lses.append(lse)
    wts = jax.nn.softmax(jnp.stack(lses), axis=0)
    o = jnp.sum(wts[..., None] * jnp.stack(outs), axis=0)
    return o.reshape(B, S, H * Dh)


def differential_attention(q, k, v, lam_params, lam_init, subln, rel_tab):
    f32 = jnp.float32
    B, S, H, _ = q.shape
    nb = S // BLOCK
    scale = DIFF_HALF ** -0.5
    lp = lam_params.astype(f32)
    lam = jnp.exp(jnp.sum(lp[0] * lp[1])) - jnp.exp(jnp.sum(lp[2] * lp[3])) + lam_init
    kf = k.astype(f32).reshape(B, S, H, 2, DIFF_HALF)
    vf = v.astype(f32)
    qb = jnp.moveaxis(q.astype(f32).reshape(B, nb, BLOCK, H, 2, DIFF_HALF), 1, 0)
    key_pos = jnp.arange(S)

    def block(args):
        q_blk, i = args
        qpos = i * BLOCK + jnp.arange(BLOCK)
        dist = qpos[:, None] - key_pos[None, :]
        bias = jnp.transpose(rel_tab[rel_bucket(jnp.maximum(dist, 0))], (2, 0, 1))
        s = jnp.einsum('bqhcd,bkhcd->bchqk', q_blk, kf) * scale + bias[None, None]
        s = jnp.where(dist >= 0, s, -jnp.inf)
        p = jax.nn.softmax(s, axis=-1)
        return jnp.einsum('bhqk,bkhd->bqhd', p[:, 0] - lam * p[:, 1], vf)

    o = jnp.moveaxis(lax.map(block, (qb, jnp.arange(nb))), 0, 1).reshape(B, S, H, DIFF_VDIM)
    o = rms_norm(o, subln, SUBLN_EPS) * (1.0 - lam_init)
    return o.reshape(B, S, H * DIFF_VDIM)


def setup_inputs(seed: int = 0) -> dict:
    key = jax.random.key(seed)
    ks = jax.random.split(key, 32)
    f32 = jnp.float32
    nrm = lambda kk, shape, scale: jax.random.normal(kk, shape, f32) * scale
    uni = lambda kk, shape, lo, hi: jax.random.uniform(kk, shape, f32, lo, hi)
    D, C = D_MODEL, GROUP_WIDTH
    return {
        "x": nrm(ks[0], (BATCH, SEQ, D), 1.0),
        "c": nrm(ks[1], (BATCH, D), 1.0),
        "w_ada": nrm(ks[2], (DEPTH, D, 6 * D), 0.5 * D ** -0.5),
        "b_ada": nrm(ks[3], (DEPTH, 6 * D), 0.02),
        "norm_gain": 1.0 + nrm(ks[4], (DEPTH, 2, D), 0.05),
        "w_in": nrm(ks[5], (DEPTH, D, N_IN), D ** -0.5),
        "w_out": nrm(ks[6], (DEPTH, MIX_WIDTH, D), MIX_WIDTH ** -0.5),
        "rel_bias": nrm(ks[7], (NUM_BUCKETS, H_C + H_D), 0.5),
        "rwkv_mu": uni(ks[8], (DEPTH, N_A_COLS), 0.0, 1.0),
        "rwkv_w0": uni(ks[9], (DEPTH, C), -6.0, 1.0),
        "rwkv_w_up": nrm(ks[10], (DEPTH, W_LORA, C), W_LORA ** -0.5),
        "rwkv_a0": nrm(ks[11], (DEPTH, C), 0.5),
        "rwkv_a_up": nrm(ks[12], (DEPTH, A_LORA, C), A_LORA ** -0.5),
        "rwkv_g_up": nrm(ks[13], (DEPTH, G_LORA, C), G_LORA ** -0.5),
        "rwkv_k_k": 0.85 + nrm(ks[14], (DEPTH, C), 0.05),
        "rwkv_k_a": 1.0 + nrm(ks[15], (DEPTH, C), 0.05),
        "rwkv_r_k": nrm(ks[16], (DEPTH, H_A, HEAD_DIM), 0.1),
        "rwkv_ln_w": 1.0 + nrm(ks[17], (DEPTH, C), 0.05),
        "rwkv_ln_b": nrm(ks[18], (DEPTH, C), 0.02),
        "vres_down": nrm(ks[19], (DEPTH - 1, D, V_LORA), D ** -0.5),
        "vres_mu": uni(ks[20], (DEPTH - 1, V_LORA), 0.0, 1.0),
        "vres_up": nrm(ks[21], (DEPTH - 1, V_LORA, C), V_LORA ** -0.5),
        "vres_bias": nrm(ks[22], (DEPTH - 1, C), 0.5),
        "diff_lambda": nrm(ks[23], (DEPTH, 4, DIFF_HALF), 0.1),
        "diff_subln": 1.0 + nrm(ks[24], (DEPTH, DIFF_VDIM), 0.05),
        "ffn_w13": nrm(ks[25], (DEPTH, D, 2 * D_FF), D ** -0.5),
        "ffn_w2": nrm(ks[26], (DEPTH, D_FF, D), D_FF ** -0.5),
        "final_gain": 1.0 + nrm(ks[27], (D,), 0.05),
    }


def reference(x, c, w_ada, b_ada, norm_gain, w_in, w_out, rel_bias, rwkv_mu, rwkv_w0,
              rwkv_w_up, rwkv_a0, rwkv_a_up, rwkv_g_up, rwkv_k_k, rwkv_k_a, rwkv_r_k,
              rwkv_ln_w, rwkv_ln_b, vres_down, vres_mu, vres_up, vres_bias, diff_lambda,
              diff_subln, ffn_w13, ffn_w2, final_gain):
    out_dtype = x.dtype
    B, S, D = x.shape
    x = x.astype(jnp.float32)
    cond = jax.nn.silu(c.astype(jnp.float32))
    rel_tab_c = rel_bias[:, :H_C].astype(jnp.float32)
    rel_tab_d = rel_bias[:, H_C:].astype(jnp.float32)
    heads = lambda t, dh: t.reshape(B, S, -1, dh)
    v_first = None
    for l in range(DEPTH):
        mod = cond @ w_ada[l] + b_ada[l]
        sh1, sc1, g1, sh2, sc2, g2 = jnp.split(mod[:, None, :], 6, axis=-1)
        h = rms_norm(x, norm_gain[l, 0]) * (1.0 + sc1) + sh1
        w_proj = w_in[l] if l == 0 else jnp.concatenate([w_in[l], vres_down[l - 1]], axis=1)
        p = h @ w_proj
        pa = token_shift(p[..., :N_A_COLS], rwkv_mu[l])
        r_a, k_a_, v_a, w_lo, a_lo, g_lo = jnp.split(pa, A_SPLITS, axis=-1)
        if l == 0:
            v_first = v_a
        else:
            pv = token_shift(p[..., N_IN:], vres_mu[l - 1])
            v_a = v_a + (v_first - v_a) * jax.nn.sigmoid(vres_bias[l - 1] + pv @ vres_up[l - 1])
        y_a = rwkv7_time_mix(r_a, k_a_, v_a, w_lo, a_lo, g_lo, rwkv_w0[l], rwkv_w_up[l],
                             rwkv_a0[l], rwkv_a_up[l], rwkv_g_up[l], rwkv_k_k[l],
                             rwkv_k_a[l], rwkv_r_k[l], rwkv_ln_w[l], rwkv_ln_b[l])
        q_b, k_b, v_b, q_c, k_c, v_c, q_d, k_d, v_d = jnp.split(p[..., N_A_COLS:N_IN], 9, axis=-1)
        y_b = stick_breaking_attention(heads(q_b, HEAD_DIM), heads(k_b, HEAD_DIM), heads(v_b, HEAD_DIM))
        y_c = dilated_attention(heads(q_c, HEAD_DIM), heads(k_c, HEAD_DIM), heads(v_c, HEAD_DIM), rel_tab_c)
        lam_init = 0.8 - 0.6 * math.exp(-0.3 * l)
        y_d = differential_attention(heads(q_d, 2 * DIFF_HALF), heads(k_d, 2 * DIFF_HALF),
                                     heads(v_d, DIFF_VDIM), diff_lambda[l], lam_init,
                                     diff_subln[l], rel_tab_d)
        y_mix = jnp.concatenate([y_a, y_b, y_c, y_d], axis=-1) @ w_out[l]
        x = x + g1 * y_mix
        h2 = rms_norm(x, norm_gain[l, 1]) * (1.0 + sc2) + sh2
        gate, up = jnp.split(h2 @ ffn_w13[l], 2, axis=-1)
        x = x + g2 * ((jax.nn.silu(gate) * up) @ ffn_w2[l])
    return rms_norm(x, final_gain).astype(out_dtype)
```

```python
import functools
import math

import jax
import jax.numpy as jnp
from jax import lax
from jax.experimental import pallas as pl
from jax.experimental.pallas import tpu as pltpu

F32 = jnp.float32
BF16 = jnp.bfloat16

D_MODEL = 2048
DEPTH = 2
HEAD_DIM = 64
GROUP_WIDTH = D_MODEL // 4
N_HEADS = GROUP_WIDTH // HEAD_DIM
DIFF_HALF = HEAD_DIM
DIFF_VDIM = 2 * DIFF_HALF
H_D = GROUP_WIDTH // DIFF_VDIM
W_LORA = max(32, int(round(D_MODEL ** 0.5 * 1.8 / 32)) * 32)
A_LORA = max(32, int(round(D_MODEL ** 0.5 * 1.8 / 32)) * 32)
V_LORA = max(32, int(round(D_MODEL ** 0.5 * 1.3 / 32)) * 32)
G_LORA = max(32, int(round(D_MODEL ** 0.8 / 32)) * 32)
N_A_COLS = 3 * GROUP_WIDTH + W_LORA + A_LORA + G_LORA
N_IN = N_A_COLS + 9 * GROUP_WIDTH
D_FF = ((8 * D_MODEL + 3 * 256 - 1) // (3 * 256)) * 256
DILATIONS = ((128, 1), (512, 4), (2048, 16))
BLOCK = 128
NUM_BUCKETS = 32
MAX_DISTANCE = 2048
NORM_EPS = 1e-6
RWKV_LN_EPS = 64e-5
SUBLN_EPS = 1e-5

LANES = 128
VMEM_LIMIT_BYTES = 56 * 1024 * 1024

LORA_PAD = 128
G_PAD = 512
A_OFF_W = 3 * GROUP_WIDTH
A_OFF_A = A_OFF_W + LORA_PAD
A_OFF_G = A_OFF_A + LORA_PAD
A_OFF_V = A_OFF_G + G_PAD
NA_L0 = A_OFF_V
NA_LN = A_OFF_V + 2 * LORA_PAD
CHUNK = 64
RWKV_TILE = 256
NEG_BIG = -1e30


def _bf(x):
    return x.astype(BF16)


def _mm(a, b):
    return jnp.dot(_bf(a), _bf(b), preferred_element_type=F32)


def _mm_nt(a, b):
    return lax.dot_general(_bf(a), _bf(b), (((1,), (1,)), ((), ())), preferred_element_type=F32)


def _mm_tn(a, b):
    return lax.dot_general(_bf(a), _bf(b), (((0,), (0,)), ((), ())), preferred_element_type=F32)


def _split_bf16(x, n):
    parts = []
    rem = x
    for i in range(n):
        p = _bf(rem)
        parts.append(p)
        if i + 1 < n:
            rem = rem - p.astype(F32)
    return parts


def _mm_split_lhs(a, b_bf16, n):
    out = None
    for p in _split_bf16(a, n):
        t = jnp.dot(p, b_bf16, preferred_element_type=F32)
        out = t if out is None else out + t
    return out


def _mm_split_rhs(a_bf16, b, n):
    out = None
    for p in _split_bf16(b, n):
        t = jnp.dot(a_bf16, p, preferred_element_type=F32)
        out = t if out is None else out + t
    return out


def _softplus(x):
    return jnp.maximum(x, 0.0) + jnp.log1p(jnp.exp(-jnp.abs(x)))


def _sigmoid(x):
    return jax.nn.sigmoid(x)


def _iota2(shape, dim):
    return lax.broadcasted_iota(jnp.int32, shape, dim)


def _params(sem, vmem=VMEM_LIMIT_BYTES):
    return pltpu.CompilerParams(dimension_semantics=sem, vmem_limit_bytes=vmem)


def _mod_kernel(c_ref, w_ref, b_ref, o_ref):
    c = c_ref[...]
    cond = c * _sigmoid(c)
    o_ref[0] = _mm(cond, w_ref[0]) + b_ref[0]


def _modulation(c, w_ada, b_ada):
    depth, d, n = w_ada.shape
    bsz = c.shape[0]
    tn = 1024
    return pl.pallas_call(
        _mod_kernel,
        grid=(depth, n // tn),
        in_specs=[
            pl.BlockSpec((bsz, d), lambda l, j: (0, 0)),
            pl.BlockSpec((1, d, tn), lambda l, j: (l, 0, j)),
            pl.BlockSpec((1, 1, tn), lambda l, j: (l, 0, j)),
        ],
        out_specs=pl.BlockSpec((1, bsz, tn), lambda l, j: (l, 0, j)),
        out_shape=jax.ShapeDtypeStruct((depth, bsz, n), F32),
        compiler_params=_params(("parallel", "parallel")),
        name="adaln_mod",
    )(c, w_ada, b_ada.reshape(depth, 1, n))


def _norm_mod(x, gain, sc, sh):
    ms = jnp.mean(x * x, axis=-1, keepdims=True)
    return x * lax.rsqrt(ms + NORM_EPS) * gain * (1.0 + sc) + sh


def _proj_kernel(x_ref, gain_ref, sc_ref, sh_ref, w_ref, o_ref, h_scr):
    @pl.when(pl.program_id(2) == 0)
    def _():
        h_scr[...] = _bf(_norm_mod(x_ref[0], gain_ref[...], sc_ref[0], sh_ref[0]))

    o_ref[0] = jnp.dot(h_scr[...], w_ref[...], preferred_element_type=F32).astype(o_ref.dtype)


def _pick_tile(n, candidates):
    for t in candidates:
        if n % t == 0:
            return t
    raise ValueError(f"no tile for {n}")


def _norm_proj(x, gain, sc, sh, w, out_dtype, name):
    bsz, s, d = x.shape
    n = w.shape[1]
    tm = 1024
    tn = _pick_tile(n, (768, 640, 512, 256, 128))
    return pl.pallas_call(
        _proj_kernel,
        grid=(bsz, s // tm, n // tn),
        in_specs=[
            pl.BlockSpec((1, tm, d), lambda b, i, j: (b, i, 0)),
            pl.BlockSpec((1, d), lambda b, i, j: (0, 0)),
            pl.BlockSpec((1, 1, d), lambda b, i, j: (b, 0, 0)),
            pl.BlockSpec((1, 1, d), lambda b, i, j: (b, 0, 0)),
            pl.BlockSpec((d, tn), lambda b, i, j: (0, j)),
        ],
        out_specs=pl.BlockSpec((1, tm, tn), lambda b, i, j: (b, i, j)),
        out_shape=jax.ShapeDtypeStruct((bsz, s, n), out_dtype),
        scratch_shapes=[pltpu.VMEM((tm, d), BF16)],
        compiler_params=_params(("parallel", "parallel", "arbitrary")),
        name=name,
    )(x, gain, sc, sh, w)


def _outproj_kernel(ya_ref, yb_ref, yc_ref, yd_ref, w_ref, x_ref, g_ref, o_ref):
    c = GROUP_WIDTH
    acc = jnp.dot(ya_ref[0], w_ref[0:c, :], preferred_element_type=F32)
    acc += jnp.dot(yb_ref[0], w_ref[c:2 * c, :], preferred_element_type=F32)
    acc += jnp.dot(yc_ref[0], w_ref[2 * c:3 * c, :], preferred_element_type=F32)
    acc += jnp.dot(yd_ref[0], w_ref[3 * c:4 * c, :], preferred_element_type=F32)
    o_ref[0] = x_ref[0] + g_ref[0] * acc


def _out_proj(ya, yb, yc, yd, w, x, g1):
    bsz, s, d = x.shape
    c = GROUP_WIDTH
    tm, tn = 1024, 1024
    yspec = pl.BlockSpec((1, tm, c), lambda b, i, j: (b, i, 0))
    return pl.pallas_call(
        _outproj_kernel,
        grid=(bsz, s // tm, d // tn),
        in_specs=[
            yspec, yspec, yspec, yspec,
            pl.BlockSpec((4 * c, tn), lambda b, i, j: (0, j)),
            pl.BlockSpec((1, tm, tn), lambda b, i, j: (b, i, j)),
            pl.BlockSpec((1, 1, tn), lambda b, i, j: (b, 0, j)),
        ],
        out_specs=pl.BlockSpec((1, tm, tn), lambda b, i, j: (b, i, j)),
        out_shape=jax.ShapeDtypeStruct((bsz, s, d), F32),
        compiler_params=_params(("parallel", "parallel", "arbitrary")),
        name="out_proj",
    )(ya, yb, yc, yd, w, x, g1)


def _ffn_kernel(x_ref, gain_ref, sc_ref, sh_ref, g_ref, w1_ref, w3_ref, w2_ref, fg_ref,
                o_ref, h_scr, acc_scr, *, final):
    f = pl.program_id(2)

    @pl.when(f == 0)
    def _():
        h_scr[...] = _bf(_norm_mod(x_ref[0], gain_ref[...], sc_ref[0], sh_ref[0]))
        acc_scr[...] = jnp.zeros_like(acc_scr)

    h = h_scr[...]
    gate = jnp.dot(h, w1_ref[...], preferred_element_type=F32)
    up = jnp.dot(h, w3_ref[...], preferred_element_type=F32)
    act = gate * _sigmoid(gate) * up
    acc_scr[...] += jnp.dot(_bf(act), w2_ref[...], preferred_element_type=F32)

    @pl.when(f == pl.num_programs(2) - 1)
    def _():
        xn = x_ref[0] + g_ref[0] * acc_scr[...]
        if final:
            ms = jnp.mean(xn * xn, axis=-1, keepdims=True)
            xn = xn * lax.rsqrt(ms + NORM_EPS) * fg_ref[...]
        o_ref[0] = xn


def _ffn(x, gain, sc, sh, g2, w13, w2, final_gain, final):
    bsz, s, d = x.shape
    dff = w2.shape[0]
    tm, tf = 512, 512
    nf = dff // tf
    vec = pl.BlockSpec((1, 1, d), lambda b, i, f: (b, 0, 0))
    return pl.pallas_call(
        functools.partial(_ffn_kernel, final=final),
        grid=(bsz, s // tm, nf),
        in_specs=[
            pl.BlockSpec((1, tm, d), lambda b, i, f: (b, i, 0)),
            pl.BlockSpec((1, d), lambda b, i, f: (0, 0)),
            vec, vec, vec,
            pl.BlockSpec((d, tf), lambda b, i, f: (0, f)),
            pl.BlockSpec((d, tf), lambda b, i, f: (0, f + nf)),
            pl.BlockSpec((tf, d), lambda b, i, f: (f, 0)),
            pl.BlockSpec((1, d), lambda b, i, f: (0, 0)),
        ],
        out_specs=pl.BlockSpec((1, tm, d), lambda b, i, f: (b, i, 0)),
        out_shape=jax.ShapeDtypeStruct((bsz, s, d), F32),
        scratch_shapes=[pltpu.VMEM((tm, d), BF16), pltpu.VMEM((tm, d), F32)],
        compiler_params=_params(("parallel", "parallel", "arbitrary")),
        name="ffn_final" if final else "ffn",
    )(x, gain, sc, sh, g2, w13, w13, w2, final_gain)


def _sb_kernel(q_ref, k_ref, v_ref, o_ref):
    qb = pl.program_id(2)
    scale = HEAD_DIM ** -0.5
    row = _iota2((BLOCK, BLOCK), 0)
    col = _iota2((BLOCK, BLOCK), 1)
    later = jnp.where(row > col, 1.0, 0.0).astype(BF16)
    past = col < row

    def tile(q, k, v, run, mask):
        z = _mm_nt(q, k) * scale
        sp = _softplus(z)
        l1m = -sp
        if mask is not None:
            l1m = jnp.where(mask, l1m, 0.0)
        between = run + _mm_split_lhs(l1m, later, 2)
        att = jnp.exp(z - sp + between)
        if mask is not None:
            att = jnp.where(mask, att, 0.0)
        contrib = jnp.dot(_bf(att), v, preferred_element_type=F32)
        return contrib, run + jnp.sum(l1m, axis=1, keepdims=True)

    for h in range(LANES // HEAD_DIM):
        sl = slice(h * HEAD_DIM, (h + 1) * HEAD_DIM)
        q = q_ref[0, :, sl]
        start = pl.multiple_of(qb * BLOCK, BLOCK)
        acc, run = tile(q, k_ref[0, pl.ds(start, BLOCK), sl], v_ref[0, pl.ds(start, BLOCK), sl],
                        jnp.zeros((BLOCK, 1), F32), past)

        def body(i, carry, q=q, sl=sl):
            acc, run = carry
            st = pl.multiple_of((qb - 1 - i) * BLOCK, BLOCK)
            contrib, run = tile(q, k_ref[0, pl.ds(st, BLOCK), sl], v_ref[0, pl.ds(st, BLOCK), sl],
                                run, None)
            return acc + contrib, run

        acc, _ = lax.fori_loop(0, qb, body, (acc, run))
        o_ref[0, :, sl] = acc.astype(o_ref.dtype)


def _stick_breaking(qkv, col0):
    bsz, s, _ = qkv.shape
    nhp = GROUP_WIDTH // LANES
    return pl.pallas_call(
        _sb_kernel,
        grid=(bsz, nhp, s // BLOCK),
        in_specs=[
            pl.BlockSpec((1, BLOCK, LANES), lambda b, h, i: (b, i, col0 + h)),
            pl.BlockSpec((1, s, LANES), lambda b, h, i: (b, 0, col0 + nhp + h)),
            pl.BlockSpec((1, s, LANES), lambda b, h, i: (b, 0, col0 + 2 * nhp + h)),
        ],
        out_specs=pl.BlockSpec((1, BLOCK, LANES), lambda b, h, i: (b, i, h)),
        out_shape=jax.ShapeDtypeStruct((bsz, s, GROUP_WIDTH), BF16),
        compiler_params=_params(("parallel", "parallel", "arbitrary")),
        name="stick_breaking",
    )(qkv, qkv, qkv)


def _diff_kernel(q_ref, k_ref, v_ref, bias_ref, lam_ref, subln_ref, o_ref, *, lam_init):
    qb = pl.program_id(2)
    scale = DIFF_HALF ** -0.5
    row = _iota2((BLOCK, BLOCK), 0)
    col = _iota2((BLOCK, BLOCK), 1)
    causal = col <= row
    q = q_ref[0]

    def tile(carry, st, bias, mask):
        k = k_ref[0, pl.ds(st, BLOCK), :]
        v = v_ref[0, pl.ds(st, BLOCK), :]
        out = []
        for c in range(2):
            m, l, acc = carry[c]
            sl = slice(c * DIFF_HALF, (c + 1) * DIFF_HALF)
            s = _mm_nt(q[:, sl], k[:, sl]) * scale + bias
            if mask is not None:
                s = jnp.where(mask, s, NEG_BIG)
            m_new = jnp.maximum(m, jnp.max(s, axis=1, keepdims=True))
            alpha = jnp.exp(m - m_new)
            p = jnp.exp(s - m_new)
            l = alpha * l + jnp.sum(p, axis=1, keepdims=True)
            acc = alpha * acc + jnp.dot(_bf(p), v, preferred_element_type=F32)
            out.append((m_new, l, acc))
        return tuple(out)

    def init():
        return (jnp.full((BLOCK, 1), NEG_BIG, F32), jnp.zeros((BLOCK, 1), F32),
                jnp.zeros((BLOCK, DIFF_VDIM), F32))

    def body(kb, carry):
        st = pl.multiple_of(kb * BLOCK, BLOCK)
        return tile(carry, st, bias_ref[0, qb - kb], None)

    carry = lax.fori_loop(0, qb, body, (init(), init()))
    carry = tile(carry, pl.multiple_of(qb * BLOCK, BLOCK), bias_ref[0, 0], causal)

    lp = lam_ref[...]
    lam = (jnp.exp(jnp.sum(lp[0:1] * lp[1:2], axis=1, keepdims=True))
           - jnp.exp(jnp.sum(lp[2:3] * lp[3:4], axis=1, keepdims=True)) + lam_init)
    (_, l0, a0), (_, l1, a1) = carry
    o = a0 / l0 - lam * (a1 / l1)
    ms = jnp.mean(o * o, axis=-1, keepdims=True)
    o = o * lax.rsqrt(ms + SUBLN_EPS) * subln_ref[...] * (1.0 - lam_init)
    o_ref[0] = o.astype(o_ref.dtype)


def _diff_attention(qkv, col0, bias_tiles, lam_params, subln, lam_init):
    bsz, s, _ = qkv.shape
    nb = s // BLOCK
    return pl.pallas_call(
        functools.partial(_diff_kernel, lam_init=lam_init),
        grid=(bsz, H_D, nb),
        in_specs=[
            pl.BlockSpec((1, BLOCK, LANES), lambda b, h, i: (b, i, col0 + h)),
            pl.BlockSpec((1, s, LANES), lambda b, h, i: (b, 0, col0 + H_D + h)),
            pl.BlockSpec((1, s, LANES), lambda b, h, i: (b, 0, col0 + 2 * H_D + h)),
            pl.BlockSpec((1, nb, BLOCK, BLOCK), lambda b, h, i: (h, 0, 0, 0)),
            pl.BlockSpec((4, DIFF_HALF), lambda b, h, i: (0, 0)),
            pl.BlockSpec((1, DIFF_VDIM), lambda b, h, i: (0, 0)),
        ],
        out_specs=pl.BlockSpec((1, BLOCK, LANES), lambda b, h, i: (b, i, h)),
        out_shape=jax.ShapeDtypeStruct((bsz, s, GROUP_WIDTH), BF16),
        compiler_params=_params(("parallel", "parallel", "arbitrary")),
        name="diff_attention",
    )(qkv, qkv, qkv, bias_tiles, lam_params, subln)


def _band_kernel(q_ref, kp_ref, kc_ref, vp_ref, vc_ref, bias_ref, o_ref, lse_ref, *, n_back):
    c = pl.program_id(1)
    scale = HEAD_DIM ** -0.5
    qi = _iota2((BLOCK, 2 * BLOCK), 0)
    ki = _iota2((BLOCK, 2 * BLOCK), 1)
    steps = qi + BLOCK - ki
    first_key = jnp.where(c > 0, 0, BLOCK)
    valid = (steps >= 0) & (steps <= n_back) & (ki >= first_key)
    for h in range(N_HEADS):
        sl = slice(h * HEAD_DIM, (h + 1) * HEAD_DIM)
        q = q_ref[0, :, sl]
        s = jnp.concatenate([_mm_nt(q, kp_ref[0, :, sl]), _mm_nt(q, kc_ref[0, :, sl])], axis=1)
        s = jnp.where(valid, s * scale + bias_ref[h], NEG_BIG)
        m = jnp.max(s, axis=1, keepdims=True)
        p = jnp.exp(s - m)
        l = jnp.sum(p, axis=1, keepdims=True)
        pb = _bf(p)
        o = (jnp.dot(pb[:, :BLOCK], vp_ref[0, :, sl], preferred_element_type=F32)
             + jnp.dot(pb[:, BLOCK:], vc_ref[0, :, sl], preferred_element_type=F32))
        o_ref[0, :, sl] = o / l
        lse_ref[0, :, sl] = jnp.broadcast_to(m + jnp.log(l), (BLOCK, HEAD_DIM))


def _band_attention(arr, col0, bias, n_back):
    g, length, _ = arr.shape
    nc = length // BLOCK
    w = GROUP_WIDTH
    prev = lambda i: jnp.maximum(i - 1, 0)
    out = jax.ShapeDtypeStruct((g, length, w), F32)
    return pl.pallas_call(
        functools.partial(_band_kernel, n_back=n_back),
        grid=(g, nc),
        in_specs=[
            pl.BlockSpec((1, BLOCK, w), lambda b, i: (b, i, col0)),
            pl.BlockSpec((1, BLOCK, w), lambda b, i: (b, prev(i), col0 + 1)),
            pl.BlockSpec((1, BLOCK, w), lambda b, i: (b, i, col0 + 1)),
            pl.BlockSpec((1, BLOCK, w), lambda b, i: (b, prev(i), col0 + 2)),
            pl.BlockSpec((1, BLOCK, w), lambda b, i: (b, i, col0 + 2)),
            pl.BlockSpec((N_HEADS, BLOCK, 2 * BLOCK), lambda b, i: (0, 0, 0)),
        ],
        out_specs=[pl.BlockSpec((1, BLOCK, w), lambda b, i: (b, i, 0)),
                   pl.BlockSpec((1, BLOCK, w), lambda b, i: (b, i, 0))],
        out_shape=[out, out],
        compiler_params=_params(("parallel", "arbitrary")),
        name=f"band_attention_{n_back}",
    )(arr, arr, arr, arr, arr, bias)


def _merge_kernel(o1, o2, o3, l1, l2, l3, y_ref):
    a, b, c = l1[0], l2[0], l3[0]
    m = jnp.maximum(jnp.maximum(a, b), c)
    ea, eb, ec = jnp.exp(a - m), jnp.exp(b - m), jnp.exp(c - m)
    y = (ea * o1[0] + eb * o2[0] + ec * o3[0]) / (ea + eb + ec)
    y_ref[0] = y.astype(y_ref.dtype)


def _merge_patterns(outs, lses):
    bsz, s, w = outs[0].shape
    tm = 512
    spec = pl.BlockSpec((1, tm, w), lambda b, i: (b, i, 0))
    return pl.pallas_call(
        _merge_kernel,
        grid=(bsz, s // tm),
        in_specs=[spec] * 6,
        out_specs=spec,
        out_shape=jax.ShapeDtypeStruct((bsz, s, w), BF16),
        compiler_params=_params(("parallel", "parallel")),
        name="merge_dilations",
    )(*outs, *lses)


def _gather_stride(t, dil):
    b, s, w = t.shape
    length = s // dil
    return t.reshape(b, length, dil, w).transpose(0, 2, 1, 3).reshape(b * dil, length, w)


def _scatter_stride(t, dil, bsz):
    g, length, w = t.shape
    return t.reshape(bsz, dil, length, w).transpose(0, 2, 1, 3).reshape(bsz, length * dil, w)


def _dilated_attention(qkv, col0, biases):
    bsz = qkv.shape[0]
    w = GROUP_WIDTH
    outs, lses = [], []
    for (window, dil), bias in zip(DILATIONS, biases):
        n_back = window // dil
        if dil == 1:
            o, lse = _band_attention(qkv, col0, bias, n_back)
        else:
            slab = _gather_stride(qkv[:, :, col0 * w:(col0 + 3) * w], dil)
            o, lse = _band_attention(slab, 0, bias, n_back)
            o, lse = _scatter_stride(o, dil, bsz), _scatter_stride(lse, dil, bsz)
        outs.append(o)
        lses.append(lse)
    return _merge_patterns(outs, lses)


def _unit_lower_inverse(n):
    row = _iota2((CHUNK, CHUNK), 0)
    col = _iota2((CHUNK, CHUNK), 1)
    eye = jnp.where(row == col, 1.0, 0.0)
    base = 8
    n8 = jnp.where((row >> 3) == (col >> 3), n, 0.0)
    n2 = _mm(n8, n8)
    n4 = _mm(n2, n2)
    t = eye + n8
    t = t + _mm(t, n2)
    t = t + _mm(t, n4)
    size = 2 * base
    while size <= CHUNK:
        half = size // 2
        shift = size.bit_length() - 1
        off = (((row >> shift) == (col >> shift)) & ((row & (size - 1)) >= half)
               & ((col & (size - 1)) < half))
        t = t + _mm(_mm(t, jnp.where(off, n, 0.0)), t)
        size *= 2
    return t


def _rwkv_kernel(*refs, has_vres):
    if has_vres:
        (pa_ref, mu_ref, w0_ref, wup_ref, a0_ref, aup_ref, gup_ref, kk_ref, ka_ref, rk_ref,
         lnw_ref, lnb_ref, seg_ref, vf_ref, vup_ref, vb_ref,
         y_ref, prev_scr, s_scr, r_s, k_s, v_s, lw_s, am_s, b_s, y_s) = refs
    else:
        (pa_ref, mu_ref, w0_ref, wup_ref, a0_ref, aup_ref, gup_ref, kk_ref, ka_ref, rk_ref,
         lnw_ref, lnb_ref, seg_ref,
         y_ref, vf_out_ref, prev_scr, s_scr, r_s, k_s, v_s, lw_s, am_s, b_s, y_s) = refs
    c = GROUP_WIDTH
    tt = pa_ref.shape[1]

    @pl.when(pl.program_id(1) == 0)
    def _():
        prev_scr[...] = jnp.zeros_like(prev_scr)
        s_scr[...] = jnp.zeros_like(s_scr)

    seg = seg_ref[...]
    seg_sum = lambda t: _mm_split_lhs(t, seg, 2)

    pa = pa_ref[0]
    rolled = pltpu.roll(pa, 1, 0)
    first = _iota2(pa.shape, 0) == 0
    prev = jnp.where(first, prev_scr[...], rolled)
    prev_scr[...] = pa[tt - 1:tt, :]
    sh = pa + (prev - pa) * mu_ref[...]
    r = sh[:, 0:c]
    k = sh[:, c:2 * c]
    v = sh[:, 2 * c:3 * c]
    w_lo = sh[:, A_OFF_W:A_OFF_W + LORA_PAD]
    a_lo = sh[:, A_OFF_A:A_OFF_A + LORA_PAD]
    g_lo = sh[:, A_OFF_G:A_OFF_G + G_PAD]
    if has_vres:
        pv = sh[:, A_OFF_V:A_OFF_V + LORA_PAD]
        v = v + (vf_ref[0] - v) * _sigmoid(vb_ref[...] + _mm(pv, vup_ref[...]))
    else:
        vf_out_ref[0] = v

    w_log = -_softplus(-(w0_ref[...] + _mm(jnp.tanh(w_lo), wup_ref[...]))) - 0.5
    lw = -jnp.exp(w_log)
    a = _sigmoid(a0_ref[...] + _mm(a_lo, aup_ref[...]))
    g = _mm(_sigmoid(g_lo), gup_ref[...])
    kkr = k * kk_ref[...]
    kk = kkr * lax.rsqrt(jnp.maximum(seg_sum(kkr * kkr), 1e-24))
    k = k * (1.0 + (a - 1.0) * ka_ref[...])
    r_s[...] = r
    k_s[...] = k
    v_s[...] = v
    lw_s[...] = lw
    am_s[...] = -kk
    b_s[...] = kk * a

    row = _iota2((CHUNK, CHUNK), 0)
    col = _iota2((CHUNK, CHUNK), 1)
    strict = row > col
    lower = row >= col
    eye = row == col
    lower_ones = jnp.where(lower, 1.0, 0.0).astype(BF16)

    def chunk_body(ci, carry):
        rows = pl.ds(pl.multiple_of(ci * CHUNK, CHUNK), CHUNK)
        lw_c = lw_s[rows, :]
        cum = _mm_split_rhs(lower_ones, lw_c, 3)
        eg = jnp.exp(cum)
        eig = jnp.exp(-cum)
        at = am_s[rows, :] * jnp.exp(cum - lw_c)
        rt = r_s[rows, :] * eg
        bt = b_s[rows, :] * eig
        kt = k_s[rows, :] * eig
        g_end = eg[CHUNK - 1:CHUNK, :]
        bg = bt * g_end
        kg = kt * g_end
        vv = v_s[rows, :]
        for h in range(N_HEADS):
            sl = slice(h * HEAD_DIM, (h + 1) * HEAD_DIM)
            at_h, rt_h, v_h, bg_h = at[:, sl], rt[:, sl], vv[:, sl], bg[:, sl]
            gram = _mm_nt(jnp.concatenate([at_h, rt_h], axis=0),
                          jnp.concatenate([bt[:, sl], kt[:, sl]], axis=0))
            a_ab = jnp.where(strict, gram[:CHUNK, :CHUNK], 0.0)
            a_ak = jnp.where(strict, gram[:CHUNK, CHUNK:], 0.0)
            a_rb = jnp.where(lower, gram[CHUNK:, :CHUNK], 0.0)
            a_rk = jnp.where(lower, gram[CHUNK:, CHUNK:], 0.0)
            t = _unit_lower_inverse(a_ab)
            p = _mm(t, at_h)
            q = _mm(t, _mm(a_ak, v_h))
            r2 = rt_h + _mm(a_rb, p)
            y0 = _mm(a_rb, q) + _mm(a_rk, v_h)
            m = jnp.where(eye, jnp.broadcast_to(g_end[:, sl], (CHUNK, CHUNK)), 0.0) + _mm_tn(p, bg_h)
            z = _mm_tn(q, bg_h) + _mm_tn(v_h, kg[:, sl])
            s0 = s_scr[h]
            y_s[rows, sl] = _mm_nt(r2, s0) + y0
            s_scr[h] = _mm(s0, m) + z
        return carry

    lax.fori_loop(0, tt // CHUNK, chunk_body, 0)

    y = y_s[...]
    mu = seg_sum(y) * (1.0 / HEAD_DIM)
    d = y - mu
    var = seg_sum(d * d) * (1.0 / HEAD_DIM)
    yn = d * lax.rsqrt(var + RWKV_LN_EPS) * lnw_ref[...] + lnb_ref[...]
    bonus = seg_sum(r_s[...] * k_s[...] * rk_ref[...]) * v_s[...]
    y_ref[0] = ((yn + bonus) * g).astype(y_ref.dtype)


def _rwkv7(pa, prm, v_first):
    bsz, s, na = pa.shape
    c = GROUP_WIDTH
    tt = RWKV_TILE
    has_vres = v_first is not None
    full = lambda shape: pl.BlockSpec(shape, lambda b, t: (0,) * len(shape))
    tile = lambda w: pl.BlockSpec((1, tt, w), lambda b, t: (b, t, 0))
    in_specs = [tile(na), full((1, na)), full((1, c)), full((LORA_PAD, c)), full((1, c)),
                full((LORA_PAD, c)), full((G_PAD, c)), full((1, c)), full((1, c)), full((1, c)),
                full((1, c)), full((1, c)), full((c, c))]
    args = [pa, prm["mu"], prm["w0"], prm["w_up"], prm["a0"], prm["a_up"], prm["g_up"],
            prm["k_k"], prm["k_a"], prm["r_k"], prm["ln_w"], prm["ln_b"], prm["seg"]]
    y_shape = jax.ShapeDtypeStruct((bsz, s, c), BF16)
    if has_vres:
        in_specs += [tile(c), full((LORA_PAD, c)), full((1, c))]
        args += [v_first, prm["vres_up"], prm["vres_bias"]]
        out_specs, out_shape = tile(c), y_shape
    else:
        out_specs = [tile(c), tile(c)]
        out_shape = [y_shape, jax.ShapeDtypeStruct((bsz, s, c), F32)]
    tile_scr = pltpu.VMEM((tt, c), F32)
    res = pl.pallas_call(
        functools.partial(_rwkv_kernel, has_vres=has_vres),
        grid=(bsz, s // tt),
        in_specs=in_specs,
        out_specs=out_specs,
        out_shape=out_shape,
        scratch_shapes=[pltpu.VMEM((1, na), F32), pltpu.VMEM((N_HEADS, HEAD_DIM, HEAD_DIM), F32)]
        + [tile_scr] * 7,
        compiler_params=_params(("parallel", "arbitrary")),
        name="rwkv7_vres" if has_vres else "rwkv7",
    )(*args)
    if has_vres:
        return res, v_first
    return res[0], res[1]


def _rel_bucket(dist):
    max_exact = NUM_BUCKETS // 2
    d_f = jnp.maximum(dist, 1).astype(F32)
    large = max_exact + (jnp.log(d_f / max_exact) / math.log(MAX_DISTANCE / max_exact)
                         * (NUM_BUCKETS - max_exact)).astype(jnp.int32)
    large = jnp.minimum(large, NUM_BUCKETS - 1)
    return jnp.where(dist < max_exact, dist, large)


def _diff_bias_tiles(rel_tab, nb):
    qi = jnp.arange(BLOCK)[:, None]
    ki = jnp.arange(BLOCK)[None, :]
    dist = jnp.arange(nb)[:, None, None] * BLOCK + (qi - ki)[None]
    bias = rel_tab[_rel_bucket(jnp.maximum(dist, 0))]
    return jnp.transpose(bias, (3, 0, 1, 2)).astype(F32)


def _band_bias(rel_tab, dil):
    qi = jnp.arange(BLOCK)[:, None]
    ki = jnp.arange(2 * BLOCK)[None, :]
    steps = qi + BLOCK - ki
    bias = rel_tab[_rel_bucket(jnp.maximum(steps, 0) * dil)]
    return jnp.transpose(bias, (2, 0, 1)).astype(F32)


def _pad_axis(t, size, axis):
    pad = [(0, 0)] * t.ndim
    pad[axis] = (0, size - t.shape[axis])
    return jnp.pad(t, pad)


def _layout_mixer_a(t, vres=None):
    c = GROUP_WIDTH
    o_w, o_a, o_g = 3 * c, 3 * c + W_LORA, 3 * c + W_LORA + A_LORA
    parts = [t[..., :o_w], _pad_axis(t[..., o_w:o_a], LORA_PAD, -1),
             _pad_axis(t[..., o_a:o_g], LORA_PAD, -1), _pad_axis(t[..., o_g:N_A_COLS], G_PAD, -1)]
    if vres is not None:
        parts.append(_pad_axis(vres, 2 * LORA_PAD, -1))
    return jnp.concatenate(parts, axis=-1)


def kernel(x, c, w_ada, b_ada, norm_gain, w_in, w_out, rel_bias, rwkv_mu, rwkv_w0, rwkv_w_up,
           rwkv_a0, rwkv_a_up, rwkv_g_up, rwkv_k_k, rwkv_k_a, rwkv_r_k, rwkv_ln_w, rwkv_ln_b,
           vres_down, vres_mu, vres_up, vres_bias, diff_lambda, diff_subln, ffn_w13, ffn_w2,
           final_gain):
    out_dtype = x.dtype
    bsz, s, d = x.shape
    cw = GROUP_WIDTH
    x = x.astype(F32)
    row = lambda t: t.reshape(1, -1).astype(F32)

    mod = _modulation(c.astype(F32), w_ada, b_ada)
    rel_tab_c = rel_bias[:, :N_HEADS].astype(F32)
    rel_tab_d = rel_bias[:, N_HEADS:].astype(F32)
    band_biases = [_band_bias(rel_tab_c, dil) for _, dil in DILATIONS]
    diff_bias = _diff_bias_tiles(rel_tab_d, s // BLOCK)
    seg = (jnp.arange(cw)[:, None] // HEAD_DIM == jnp.arange(cw)[None, :] // HEAD_DIM).astype(BF16)

    v_first = None
    for l in range(DEPTH):
        sh1, sc1, g1, sh2, sc2, g2 = [mod[l, :, None, i * d:(i + 1) * d] for i in range(6)]
        gain1, gain2 = row(norm_gain[l, 0]), row(norm_gain[l, 1])
        vres_w = vres_down[l - 1] if l > 0 else None
        w_a = _bf(_layout_mixer_a(w_in[l][:, :N_A_COLS], vres_w))
        w_bcd = _bf(w_in[l][:, N_A_COLS:N_IN])
        mu = _layout_mixer_a(rwkv_mu[l], vres_mu[l - 1] if l > 0 else None)

        pa = _norm_proj(x, gain1, sc1, sh1, w_a, F32, "proj_a")
        qkv = _norm_proj(x, gain1, sc1, sh1, w_bcd, BF16, "proj_bcd")

        prm = {
            "mu": row(mu), "w0": row(rwkv_w0[l]), "a0": row(rwkv_a0[l]),
            "w_up": _bf(_pad_axis(rwkv_w_up[l], LORA_PAD, 0)),
            "a_up": _bf(_pad_axis(rwkv_a_up[l], LORA_PAD, 0)),
            "g_up": _bf(_pad_axis(rwkv_g_up[l], G_PAD, 0)),
            "k_k": row(rwkv_k_k[l]), "k_a": row(rwkv_k_a[l]), "r_k": row(rwkv_r_k[l]),
            "ln_w": row(rwkv_ln_w[l]), "ln_b": row(rwkv_ln_b[l]), "seg": seg,
        }
        if l > 0:
            prm["vres_up"] = _bf(_pad_axis(vres_up[l - 1], LORA_PAD, 0))
            prm["vres_bias"] = row(vres_bias[l - 1])
        y_a, v_first = _rwkv7(pa, prm, v_first)

        y_b = _stick_breaking(qkv, 0)
        y_c = _dilated_attention(qkv, 3, band_biases)
        lam_init = 0.8 - 0.6 * math.exp(-0.3 * l)
        y_d = _diff_attention(qkv, 6 * cw // LANES, diff_bias, diff_lambda[l].astype(F32),
                              row(diff_subln[l]), lam_init)

        x = _out_proj(y_a, y_b, y_c, y_d, _bf(w_out[l]), x, g1)
        x = _ffn(x, gain2, sc2, sh2, g2, _bf(ffn_w13[l]), _bf(ffn_w2[l]), row(final_gain),
                 final=(l == DEPTH - 1))
    return x.astype(out_dtype)
```

```python
import functools
import math

import jax
import jax.numpy as jnp
from jax import lax
from jax.experimental import pallas as pl
from jax.experimental.pallas import tpu as pltpu

F32 = jnp.float32
BF16 = jnp.bfloat16

D_MODEL = 2048
DEPTH = 2
HEAD_DIM = 64
GROUP_WIDTH = D_MODEL // 4
N_HEADS = GROUP_WIDTH // HEAD_DIM
DIFF_HALF = HEAD_DIM
DIFF_VDIM = 2 * DIFF_HALF
H_D = GROUP_WIDTH // DIFF_VDIM
W_LORA = max(32, int(round(D_MODEL ** 0.5 * 1.8 / 32)) * 32)
A_LORA = max(32, int(round(D_MODEL ** 0.5 * 1.8 / 32)) * 32)
V_LORA = max(32, int(round(D_MODEL ** 0.5 * 1.3 / 32)) * 32)
G_LORA = max(32, int(round(D_MODEL ** 0.8 / 32)) * 32)
N_A_COLS = 3 * GROUP_WIDTH + W_LORA + A_LORA + G_LORA
N_IN = N_A_COLS + 9 * GROUP_WIDTH
D_FF = ((8 * D_MODEL + 3 * 256 - 1) // (3 * 256)) * 256
DILATIONS = ((128, 1), (512, 4), (2048, 16))
BLOCK = 128
NUM_BUCKETS = 32
MAX_DISTANCE = 2048
NORM_EPS = 1e-6
RWKV_LN_EPS = 64e-5
SUBLN_EPS = 1e-5

LANES = 128
VMEM_LIMIT_BYTES = 56 * 1024 * 1024

LORA_PAD = 128
G_PAD = 512
A_OFF_W = 3 * GROUP_WIDTH
A_OFF_A = A_OFF_W + LORA_PAD
A_OFF_G = A_OFF_A + LORA_PAD
A_OFF_V = A_OFF_G + G_PAD
NA_L0 = A_OFF_V
NA_LN = A_OFF_V + 2 * LORA_PAD
CHUNK = 64
RWKV_TILE = 256
SPAN_BLOCKS = 4
SPAN = SPAN_BLOCKS * BLOCK
NEG_BIG = -1e30


def _bf(x):
    return x.astype(BF16)


def _mm(a, b):
    return jnp.dot(_bf(a), _bf(b), preferred_element_type=F32)


def _mm_nt(a, b):
    return lax.dot_general(_bf(a), _bf(b), (((1,), (1,)), ((), ())), preferred_element_type=F32)


def _mm_tn(a, b):
    return lax.dot_general(_bf(a), _bf(b), (((0,), (0,)), ((), ())), preferred_element_type=F32)


def _split_bf16(x, n):
    parts = []
    rem = x
    for i in range(n):
        p = _bf(rem)
        parts.append(p)
        if i + 1 < n:
            rem = rem - p.astype(F32)
    return parts


def _mm_split_lhs(a, b_bf16, n):
    out = None
    for p in _split_bf16(a, n):
        t = jnp.dot(p, b_bf16, preferred_element_type=F32)
        out = t if out is None else out + t
    return out


def _mm_split_rhs(a_bf16, b, n):
    out = None
    for p in _split_bf16(b, n):
        t = jnp.dot(a_bf16, p, preferred_element_type=F32)
        out = t if out is None else out + t
    return out


def _softplus(x):
    return jnp.maximum(x, 0.0) + jnp.log1p(jnp.exp(-jnp.abs(x)))


def _sigmoid(x):
    return jax.nn.sigmoid(x)


def _iota2(shape, dim):
    return lax.broadcasted_iota(jnp.int32, shape, dim)


def _params(sem, vmem=VMEM_LIMIT_BYTES):
    return pltpu.CompilerParams(dimension_semantics=sem, vmem_limit_bytes=vmem)


def _mod_kernel(c_ref, w_ref, b_ref, o_ref):
    c = c_ref[...]
    cond = c * _sigmoid(c)
    o_ref[0] = _mm(cond, w_ref[0]) + b_ref[0]


def _modulation(c, w_ada, b_ada):
    depth, d, n = w_ada.shape
    bsz = c.shape[0]
    tn = 1024
    return pl.pallas_call(
        _mod_kernel,
        grid=(depth, n // tn),
        in_specs=[
            pl.BlockSpec((bsz, d), lambda l, j: (0, 0)),
            pl.BlockSpec((1, d, tn), lambda l, j: (l, 0, j)),
            pl.BlockSpec((1, 1, tn), lambda l, j: (l, 0, j)),
        ],
        out_specs=pl.BlockSpec((1, bsz, tn), lambda l, j: (l, 0, j)),
        out_shape=jax.ShapeDtypeStruct((depth, bsz, n), F32),
        compiler_params=_params(("parallel", "parallel")),
        name="adaln_mod",
    )(c, w_ada, b_ada.reshape(depth, 1, n))


def _norm_mod(x, gain, sc, sh):
    ms = jnp.mean(x * x, axis=-1, keepdims=True)
    return x * lax.rsqrt(ms + NORM_EPS) * gain * (1.0 + sc) + sh


def _proj_kernel(x_ref, gain_ref, sc_ref, sh_ref, w_ref, o_ref, h_scr):
    @pl.when(pl.program_id(2) == 0)
    def _():
        h_scr[...] = _bf(_norm_mod(x_ref[0], gain_ref[...], sc_ref[0], sh_ref[0]))

    o_ref[0] = jnp.dot(h_scr[...], w_ref[...], preferred_element_type=F32).astype(o_ref.dtype)


def _pick_tile(n, candidates):
    for t in candidates:
        if n % t == 0:
            return t
    raise ValueError(f"no tile for {n}")


def _norm_proj(x, gain, sc, sh, w, out_dtype, name):
    bsz, s, d = x.shape
    n = w.shape[1]
    tm = 1024
    tn = _pick_tile(n, (768, 640, 512, 256, 128))
    return pl.pallas_call(
        _proj_kernel,
        grid=(bsz, s // tm, n // tn),
        in_specs=[
            pl.BlockSpec((1, tm, d), lambda b, i, j: (b, i, 0)),
            pl.BlockSpec((1, d), lambda b, i, j: (0, 0)),
            pl.BlockSpec((1, 1, d), lambda b, i, j: (b, 0, 0)),
            pl.BlockSpec((1, 1, d), lambda b, i, j: (b, 0, 0)),
            pl.BlockSpec((d, tn), lambda b, i, j: (0, j)),
        ],
        out_specs=pl.BlockSpec((1, tm, tn), lambda b, i, j: (b, i, j)),
        out_shape=jax.ShapeDtypeStruct((bsz, s, n), out_dtype),
        scratch_shapes=[pltpu.VMEM((tm, d), BF16)],
        compiler_params=_params(("parallel", "parallel", "arbitrary")),
        name=name,
    )(x, gain, sc, sh, w)


def _outproj_kernel(ya_ref, yb_ref, yc_ref, yd_ref, w_ref, x_ref, g_ref, o_ref):
    c = GROUP_WIDTH
    acc = jnp.dot(ya_ref[0], w_ref[0:c, :], preferred_element_type=F32)
    acc += jnp.dot(yb_ref[0], w_ref[c:2 * c, :], preferred_element_type=F32)
    acc += jnp.dot(yc_ref[0], w_ref[2 * c:3 * c, :], preferred_element_type=F32)
    acc += jnp.dot(yd_ref[0], w_ref[3 * c:4 * c, :], preferred_element_type=F32)
    o_ref[0] = x_ref[0] + g_ref[0] * acc


def _out_proj(ya, yb, yc, yd, w, x, g1):
    bsz, s, d = x.shape
    c = GROUP_WIDTH
    tm, tn = 1024, 1024
    yspec = pl.BlockSpec((1, tm, c), lambda b, i, j: (b, i, 0))
    return pl.pallas_call(
        _outproj_kernel,
        grid=(bsz, s // tm, d // tn),
        in_specs=[
            yspec, yspec, yspec, yspec,
            pl.BlockSpec((4 * c, tn), lambda b, i, j: (0, j)),
            pl.BlockSpec((1, tm, tn), lambda b, i, j: (b, i, j)),
            pl.BlockSpec((1, 1, tn), lambda b, i, j: (b, 0, j)),
        ],
        out_specs=pl.BlockSpec((1, tm, tn), lambda b, i, j: (b, i, j)),
        out_shape=jax.ShapeDtypeStruct((bsz, s, d), F32),
        compiler_params=_params(("parallel", "parallel", "arbitrary")),
        name="out_proj",
    )(ya, yb, yc, yd, w, x, g1)


def _ffn_kernel(x_ref, gain_ref, sc_ref, sh_ref, g_ref, w1_ref, w3_ref, w2_ref, fg_ref,
                o_ref, h_scr, acc_scr, *, final):
    f = pl.program_id(2)

    @pl.when(f == 0)
    def _():
        h_scr[...] = _bf(_norm_mod(x_ref[0], gain_ref[...], sc_ref[0], sh_ref[0]))
        acc_scr[...] = jnp.zeros_like(acc_scr)

    h = h_scr[...]
    gate = jnp.dot(h, w1_ref[...], preferred_element_type=F32)
    up = jnp.dot(h, w3_ref[...], preferred_element_type=F32)
    act = gate * _sigmoid(gate) * up
    acc_scr[...] += jnp.dot(_bf(act), w2_ref[...], preferred_element_type=F32)

    @pl.when(f == pl.num_programs(2) - 1)
    def _():
        xn = x_ref[0] + g_ref[0] * acc_scr[...]
        if final:
            ms = jnp.mean(xn * xn, axis=-1, keepdims=True)
            xn = xn * lax.rsqrt(ms + NORM_EPS) * fg_ref[...]
        o_ref[0] = xn


def _ffn(x, gain, sc, sh, g2, w13, w2, final_gain, final):
    bsz, s, d = x.shape
    dff = w2.shape[0]
    tm, tf = 512, 512
    nf = dff // tf
    vec = pl.BlockSpec((1, 1, d), lambda b, i, f: (b, 0, 0))
    return pl.pallas_call(
        functools.partial(_ffn_kernel, final=final),
        grid=(bsz, s // tm, nf),
        in_specs=[
            pl.BlockSpec((1, tm, d), lambda b, i, f: (b, i, 0)),
            pl.BlockSpec((1, d), lambda b, i, f: (0, 0)),
            vec, vec, vec,
            pl.BlockSpec((d, tf), lambda b, i, f: (0, f)),
            pl.BlockSpec((d, tf), lambda b, i, f: (0, f + nf)),
            pl.BlockSpec((tf, d), lambda b, i, f: (f, 0)),
            pl.BlockSpec((1, d), lambda b, i, f: (0, 0)),
        ],
        out_specs=pl.BlockSpec((1, tm, d), lambda b, i, f: (b, i, 0)),
        out_shape=jax.ShapeDtypeStruct((bsz, s, d), F32),
        scratch_shapes=[pltpu.VMEM((tm, d), BF16), pltpu.VMEM((tm, d), F32)],
        compiler_params=_params(("parallel", "parallel", "arbitrary")),
        name="ffn_final" if final else "ffn",
    )(x, gain, sc, sh, g2, w13, w13, w2, final_gain)


def _sb_kernel(q_ref, k_ref, v_ref, later_ref, o_ref):
    qs = pl.program_id(2)
    scale = HEAD_DIM ** -0.5
    nh = LANES // HEAD_DIM
    later = later_ref[...]
    row = _iota2((SPAN, BLOCK), 0)
    col = _iota2((SPAN, BLOCK), 1)

    def span_tile(q, sl, span, run, masked):
        st = pl.multiple_of(span * SPAN, SPAN)
        z = _mm_nt(q, k_ref[0, pl.ds(st, SPAN), sl])
        subs = []
        for j in range(SPAN_BLOCKS):
            zj = z[:, j * BLOCK:(j + 1) * BLOCK]
            spj = _softplus(zj)
            l1m = -spj
            mask = None
            if masked:
                mask = (j * BLOCK + col) < row
                l1m = jnp.where(mask, l1m, 0.0)
            subs.append((zj - spj, l1m, mask))
        pieces = [_split_bf16(l1m, 2) for _, l1m, _ in subs]
        stacked = jnp.concatenate([p[0] for p in pieces] + [p[1] for p in pieces], axis=0)
        within = jnp.dot(stacked, later, preferred_element_type=F32)
        atts = [None] * SPAN_BLOCKS
        for j in reversed(range(SPAN_BLOCKS)):
            log_beta, l1m, mask = subs[j]
            w = (within[j * SPAN:(j + 1) * SPAN]
                 + within[(SPAN_BLOCKS + j) * SPAN:(SPAN_BLOCKS + j + 1) * SPAN])
            att = jnp.exp(log_beta + w + run)
            if masked:
                att = jnp.where(mask, att, 0.0)
            atts[j] = _bf(att)
            run = run + jnp.sum(l1m, axis=1, keepdims=True)
        contrib = jnp.dot(jnp.concatenate(atts, axis=1), v_ref[0, pl.ds(st, SPAN), sl],
                          preferred_element_type=F32)
        return contrib, run

    heads = []
    for h in range(nh):
        sl = slice(h * HEAD_DIM, (h + 1) * HEAD_DIM)
        heads.append((q_ref[0, :, sl] * scale, sl))

    def step(span, carry, masked):
        out = []
        for (qh, sl), (acc, run) in zip(heads, carry):
            contrib, run = span_tile(qh, sl, span, run, masked)
            out.append((acc + contrib, run))
        return tuple(out)

    init = tuple((jnp.zeros((SPAN, HEAD_DIM), F32), jnp.zeros((SPAN, 1), F32)) for _ in range(nh))
    carry = step(qs, init, True)
    carry = lax.fori_loop(0, qs, lambda i, c: step(qs - 1 - i, c, False), carry)
    for (_, sl), (acc, _) in zip(heads, carry):
        o_ref[0, :, sl] = acc.astype(o_ref.dtype)


def _stick_breaking(qkv, col0):
    bsz, s, _ = qkv.shape
    nhp = GROUP_WIDTH // LANES
    later = jnp.tril(jnp.ones((BLOCK, BLOCK), F32), -1).astype(BF16)
    return pl.pallas_call(
        _sb_kernel,
        grid=(bsz, nhp, s // SPAN),
        in_specs=[
            pl.BlockSpec((1, SPAN, LANES), lambda b, h, i: (b, i, col0 + h)),
            pl.BlockSpec((1, s, LANES), lambda b, h, i: (b, 0, col0 + nhp + h)),
            pl.BlockSpec((1, s, LANES), lambda b, h, i: (b, 0, col0 + 2 * nhp + h)),
            pl.BlockSpec((BLOCK, BLOCK), lambda b, h, i: (0, 0)),
        ],
        out_specs=pl.BlockSpec((1, SPAN, LANES), lambda b, h, i: (b, i, h)),
        out_shape=jax.ShapeDtypeStruct((bsz, s, GROUP_WIDTH), BF16),
        compiler_params=_params(("parallel", "parallel", "arbitrary")),
        name="stick_breaking",
    )(qkv, qkv, qkv, later)


def _diff_kernel(q_ref, k_ref, v_ref, bias_ref, lam_ref, subln_ref, o_ref, *, lam_init):
    qs = pl.program_id(2)
    scale = DIFF_HALF ** -0.5
    causal = _iota2((SPAN, SPAN), 1) <= _iota2((SPAN, SPAN), 0)
    halves = [slice(c * DIFF_HALF, (c + 1) * DIFF_HALF) for c in range(2)]
    q = [q_ref[0, :, sl] * scale for sl in halves]

    def step(span, carry, masked):
        st = pl.multiple_of(span * SPAN, SPAN)
        bias = bias_ref[0, qs - span]
        v = v_ref[0, pl.ds(st, SPAN), :]
        out = []
        for c in range(2):
            m, l, acc = carry[c]
            s = _mm_nt(q[c], k_ref[0, pl.ds(st, SPAN), halves[c]]) + bias
            if masked:
                s = jnp.where(causal, s, NEG_BIG)
            m_new = jnp.maximum(m, jnp.max(s, axis=1, keepdims=True))
            alpha = jnp.exp(m - m_new)
            p = jnp.exp(s - m_new)
            l = alpha * l + jnp.sum(p, axis=1, keepdims=True)
            acc = alpha * acc + jnp.dot(_bf(p), v, preferred_element_type=F32)
            out.append((m_new, l, acc))
        return tuple(out)

    def init():
        return (jnp.full((SPAN, 1), NEG_BIG, F32), jnp.zeros((SPAN, 1), F32),
                jnp.zeros((SPAN, DIFF_VDIM), F32))

    carry = step(qs, (init(), init()), True)
    carry = lax.fori_loop(0, qs, lambda i, c: step(i, c, False), carry)

    lp = lam_ref[...]
    lam = (jnp.exp(jnp.sum(lp[0:1] * lp[1:2], axis=1, keepdims=True))
           - jnp.exp(jnp.sum(lp[2:3] * lp[3:4], axis=1, keepdims=True)) + lam_init)
    (_, l0, a0), (_, l1, a1) = carry
    o = a0 / l0 - lam * (a1 / l1)
    ms = jnp.mean(o * o, axis=-1, keepdims=True)
    o = o * lax.rsqrt(ms + SUBLN_EPS) * subln_ref[...] * (1.0 - lam_init)
    o_ref[0] = o.astype(o_ref.dtype)


def _diff_attention(qkv, col0, bias_tiles, lam_params, subln, lam_init):
    bsz, s, _ = qkv.shape
    ns = s // SPAN
    return pl.pallas_call(
        functools.partial(_diff_kernel, lam_init=lam_init),
        grid=(bsz, H_D, ns),
        in_specs=[
            pl.BlockSpec((1, SPAN, LANES), lambda b, h, i: (b, i, col0 + h)),
            pl.BlockSpec((1, s, LANES), lambda b, h, i: (b, 0, col0 + H_D + h)),
            pl.BlockSpec((1, s, LANES), lambda b, h, i: (b, 0, col0 + 2 * H_D + h)),
            pl.BlockSpec((1, ns, SPAN, SPAN), lambda b, h, i: (h, 0, 0, 0)),
            pl.BlockSpec((4, DIFF_HALF), lambda b, h, i: (0, 0)),
            pl.BlockSpec((1, DIFF_VDIM), lambda b, h, i: (0, 0)),
        ],
        out_specs=pl.BlockSpec((1, SPAN, LANES), lambda b, h, i: (b, i, h)),
        out_shape=jax.ShapeDtypeStruct((bsz, s, GROUP_WIDTH), BF16),
        compiler_params=_params(("parallel", "parallel", "arbitrary")),
        name="diff_attention",
    )(qkv, qkv, qkv, bias_tiles, lam_params, subln)


def _band_kernel(q_ref, kp_ref, kc_ref, vp_ref, vc_ref, bias_ref, o_ref, lse_ref, *, n_back):
    c = pl.program_id(1)
    scale = HEAD_DIM ** -0.5
    qi = _iota2((BLOCK, 2 * BLOCK), 0)
    ki = _iota2((BLOCK, 2 * BLOCK), 1)
    steps = qi + BLOCK - ki
    first_key = jnp.where(c > 0, 0, BLOCK)
    valid = (steps >= 0) & (steps <= n_back) & (ki >= first_key)
    for h in range(N_HEADS):
        sl = slice(h * HEAD_DIM, (h + 1) * HEAD_DIM)
        q = q_ref[0, :, sl]
        s = jnp.concatenate([_mm_nt(q, kp_ref[0, :, sl]), _mm_nt(q, kc_ref[0, :, sl])], axis=1)
        s = jnp.where(valid, s * scale + bias_ref[h], NEG_BIG)
        m = jnp.max(s, axis=1, keepdims=True)
        p = jnp.exp(s - m)
        l = jnp.sum(p, axis=1, keepdims=True)
        pb = _bf(p)
        o = (jnp.dot(pb[:, :BLOCK], vp_ref[0, :, sl], preferred_element_type=F32)
             + jnp.dot(pb[:, BLOCK:], vc_ref[0, :, sl], preferred_element_type=F32))
        o_ref[0, :, sl] = o / l
        lse_ref[0, :, sl] = jnp.broadcast_to(m + jnp.log(l), (BLOCK, HEAD_DIM))


def _band_attention(arr, col0, bias, n_back):
    g, length, _ = arr.shape
    nc = length // BLOCK
    w = GROUP_WIDTH
    prev = lambda i: jnp.maximum(i - 1, 0)
    out = jax.ShapeDtypeStruct((g, length, w), F32)
    return pl.pallas_call(
        functools.partial(_band_kernel, n_back=n_back),
        grid=(g, nc),
        in_specs=[
            pl.BlockSpec((1, BLOCK, w), lambda b, i: (b, i, col0)),
            pl.BlockSpec((1, BLOCK, w), lambda b, i: (b, prev(i), col0 + 1)),
            pl.BlockSpec((1, BLOCK, w), lambda b, i: (b, i, col0 + 1)),
            pl.BlockSpec((1, BLOCK, w), lambda b, i: (b, prev(i), col0 + 2)),
            pl.BlockSpec((1, BLOCK, w), lambda b, i: (b, i, col0 + 2)),
            pl.BlockSpec((N_HEADS, BLOCK, 2 * BLOCK), lambda b, i: (0, 0, 0)),
        ],
        out_specs=[pl.BlockSpec((1, BLOCK, w), lambda b, i: (b, i, 0)),
                   pl.BlockSpec((1, BLOCK, w), lambda b, i: (b, i, 0))],
        out_shape=[out, out],
        compiler_params=_params(("parallel", "arbitrary")),
        name=f"band_attention_{n_back}",
    )(arr, arr, arr, arr, arr, bias)


def _merge_kernel(o1, o2, o3, l1, l2, l3, y_ref):
    a, b, c = l1[0], l2[0], l3[0]
    m = jnp.maximum(jnp.maximum(a, b), c)
    ea, eb, ec = jnp.exp(a - m), jnp.exp(b - m), jnp.exp(c - m)
    y = (ea * o1[0] + eb * o2[0] + ec * o3[0]) / (ea + eb + ec)
    y_ref[0] = y.astype(y_ref.dtype)


def _merge_patterns(outs, lses):
    bsz, s, w = outs[0].shape
    tm = 512
    spec = pl.BlockSpec((1, tm, w), lambda b, i: (b, i, 0))
    return pl.pallas_call(
        _merge_kernel,
        grid=(bsz, s // tm),
        in_specs=[spec] * 6,
        out_specs=spec,
        out_shape=jax.ShapeDtypeStruct((bsz, s, w), BF16),
        compiler_params=_params(("parallel", "parallel")),
        name="merge_dilations",
    )(*outs, *lses)


def _gather_stride(t, dil):
    b, s, w = t.shape
    length = s // dil
    return t.reshape(b, length, dil, w).transpose(0, 2, 1, 3).reshape(b * dil, length, w)


def _scatter_stride(t, dil, bsz):
    g, length, w = t.shape
    return t.reshape(bsz, dil, length, w).transpose(0, 2, 1, 3).reshape(bsz, length * dil, w)


def _dilated_attention(qkv, col0, biases):
    bsz = qkv.shape[0]
    w = GROUP_WIDTH
    outs, lses = [], []
    for (window, dil), bias in zip(DILATIONS, biases):
        n_back = window // dil
        if dil == 1:
            o, lse = _band_attention(qkv, col0, bias, n_back)
        else:
            slab = _gather_stride(qkv[:, :, col0 * w:(col0 + 3) * w], dil)
            o, lse = _band_attention(slab, 0, bias, n_back)
            o, lse = _scatter_stride(o, dil, bsz), _scatter_stride(lse, dil, bsz)
        outs.append(o)
        lses.append(lse)
    return _merge_patterns(outs, lses)


def _unit_lower_inverses(ns):
    size_all = 2 * CHUNK
    row = _iota2((size_all, size_all), 0)
    col = _iota2((size_all, size_all), 1)
    eye = jnp.where(row == col, 1.0, 0.0)
    base = 8
    same8 = (row >> 3) == (col >> 3)
    n8 = [jnp.where(same8, n, 0.0) for n in ns]
    n2 = [_mm(a, a) for a in n8]
    n4 = [_mm(a, a) for a in n2]
    ts = [eye + a for a in n8]
    ts = [t + _mm(t, b) for t, b in zip(ts, n2)]
    ts = [t + _mm(t, b) for t, b in zip(ts, n4)]
    size = 2 * base
    while size <= CHUNK:
        half = size // 2
        shift = size.bit_length() - 1
        off = (((row >> shift) == (col >> shift)) & ((row & (size - 1)) >= half)
               & ((col & (size - 1)) < half))
        us = [_mm(t, jnp.where(off, n, 0.0)) for t, n in zip(ts, ns)]
        ts = [t + _mm(u, t) for t, u in zip(ts, us)]
        size *= 2
    return ts


def _rwkv_kernel(*refs, has_vres):
    if has_vres:
        (pa_ref, mu_ref, w0_ref, wup_ref, a0_ref, aup_ref, gup_ref, kk_ref, ka_ref, rk_ref,
         lnw_ref, lnb_ref, seg_ref, vf_ref, vup_ref, vb_ref,
         y_ref, prev_scr, s_scr, r_s, k_s, v_s, lw_s, am_s, b_s, y_s) = refs
    else:
        (pa_ref, mu_ref, w0_ref, wup_ref, a0_ref, aup_ref, gup_ref, kk_ref, ka_ref, rk_ref,
         lnw_ref, lnb_ref, seg_ref,
         y_ref, vf_out_ref, prev_scr, s_scr, r_s, k_s, v_s, lw_s, am_s, b_s, y_s) = refs
    c = GROUP_WIDTH
    tt = pa_ref.shape[1]

    @pl.when(pl.program_id(1) == 0)
    def _():
        prev_scr[...] = jnp.zeros_like(prev_scr)
        s_scr[...] = jnp.zeros_like(s_scr)

    seg = seg_ref[...]
    seg_sum = lambda t: _mm_split_lhs(t, seg, 2)

    pa = pa_ref[0]
    rolled = pltpu.roll(pa, 1, 0)
    first = _iota2(pa.shape, 0) == 0
    prev = jnp.where(first, prev_scr[...], rolled)
    prev_scr[...] = pa[tt - 1:tt, :]
    sh = pa + (prev - pa) * mu_ref[...]
    r = sh[:, 0:c]
    k = sh[:, c:2 * c]
    v = sh[:, 2 * c:3 * c]
    w_lo = sh[:, A_OFF_W:A_OFF_W + LORA_PAD]
    a_lo = sh[:, A_OFF_A:A_OFF_A + LORA_PAD]
    g_lo = sh[:, A_OFF_G:A_OFF_G + G_PAD]
    if has_vres:
        pv = sh[:, A_OFF_V:A_OFF_V + LORA_PAD]
        v = v + (vf_ref[0] - v) * _sigmoid(vb_ref[...] + _mm(pv, vup_ref[...]))
    else:
        vf_out_ref[0] = v

    w_log = -_softplus(-(w0_ref[...] + _mm(jnp.tanh(w_lo), wup_ref[...]))) - 0.5
    lw = -jnp.exp(w_log)
    a = _sigmoid(a0_ref[...] + _mm(a_lo, aup_ref[...]))
    g = _mm(_sigmoid(g_lo), gup_ref[...])
    kkr = k * kk_ref[...]
    kk = kkr * lax.rsqrt(jnp.maximum(seg_sum(kkr * kkr), 1e-24))
    k = k * (1.0 + (a - 1.0) * ka_ref[...])
    r_s[...] = r
    k_s[...] = k
    v_s[...] = v
    lw_s[...] = lw
    am_s[...] = -kk
    b_s[...] = kk * a

    pair_rows = 2 * CHUNK
    n_pairs = c // LANES
    first_head = _iota2((CHUNK, LANES), 1) < HEAD_DIM
    row = _iota2((pair_rows, pair_rows), 0)
    col = _iota2((pair_rows, pair_rows), 1)
    strict = (row & (CHUNK - 1)) > (col & (CHUNK - 1))
    lower = (row & (CHUNK - 1)) >= (col & (CHUNK - 1))
    eye = row == col
    crow = _iota2((CHUNK, CHUNK), 0)
    ccol = _iota2((CHUNK, CHUNK), 1)
    lower_ones = jnp.where(crow >= ccol, 1.0, 0.0).astype(BF16)

    def stack(t):
        return jnp.concatenate([jnp.where(first_head, t, 0.0), jnp.where(first_head, 0.0, t)], axis=0)

    def chunk_body(ci, carry):
        rows = pl.ds(pl.multiple_of(ci * CHUNK, CHUNK), CHUNK)
        lw_c = lw_s[rows, :]
        cum = _mm_split_rhs(lower_ones, lw_c, 3)
        eg = jnp.exp(cum)
        eig = jnp.exp(-cum)
        at = am_s[rows, :] * jnp.exp(cum - lw_c)
        rt = r_s[rows, :] * eg
        bt = b_s[rows, :] * eig
        kt = k_s[rows, :] * eig
        g_end = eg[CHUNK - 1:CHUNK, :]
        bg = bt * g_end
        kg = kt * g_end
        vv = v_s[rows, :]
        pairs = [slice(p * LANES, (p + 1) * LANES) for p in range(n_pairs)]
        at_p = [stack(at[:, sl]) for sl in pairs]
        rt_p = [stack(rt[:, sl]) for sl in pairs]
        v_p = [stack(vv[:, sl]) for sl in pairs]
        bg_p = [stack(bg[:, sl]) for sl in pairs]
        kg_p = [stack(kg[:, sl]) for sl in pairs]
        grams = [_mm_nt(jnp.concatenate([a, r_], axis=0),
                        jnp.concatenate([stack(bt[:, sl]), stack(kt[:, sl])], axis=0))
                 for a, r_, sl in zip(at_p, rt_p, pairs)]
        a_ab = [jnp.where(strict, gm[:pair_rows, :pair_rows], 0.0) for gm in grams]
        a_ak = [jnp.where(strict, gm[:pair_rows, pair_rows:], 0.0) for gm in grams]
        a_rb = [jnp.where(lower, gm[pair_rows:, :pair_rows], 0.0) for gm in grams]
        a_rk = [jnp.where(lower, gm[pair_rows:, pair_rows:], 0.0) for gm in grams]
        akv = [_mm(a, v_) for a, v_ in zip(a_ak, v_p)]
        rkv = [_mm(a, v_) for a, v_ in zip(a_rk, v_p)]
        ts = _unit_lower_inverses(a_ab)
        ps = [_mm(t, a) for t, a in zip(ts, at_p)]
        qs = [_mm(t, x) for t, x in zip(ts, akv)]
        r2 = [r_ + _mm(a, p) for r_, a, p in zip(rt_p, a_rb, ps)]
        y0 = [_mm(a, q) + x for a, q, x in zip(a_rb, qs, rkv)]
        ms = [jnp.where(eye, jnp.broadcast_to(g_end[:, sl], (pair_rows, LANES)), 0.0) + _mm_tn(p, b)
              for sl, p, b in zip(pairs, ps, bg_p)]
        zs = [_mm_tn(q, b) + _mm_tn(v_, k_) for q, b, v_, k_ in zip(qs, bg_p, v_p, kg_p)]
        for p, sl in enumerate(pairs):
            s0 = s_scr[p]
            y_st = _mm_nt(r2[p], s0) + y0[p]
            y_s[rows, sl] = y_st[:CHUNK] + y_st[CHUNK:]
            s_scr[p] = _mm(s0, ms[p]) + zs[p]
        return carry

    lax.fori_loop(0, tt // CHUNK, chunk_body, 0)

    y = y_s[...]
    mu = seg_sum(y) * (1.0 / HEAD_DIM)
    d = y - mu
    var = seg_sum(d * d) * (1.0 / HEAD_DIM)
    yn = d * lax.rsqrt(var + RWKV_LN_EPS) * lnw_ref[...] + lnb_ref[...]
    bonus = seg_sum(r_s[...] * k_s[...] * rk_ref[...]) * v_s[...]
    y_ref[0] = ((yn + bonus) * g).astype(y_ref.dtype)


def _rwkv7(pa, prm, v_first):
    bsz, s, na = pa.shape
    c = GROUP_WIDTH
    tt = RWKV_TILE
    has_vres = v_first is not None
    full = lambda shape: pl.BlockSpec(shape, lambda b, t: (0,) * len(shape))
    tile = lambda w: pl.BlockSpec((1, tt, w), lambda b, t: (b, t, 0))
    in_specs = [tile(na), full((1, na)), full((1, c)), full((LORA_PAD, c)), full((1, c)),
                full((LORA_PAD, c)), full((G_PAD, c)), full((1, c)), full((1, c)), full((1, c)),
                full((1, c)), full((1, c)), full((c, c))]
    args = [pa, prm["mu"], prm["w0"], prm["w_up"], prm["a0"], prm["a_up"], prm["g_up"],
            prm["k_k"], prm["k_a"], prm["r_k"], prm["ln_w"], prm["ln_b"], prm["seg"]]
    y_shape = jax.ShapeDtypeStruct((bsz, s, c), BF16)
    if has_vres:
        in_specs += [tile(c), full((LORA_PAD, c)), full((1, c))]
        args += [v_first, prm["vres_up"], prm["vres_bias"]]
        out_specs, out_shape = tile(c), y_shape
    else:
        out_specs = [tile(c), tile(c)]
        out_shape = [y_shape, jax.ShapeDtypeStruct((bsz, s, c), F32)]
    tile_scr = pltpu.VMEM((tt, c), F32)
    res = pl.pallas_call(
        functools.partial(_rwkv_kernel, has_vres=has_vres),
        grid=(bsz, s // tt),
        in_specs=in_specs,
        out_specs=out_specs,
        out_shape=out_shape,
        scratch_shapes=[pltpu.VMEM((1, na), F32), pltpu.VMEM((c // LANES, LANES, LANES), F32)]
        + [tile_scr] * 7,
        compiler_params=_params(("parallel", "arbitrary")),
        name="rwkv7_vres" if has_vres else "rwkv7",
    )(*args)
    if has_vres:
        return res, v_first
    return res[0], res[1]


def _rel_bucket(dist):
    max_exact = NUM_BUCKETS // 2
    d_f = jnp.maximum(dist, 1).astype(F32)
    large = max_exact + (jnp.log(d_f / max_exact) / math.log(MAX_DISTANCE / max_exact)
                         * (NUM_BUCKETS - max_exact)).astype(jnp.int32)
    large = jnp.minimum(large, NUM_BUCKETS - 1)
    return jnp.where(dist < max_exact, dist, large)


def _toeplitz(u, n_rows, n_cols):
    length = n_rows + n_cols - 1
    g = jnp.roll(jnp.flip(u, axis=-1), -(n_rows - 1), axis=-1)
    flat = jnp.tile(g, (1,) * (u.ndim - 1) + (n_rows,))[..., :n_rows * (length - 1)]
    return flat.reshape(u.shape[:-1] + (n_rows, length - 1))[..., :n_cols]


def _diff_bias_tiles(rel_tab, ns):
    dist = jnp.arange(ns)[:, None] * SPAN + jnp.arange(2 * SPAN - 1)[None, :] - (SPAN - 1)
    u = rel_tab[_rel_bucket(jnp.maximum(dist, 0))]
    return _toeplitz(jnp.transpose(u, (2, 0, 1)).astype(F32), SPAN, SPAN)


def _band_bias(rel_tab, dil):
    steps = jnp.arange(3 * BLOCK - 1) - (BLOCK - 1)
    u = rel_tab[_rel_bucket(jnp.maximum(steps, 0) * dil)]
    return _toeplitz(jnp.transpose(u, (1, 0)).astype(F32), BLOCK, 2 * BLOCK)


def _pad_axis(t, size, axis):
    pad = [(0, 0)] * t.ndim
    pad[axis] = (0, size - t.shape[axis])
    return jnp.pad(t, pad)


def _layout_mixer_a(t, vres=None):
    c = GROUP_WIDTH
    o_w, o_a, o_g = 3 * c, 3 * c + W_LORA, 3 * c + W_LORA + A_LORA
    parts = [t[..., :o_w], _pad_axis(t[..., o_w:o_a], LORA_PAD, -1),
             _pad_axis(t[..., o_a:o_g], LORA_PAD, -1), _pad_axis(t[..., o_g:N_A_COLS], G_PAD, -1)]
    if vres is not None:
        parts.append(_pad_axis(vres, 2 * LORA_PAD, -1))
    return jnp.concatenate(parts, axis=-1)


def kernel(x, c, w_ada, b_ada, norm_gain, w_in, w_out, rel_bias, rwkv_mu, rwkv_w0, rwkv_w_up,
           rwkv_a0, rwkv_a_up, rwkv_g_up, rwkv_k_k, rwkv_k_a, rwkv_r_k, rwkv_ln_w, rwkv_ln_b,
           vres_down, vres_mu, vres_up, vres_bias, diff_lambda, diff_subln, ffn_w13, ffn_w2,
           final_gain):
    out_dtype = x.dtype
    bsz, s, d = x.shape
    cw = GROUP_WIDTH
    x = x.astype(F32)
    row = lambda t: t.reshape(1, -1).astype(F32)

    mod = _modulation(c.astype(F32), w_ada, b_ada)
    rel_tab_c = rel_bias[:, :N_HEADS].astype(F32)
    rel_tab_d = rel_bias[:, N_HEADS:].astype(F32)
    band_biases = [_band_bias(rel_tab_c, dil) for _, dil in DILATIONS]
    diff_bias = _diff_bias_tiles(rel_tab_d, s // SPAN)
    seg = (jnp.arange(cw)[:, None] // HEAD_DIM == jnp.arange(cw)[None, :] // HEAD_DIM).astype(BF16)

    v_first = None
    for l in range(DEPTH):
        sh1, sc1, g1, sh2, sc2, g2 = [mod[l, :, None, i * d:(i + 1) * d] for i in range(6)]
        gain1, gain2 = row(norm_gain[l, 0]), row(norm_gain[l, 1])
        vres_w = vres_down[l - 1] if l > 0 else None
        w_a = _bf(_layout_mixer_a(w_in[l][:, :N_A_COLS], vres_w))
        w_bcd = _bf(w_in[l][:, N_A_COLS:N_IN])
        mu = _layout_mixer_a(rwkv_mu[l], vres_mu[l - 1] if l > 0 else None)

        pa = _norm_proj(x, gain1, sc1, sh1, w_a, F32, "proj_a")
        qkv = _norm_proj(x, gain1, sc1, sh1, w_bcd, BF16, "proj_bcd")

        prm = {
            "mu": row(mu), "w0": row(rwkv_w0[l]), "a0": row(rwkv_a0[l]),
            "w_up": _bf(_pad_axis(rwkv_w_up[l], LORA_PAD, 0)),
            "a_up": _bf(_pad_axis(rwkv_a_up[l], LORA_PAD, 0)),
            "g_up": _bf(_pad_axis(rwkv_g_up[l], G_PAD, 0)),
            "k_k": row(rwkv_k_k[l]), "k_a": row(rwkv_k_a[l]), "r_k": row(rwkv_r_k[l]),
            "ln_w": row(rwkv_ln_w[l]), "ln_b": row(rwkv_ln_b[l]), "seg": seg,
        }
        if l > 0:
            prm["vres_up"] = _bf(_pad_axis(vres_up[l - 1], LORA_PAD, 0))
            prm["vres_bias"] = row(vres_bias[l - 1])
        y_a, v_first = _rwkv7(pa, prm, v_first)

        y_b = _stick_breaking(qkv, 0)
        y_c = _dilated_attention(qkv, 3, band_biases)
        lam_init = 0.8 - 0.6 * math.exp(-0.3 * l)
        y_d = _diff_attention(qkv, 6 * cw // LANES, diff_bias, diff_lambda[l].astype(F32),
                              row(diff_subln[l]), lam_init)

        x = _out_proj(y_a, y_b, y_c, y_d, _bf(w_out[l]), x, g1)
        x = _ffn(x, gain2, sc2, sh2, g2, _bf(ffn_w13[l]), _bf(ffn_w2[l]), row(final_gain),
                 final=(l == DEPTH - 1))
    return x.astype(out_dtype)
```

```python
import functools
import math

import jax
import jax.numpy as jnp
from jax import lax
from jax.experimental import pallas as pl
from jax.experimental.pallas import tpu as pltpu

F32 = jnp.float32
BF16 = jnp.bfloat16

D_MODEL = 2048
DEPTH = 2
HEAD_DIM = 64
GROUP_WIDTH = D_MODEL // 4
N_HEADS = GROUP_WIDTH // HEAD_DIM
DIFF_HALF = HEAD_DIM
DIFF_VDIM = 2 * DIFF_HALF
H_D = GROUP_WIDTH // DIFF_VDIM
W_LORA = max(32, int(round(D_MODEL ** 0.5 * 1.8 / 32)) * 32)
A_LORA = max(32, int(round(D_MODEL ** 0.5 * 1.8 / 32)) * 32)
V_LORA = max(32, int(round(D_MODEL ** 0.5 * 1.3 / 32)) * 32)
G_LORA = max(32, int(round(D_MODEL ** 0.8 / 32)) * 32)
N_A_COLS = 3 * GROUP_WIDTH + W_LORA + A_LORA + G_LORA
N_IN = N_A_COLS + 9 * GROUP_WIDTH
D_FF = ((8 * D_MODEL + 3 * 256 - 1) // (3 * 256)) * 256
DILATIONS = ((128, 1), (512, 4), (2048, 16))
BLOCK = 128
NUM_BUCKETS = 32
MAX_DISTANCE = 2048
NORM_EPS = 1e-6
RWKV_LN_EPS = 64e-5
SUBLN_EPS = 1e-5
LOG2E = math.log2(math.e)

LANES = 128
SUBLANES = 8
VMEM_LIMIT_BYTES = 56 * 1024 * 1024

LORA_PAD = 128
G_PAD = 512
A_OFF_W = 3 * GROUP_WIDTH
A_OFF_A = A_OFF_W + LORA_PAD
A_OFF_G = A_OFF_A + LORA_PAD
A_OFF_V = A_OFF_G + G_PAD
PROJ_TN = 768
CHUNK = 64
RWKV_TILE = 256
SPAN_BLOCKS = 4
SPAN = SPAN_BLOCKS * BLOCK
DIL_TILES_PER_STEP = 4
NEG_BIG = -1e30


def _bf(x):
    return x.astype(BF16)


def _mm(a, b):
    return jnp.dot(_bf(a), _bf(b), preferred_element_type=F32)


def _mm_nt(a, b):
    return lax.dot_general(_bf(a), _bf(b), (((1,), (1,)), ((), ())), preferred_element_type=F32)


def _mm_tn(a, b):
    return lax.dot_general(_bf(a), _bf(b), (((0,), (0,)), ((), ())), preferred_element_type=F32)


def _split_bf16(x, n):
    parts = []
    rem = x
    for i in range(n):
        p = _bf(rem)
        parts.append(p)
        if i + 1 < n:
            rem = rem - p.astype(F32)
    return parts


def _mm_split_lhs(a, b_bf16, n):
    out = None
    for p in _split_bf16(a, n):
        t = jnp.dot(p, b_bf16, preferred_element_type=F32)
        out = t if out is None else out + t
    return out


def _mm_split_rhs(a_bf16, b, n):
    out = None
    for p in _split_bf16(b, n):
        t = jnp.dot(a_bf16, p, preferred_element_type=F32)
        out = t if out is None else out + t
    return out


def _softplus(x):
    return jnp.maximum(x, 0.0) + jnp.log1p(jnp.exp(-jnp.abs(x)))


def _softplus2(x):
    return jnp.maximum(x, 0.0) + jnp.log2(1.0 + jnp.exp2(-jnp.abs(x)))


def _sigmoid(x):
    return jax.nn.sigmoid(x)


def _iota2(shape, dim):
    return lax.broadcasted_iota(jnp.int32, shape, dim)


def _params(sem, vmem=VMEM_LIMIT_BYTES):
    return pltpu.CompilerParams(dimension_semantics=sem, vmem_limit_bytes=vmem)


def _mod_kernel(c_ref, w_ref, b_ref, o_ref):
    c = c_ref[...]
    cond = c * _sigmoid(c)
    o_ref[0] = _mm(cond, w_ref[0]) + b_ref[0]


def _modulation(c, w_ada, b_ada):
    depth, d, n = w_ada.shape
    bsz = c.shape[0]
    tn = 1024
    return pl.pallas_call(
        _mod_kernel,
        grid=(depth, n // tn),
        in_specs=[
            pl.BlockSpec((bsz, d), lambda l, j: (0, 0)),
            pl.BlockSpec((1, d, tn), lambda l, j: (l, 0, j)),
            pl.BlockSpec((1, 1, tn), lambda l, j: (l, 0, j)),
        ],
        out_specs=pl.BlockSpec((1, bsz, tn), lambda l, j: (l, 0, j)),
        out_shape=jax.ShapeDtypeStruct((depth, bsz, n), F32),
        compiler_params=_params(("parallel", "parallel")),
        name="adaln_mod",
    )(c, w_ada, b_ada.reshape(depth, 1, n))


def _norm_mod(x, gain, sc, sh):
    ms = jnp.mean(x * x, axis=-1, keepdims=True)
    return x * lax.rsqrt(ms + NORM_EPS) * gain * (1.0 + sc) + sh


def _proj_kernel(x_ref, gain_ref, sc_ref, sh_ref, w_ref, *rest, splits):
    o_refs, h_scr = rest[:-1], rest[-1]
    j = pl.program_id(2)

    @pl.when(j == 0)
    def _():
        h_scr[...] = _bf(_norm_mod(x_ref[0], gain_ref[...], sc_ref[0], sh_ref[0]))

    lo = 0
    for o_ref, n_tiles in zip(o_refs, splits):
        @pl.when((j >= lo) & (j < lo + n_tiles))
        def _(o_ref=o_ref):
            o_ref[0] = jnp.dot(h_scr[...], w_ref[...], preferred_element_type=F32).astype(o_ref.dtype)
        lo += n_tiles


def _norm_proj(x, gain, sc, sh, w, widths, dtypes):
    bsz, s, d = x.shape
    tm, tn = 1024, PROJ_TN
    splits = tuple(n // tn for n in widths)
    starts = [sum(splits[:k]) for k in range(len(splits))]

    def out_spec(k):
        return pl.BlockSpec((1, tm, tn),
                            lambda b, i, j: (b, i, jnp.clip(j - starts[k], 0, splits[k] - 1)))

    return pl.pallas_call(
        functools.partial(_proj_kernel, splits=splits),
        grid=(bsz, s // tm, sum(splits)),
        in_specs=[
            pl.BlockSpec((1, tm, d), lambda b, i, j: (b, i, 0)),
            pl.BlockSpec((1, d), lambda b, i, j: (0, 0)),
            pl.BlockSpec((1, 1, d), lambda b, i, j: (b, 0, 0)),
            pl.BlockSpec((1, 1, d), lambda b, i, j: (b, 0, 0)),
            pl.BlockSpec((d, tn), lambda b, i, j: (0, j)),
        ],
        out_specs=[out_spec(k) for k in range(len(widths))],
        out_shape=[jax.ShapeDtypeStruct((bsz, s, n), dt) for n, dt in zip(widths, dtypes)],
        scratch_shapes=[pltpu.VMEM((tm, d), BF16)],
        compiler_params=_params(("parallel", "parallel", "arbitrary")),
        name="in_proj",
    )(x, gain, sc, sh, w)


def _outproj_kernel(ya_ref, yb_ref, yc_ref, yd_ref, w_ref, x_ref, g_ref, o_ref):
    c = GROUP_WIDTH
    acc = jnp.dot(ya_ref[0], w_ref[0:c, :], preferred_element_type=F32)
    acc += jnp.dot(yb_ref[0], w_ref[c:2 * c, :], preferred_element_type=F32)
    acc += jnp.dot(yc_ref[0], w_ref[2 * c:3 * c, :], preferred_element_type=F32)
    acc += jnp.dot(yd_ref[0], w_ref[3 * c:4 * c, :], preferred_element_type=F32)
    o_ref[0] = x_ref[0] + g_ref[0] * acc


def _out_proj(ya, yb, yc, yd, w, x, g1):
    bsz, s, d = x.shape
    c = GROUP_WIDTH
    tm, tn = 1024, 1024
    yspec = pl.BlockSpec((1, tm, c), lambda b, i, j: (b, i, 0))
    return pl.pallas_call(
        _outproj_kernel,
        grid=(bsz, s // tm, d // tn),
        in_specs=[
            yspec, yspec, yspec, yspec,
            pl.BlockSpec((4 * c, tn), lambda b, i, j: (0, j)),
            pl.BlockSpec((1, tm, tn), lambda b, i, j: (b, i, j)),
            pl.BlockSpec((1, 1, tn), lambda b, i, j: (b, 0, j)),
        ],
        out_specs=pl.BlockSpec((1, tm, tn), lambda b, i, j: (b, i, j)),
        out_shape=jax.ShapeDtypeStruct((bsz, s, d), F32),
        compiler_params=_params(("parallel", "parallel", "arbitrary")),
        name="out_proj",
    )(ya, yb, yc, yd, w, x, g1)


def _ffn_kernel(x_ref, gain_ref, sc_ref, sh_ref, g_ref, w1_ref, w3_ref, w2_ref, fg_ref,
                o_ref, h_scr, acc_scr, *, final):
    f = pl.program_id(2)

    @pl.when(f == 0)
    def _():
        h_scr[...] = _bf(_norm_mod(x_ref[0], gain_ref[...], sc_ref[0], sh_ref[0]))
        acc_scr[...] = jnp.zeros_like(acc_scr)

    h = h_scr[...]
    gate = jnp.dot(h, w1_ref[...], preferred_element_type=F32)
    up = jnp.dot(h, w3_ref[...], preferred_element_type=F32)
    act = gate * _sigmoid(gate) * up
    acc_scr[...] += jnp.dot(_bf(act), w2_ref[...], preferred_element_type=F32)

    @pl.when(f == pl.num_programs(2) - 1)
    def _():
        xn = x_ref[0] + g_ref[0] * acc_scr[...]
        if final:
            ms = jnp.mean(xn * xn, axis=-1, keepdims=True)
            xn = xn * lax.rsqrt(ms + NORM_EPS) * fg_ref[...]
        o_ref[0] = xn


def _ffn(x, gain, sc, sh, g2, w13, w2, final_gain, final):
    bsz, s, d = x.shape
    dff = w2.shape[0]
    tm, tf = 512, 512
    nf = dff // tf
    vec = pl.BlockSpec((1, 1, d), lambda b, i, f: (b, 0, 0))
    return pl.pallas_call(
        functools.partial(_ffn_kernel, final=final),
        grid=(bsz, s // tm, nf),
        in_specs=[
            pl.BlockSpec((1, tm, d), lambda b, i, f: (b, i, 0)),
            pl.BlockSpec((1, d), lambda b, i, f: (0, 0)),
            vec, vec, vec,
            pl.BlockSpec((d, tf), lambda b, i, f: (0, f)),
            pl.BlockSpec((d, tf), lambda b, i, f: (0, f + nf)),
            pl.BlockSpec((tf, d), lambda b, i, f: (f, 0)),
            pl.BlockSpec((1, d), lambda b, i, f: (0, 0)),
        ],
        out_specs=pl.BlockSpec((1, tm, d), lambda b, i, f: (b, i, 0)),
        out_shape=jax.ShapeDtypeStruct((bsz, s, d), F32),
        scratch_shapes=[pltpu.VMEM((tm, d), BF16), pltpu.VMEM((tm, d), F32)],
        compiler_params=_params(("parallel", "parallel", "arbitrary")),
        name="ffn_final" if final else "ffn",
    )(x, gain, sc, sh, g2, w13, w13, w2, final_gain)


def _sb_kernel(q_ref, k_ref, v_ref, later_ref, o_ref):
    qs = pl.program_id(2)
    nh = LANES // HEAD_DIM
    later = later_ref[...]
    row = _iota2((SPAN, BLOCK), 0)
    col = _iota2((SPAN, BLOCK), 1)

    def span_tile(q, sl, span, run, masked):
        st = pl.multiple_of(span * SPAN, SPAN)
        z = lax.dot_general(q, k_ref[0, pl.ds(st, SPAN), sl], (((1,), (1,)), ((), ())),
                            preferred_element_type=F32)
        subs = []
        for j in range(SPAN_BLOCKS):
            zj = z[:, j * BLOCK:(j + 1) * BLOCK]
            spj = _softplus2(zj)
            l1m = -spj
            mask = None
            if masked:
                mask = (j * BLOCK + col) < row
                l1m = jnp.where(mask, l1m, 0.0)
            subs.append((zj - spj, l1m, mask))
        stacked = jnp.concatenate([_bf(l1m) for _, l1m, _ in subs], axis=0)
        within = jnp.dot(stacked, later, preferred_element_type=F32)
        atts = [None] * SPAN_BLOCKS
        for j in reversed(range(SPAN_BLOCKS)):
            log_beta, l1m, mask = subs[j]
            w = within[j * SPAN:(j + 1) * SPAN]
            att = jnp.exp2(log_beta + w + run)
            if masked:
                att = jnp.where(mask, att, 0.0)
            atts[j] = _bf(att)
            run = run + jnp.sum(l1m, axis=1, keepdims=True)
        contrib = jnp.dot(jnp.concatenate(atts, axis=1), v_ref[0, pl.ds(st, SPAN), sl],
                          preferred_element_type=F32)
        return contrib, run

    heads = []
    for h in range(nh):
        sl = slice(h * HEAD_DIM, (h + 1) * HEAD_DIM)
        heads.append((q_ref[0, :, sl], sl))

    def step(span, carry, masked):
        out = []
        for (qh, sl), (acc, run) in zip(heads, carry):
            contrib, run = span_tile(qh, sl, span, run, masked)
            out.append((acc + contrib, run))
        return tuple(out)

    init = tuple((jnp.zeros((SPAN, HEAD_DIM), F32), jnp.zeros((SPAN, 1), F32)) for _ in range(nh))
    carry = step(qs, init, True)
    carry = lax.fori_loop(0, qs, lambda i, c: step(qs - 1 - i, c, False), carry)
    for (_, sl), (acc, _) in zip(heads, carry):
        o_ref[0, :, sl] = acc.astype(o_ref.dtype)


def _stick_breaking(qkv, col0):
    bsz, s, _ = qkv.shape
    nhp = GROUP_WIDTH // LANES
    later = jnp.tril(jnp.ones((BLOCK, BLOCK), F32), -1).astype(BF16)
    return pl.pallas_call(
        _sb_kernel,
        grid=(bsz, nhp, s // SPAN),
        in_specs=[
            pl.BlockSpec((1, SPAN, LANES), lambda b, h, i: (b, i, col0 + h)),
            pl.BlockSpec((1, s, LANES), lambda b, h, i: (b, 0, col0 + nhp + h)),
            pl.BlockSpec((1, s, LANES), lambda b, h, i: (b, 0, col0 + 2 * nhp + h)),
            pl.BlockSpec((BLOCK, BLOCK), lambda b, h, i: (0, 0)),
        ],
        out_specs=pl.BlockSpec((1, SPAN, LANES), lambda b, h, i: (b, i, h)),
        out_shape=jax.ShapeDtypeStruct((bsz, s, GROUP_WIDTH), BF16),
        compiler_params=_params(("parallel", "parallel", "arbitrary")),
        name="stick_breaking",
    )(qkv, qkv, qkv, later)


def _diff_kernel(q_ref, k_ref, v_ref, vec_ref, lam_ref, subln_ref, o_ref, bias_scr, *, lam_init):
    qs = pl.program_id(2)
    causal = _iota2((SPAN, SPAN), 1) <= _iota2((SPAN, SPAN), 0)
    halves = [slice(c * DIFF_HALF, (c + 1) * DIFF_HALF) for c in range(2)]
    q = [q_ref[0, :, sl] for sl in halves]

    @pl.when((pl.program_id(1) == 0) & (qs == 0))
    def _():
        for dist in range(bias_scr.shape[0]):
            vec = jnp.broadcast_to(vec_ref[0, dist][0:1, :], (SPAN, 2 * SPAN))
            bias_scr[dist] = pltpu.roll(vec, 0, 1, stride=1, stride_axis=0)[:, :SPAN]

    def step(span, carry, masked):
        st = pl.multiple_of(span * SPAN, SPAN)
        bias = bias_scr[qs - span]
        v = v_ref[0, pl.ds(st, SPAN), :]
        out = []
        for c in range(2):
            m, l, acc = carry[c]
            s = lax.dot_general(q[c], k_ref[0, pl.ds(st, SPAN), halves[c]], (((1,), (1,)), ((), ())),
                                preferred_element_type=F32) + bias
            if masked:
                s = jnp.where(causal, s, NEG_BIG)
            m_new = jnp.maximum(m, jnp.max(s, axis=1, keepdims=True))
            alpha = jnp.exp2(m - m_new)
            p = jnp.exp2(s - m_new)
            l = alpha * l + jnp.sum(p, axis=1, keepdims=True)
            acc = alpha * acc + jnp.dot(_bf(p), v, preferred_element_type=F32)
            out.append((m_new, l, acc))
        return tuple(out)

    def init():
        return (jnp.full((SPAN, 1), NEG_BIG, F32), jnp.zeros((SPAN, 1), F32),
                jnp.zeros((SPAN, DIFF_VDIM), F32))

    carry = step(qs, (init(), init()), True)
    carry = lax.fori_loop(0, qs, lambda i, c: step(i, c, False), carry)

    lp = lam_ref[...]
    lam = (jnp.exp(jnp.sum(lp[0:1] * lp[1:2], axis=1, keepdims=True))
           - jnp.exp(jnp.sum(lp[2:3] * lp[3:4], axis=1, keepdims=True)) + lam_init)
    (_, l0, a0), (_, l1, a1) = carry
    o = a0 / l0 - lam * (a1 / l1)
    ms = jnp.mean(o * o, axis=-1, keepdims=True)
    o = o * lax.rsqrt(ms + SUBLN_EPS) * subln_ref[...] * (1.0 - lam_init)
    o_ref[0] = o.astype(o_ref.dtype)


def _diff_attention(qkv, col0, bias_vecs, lam_params, subln, lam_init):
    bsz, s, _ = qkv.shape
    ns = s // SPAN
    return pl.pallas_call(
        functools.partial(_diff_kernel, lam_init=lam_init),
        grid=(H_D, bsz, ns),
        in_specs=[
            pl.BlockSpec((1, SPAN, LANES), lambda h, b, i: (b, i, col0 + h)),
            pl.BlockSpec((1, s, LANES), lambda h, b, i: (b, 0, col0 + H_D + h)),
            pl.BlockSpec((1, s, LANES), lambda h, b, i: (b, 0, col0 + 2 * H_D + h)),
            pl.BlockSpec((1, ns, SUBLANES, 2 * SPAN), lambda h, b, i: (h, 0, 0, 0)),
            pl.BlockSpec((4, DIFF_HALF), lambda h, b, i: (0, 0)),
            pl.BlockSpec((1, DIFF_VDIM), lambda h, b, i: (0, 0)),
        ],
        out_specs=pl.BlockSpec((1, SPAN, LANES), lambda h, b, i: (b, i, h)),
        out_shape=jax.ShapeDtypeStruct((bsz, s, GROUP_WIDTH), BF16),
        scratch_shapes=[pltpu.VMEM((ns, SPAN, SPAN), F32)],
        compiler_params=_params(("arbitrary", "arbitrary", "arbitrary")),
        name="diff_attention",
    )(qkv, qkv, qkv, bias_vecs, lam_params, subln)


def _dil_kernel(q_ref, k_ref, v_ref, b1_ref, b4_ref, b16_ref, y_ref, acc_scr, m_scr, l_scr):
    s_len = q_ref.shape[1]
    first_head = _iota2((BLOCK, LANES), 1) < HEAD_DIM
    bias_refs = (b1_ref, b4_ref, b16_ref)

    def stack_heads(t):
        return jnp.concatenate([jnp.where(first_head, t, 0.0), jnp.where(first_head, 0.0, t)], axis=0)

    def unstack(t):
        return jnp.where(first_head, t[:BLOCK], t[BLOCK:])

    for pi, (window, dil) in enumerate(DILATIONS):
        n_back = window // dil
        nc = s_len // dil // BLOCK
        has_prev = nc > 1
        width = 2 * BLOCK if has_prev else BLOCK
        qi = _iota2((2 * BLOCK, width), 0) & (BLOCK - 1)
        ki = _iota2((2 * BLOCK, width), 1)
        steps = qi + (BLOCK if has_prev else 0) - ki
        band = (steps >= 0) & (steps <= n_back)
        bias = bias_refs[pi][0]
        if not has_prev:
            bias = bias[:, BLOCK:]

        def body(it, carry, dil=dil, nc=nc, has_prev=has_prev, band=band, bias=bias, pi=pi, ki=ki):
            loads = []
            for u in range(DIL_TILES_PER_STEP):
                idx = it * DIL_TILES_PER_STEP + u
                r, c = idx // nc, idx % nc
                rows = pl.ds(r + dil * BLOCK * c, BLOCK, stride=dil)
                q = stack_heads(q_ref[0, rows, :])
                k = k_ref[0, rows, :]
                v = v_ref[0, rows, :]
                if has_prev:
                    prows = pl.ds(r + dil * BLOCK * jnp.maximum(c - 1, 0), BLOCK, stride=dil)
                    k = jnp.concatenate([k_ref[0, prows, :], k], axis=0)
                    v = jnp.concatenate([v_ref[0, prows, :], v], axis=0)
                loads.append((rows, c, _bf(q), _bf(k), _bf(v)))
            ss = [lax.dot_general(q, k, (((1,), (1,)), ((), ())), preferred_element_type=F32) + bias
                  for _, _, q, k, _ in loads]
            if has_prev:
                ss = [jnp.where(band & (ki >= jnp.where(c > 0, 0, BLOCK)), s, NEG_BIG)
                      for s, (_, c, _, _, _) in zip(ss, loads)]
            else:
                ss = [jnp.where(band, s, NEG_BIG) for s in ss]
            ms = [jnp.max(s, axis=1, keepdims=True) for s in ss]
            ps = [jnp.exp2(s - m) for s, m in zip(ss, ms)]
            ls = [jnp.sum(p, axis=1, keepdims=True) for p in ps]
            os_ = [jnp.dot(_bf(p), v, preferred_element_type=F32) for p, (_, _, _, _, v) in zip(ps, loads)]
            for (rows, _, _, _, _), m, l, o in zip(loads, ms, ls, os_):
                acc_scr[pi, rows, :] = unstack(o)
                m_scr[pi, rows, :] = unstack(jnp.broadcast_to(m, (2 * BLOCK, LANES)))
                l_scr[pi, rows, :] = unstack(jnp.broadcast_to(l, (2 * BLOCK, LANES)))
            return carry

        lax.fori_loop(0, s_len // BLOCK // DIL_TILES_PER_STEP, body, 0)

    tm = 2 * BLOCK

    def merge(i, carry):
        rows = pl.ds(pl.multiple_of(i * tm, tm), tm)
        m1, m2, m3 = m_scr[0, rows, :], m_scr[1, rows, :], m_scr[2, rows, :]
        m = jnp.maximum(jnp.maximum(m1, m2), m3)
        e1, e2, e3 = jnp.exp2(m1 - m), jnp.exp2(m2 - m), jnp.exp2(m3 - m)
        num = e1 * acc_scr[0, rows, :] + e2 * acc_scr[1, rows, :] + e3 * acc_scr[2, rows, :]
        den = e1 * l_scr[0, rows, :] + e2 * l_scr[1, rows, :] + e3 * l_scr[2, rows, :]
        y_ref[0, rows, :] = (num / den).astype(y_ref.dtype)
        return carry

    lax.fori_loop(0, s_len // tm, merge, 0)


def _dilated_attention(qkv_c, biases):
    bsz, s, _ = qkv_c.shape
    nhp = GROUP_WIDTH // LANES
    bspec = pl.BlockSpec((1, 2 * BLOCK, 2 * BLOCK), lambda b, h: (h, 0, 0))
    scr = pltpu.VMEM((len(DILATIONS), s, LANES), F32)
    return pl.pallas_call(
        _dil_kernel,
        grid=(bsz, nhp),
        in_specs=[
            pl.BlockSpec((1, s, LANES), lambda b, h: (b, 0, h)),
            pl.BlockSpec((1, s, LANES), lambda b, h: (b, 0, nhp + h)),
            pl.BlockSpec((1, s, LANES), lambda b, h: (b, 0, 2 * nhp + h)),
            bspec, bspec, bspec,
        ],
        out_specs=pl.BlockSpec((1, s, LANES), lambda b, h: (b, 0, h)),
        out_shape=jax.ShapeDtypeStruct((bsz, s, GROUP_WIDTH), BF16),
        scratch_shapes=[scr, scr, scr],
        compiler_params=_params(("parallel", "parallel")),
        name="dilated_attention",
    )(qkv_c, qkv_c, qkv_c, *biases)


def _unit_lower_inverses(ns):
    size_all = 2 * CHUNK
    row = _iota2((size_all, size_all), 0)
    col = _iota2((size_all, size_all), 1)
    eye = jnp.where(row == col, 1.0, 0.0)
    base = 8
    same8 = (row >> 3) == (col >> 3)
    n8 = [jnp.where(same8, n, 0.0) for n in ns]
    n2 = [_mm(a, a) for a in n8]
    n4 = [_mm(a, a) for a in n2]
    ts = [eye + a for a in n8]
    ts = [t + _mm(t, b) for t, b in zip(ts, n2)]
    ts = [t + _mm(t, b) for t, b in zip(ts, n4)]
    size = 2 * base
    while size <= CHUNK:
        half = size // 2
        shift = size.bit_length() - 1
        off = (((row >> shift) == (col >> shift)) & ((row & (size - 1)) >= half)
               & ((col & (size - 1)) < half))
        us = [_mm(t, jnp.where(off, n, 0.0)) for t, n in zip(ts, ns)]
        ts = [t + _mm(u, t) for t, u in zip(ts, us)]
        size *= 2
    return ts


def _rwkv_kernel(*refs, has_vres):
    if has_vres:
        (pa_ref, mu_ref, w0_ref, wup_ref, a0_ref, aup_ref, gup_ref, kk_ref, ka_ref, rk_ref,
         lnw_ref, lnb_ref, seg_ref, cum_ref, vf_ref, vup_ref, vb_ref,
         y_ref, prev_scr, s_scr) = refs
    else:
        (pa_ref, mu_ref, w0_ref, wup_ref, a0_ref, aup_ref, gup_ref, kk_ref, ka_ref, rk_ref,
         lnw_ref, lnb_ref, seg_ref, cum_ref,
         y_ref, vf_out_ref, prev_scr, s_scr) = refs
    c = GROUP_WIDTH
    tt = pa_ref.shape[1]

    @pl.when(pl.program_id(1) == 0)
    def _():
        prev_scr[...] = jnp.zeros_like(prev_scr)
        s_scr[...] = jnp.zeros_like(s_scr)

    seg = seg_ref[...]
    seg_sum = lambda t: _mm_split_lhs(t, seg, 2)

    pa = pa_ref[0]
    rolled = pltpu.roll(pa, 1, 0)
    first = _iota2(pa.shape, 0) == 0
    prev = jnp.where(first, prev_scr[...], rolled)
    prev_scr[...] = pa[tt - 1:tt, :]
    sh = pa + (prev - pa) * mu_ref[...]
    r = sh[:, 0:c]
    k = sh[:, c:2 * c]
    v = sh[:, 2 * c:3 * c]
    w_lo = sh[:, A_OFF_W:A_OFF_W + LORA_PAD]
    a_lo = sh[:, A_OFF_A:A_OFF_A + LORA_PAD]
    g_lo = sh[:, A_OFF_G:A_OFF_G + G_PAD]
    if has_vres:
        pv = sh[:, A_OFF_V:A_OFF_V + LORA_PAD]
        v = v + (vf_ref[0] - v) * _sigmoid(vb_ref[...] + _mm(pv, vup_ref[...]))
    else:
        vf_out_ref[0] = v

    w_log = -_softplus(-(w0_ref[...] + _mm(jnp.tanh(w_lo), wup_ref[...]))) - 0.5
    lw = -jnp.exp(w_log)
    a = _sigmoid(a0_ref[...] + _mm(a_lo, aup_ref[...]))
    g = _mm(_sigmoid(g_lo), gup_ref[...])
    kkr = k * kk_ref[...]
    kk = kkr * lax.rsqrt(jnp.maximum(seg_sum(kkr * kkr), 1e-24))
    k = k * (1.0 + (a - 1.0) * ka_ref[...])

    cum = _mm_split_rhs(cum_ref[...], lw, 3)
    eg = jnp.exp(cum)
    eig = jnp.exp(-cum)
    at = -kk * jnp.exp(cum - lw)
    rt = r * eg
    bt = kk * a * eig
    kt = k * eig

    pair_rows = 2 * CHUNK
    first_head = _iota2((CHUNK, LANES), 1) < HEAD_DIM
    row = _iota2((pair_rows, pair_rows), 0)
    col = _iota2((pair_rows, pair_rows), 1)
    strict = (row & (CHUNK - 1)) > (col & (CHUNK - 1))
    lower = (row & (CHUNK - 1)) >= (col & (CHUNK - 1))
    eye = row == col

    def stack(t):
        return jnp.concatenate([jnp.where(first_head, t, 0.0), jnp.where(first_head, 0.0, t)], axis=0)

    probs = [(ci, p) for ci in range(tt // CHUNK) for p in range(c // LANES)]
    cut = lambda t, ci, p: t[ci * CHUNK:(ci + 1) * CHUNK, p * LANES:(p + 1) * LANES]
    g_end = [cut(eg, ci, p)[CHUNK - 1:CHUNK, :] for ci, p in probs]
    at_p = [stack(cut(at, ci, p)) for ci, p in probs]
    rt_p = [stack(cut(rt, ci, p)) for ci, p in probs]
    bt_p = [stack(cut(bt, ci, p)) for ci, p in probs]
    kt_p = [stack(cut(kt, ci, p)) for ci, p in probs]
    v_p = [stack(cut(v, ci, p)) for ci, p in probs]
    bg_p = [b_ * ge for b_, ge in zip(bt_p, g_end)]
    kg_p = [k_ * ge for k_, ge in zip(kt_p, g_end)]
    grams = [_mm_nt(jnp.concatenate([a_, r_], axis=0), jnp.concatenate([b_, k_], axis=0))
             for a_, r_, b_, k_ in zip(at_p, rt_p, bt_p, kt_p)]
    a_ab = [jnp.where(strict, gm[:pair_rows, :pair_rows], 0.0) for gm in grams]
    a_ak = [jnp.where(strict, gm[:pair_rows, pair_rows:], 0.0) for gm in grams]
    a_rb = [jnp.where(lower, gm[pair_rows:, :pair_rows], 0.0) for gm in grams]
    a_rk = [jnp.where(lower, gm[pair_rows:, pair_rows:], 0.0) for gm in grams]
    akv = [_mm(a_, v_) for a_, v_ in zip(a_ak, v_p)]
    rkv = [_mm(a_, v_) for a_, v_ in zip(a_rk, v_p)]
    ts = _unit_lower_inverses(a_ab)
    ps = [_mm(t, a_) for t, a_ in zip(ts, at_p)]
    qs = [_mm(t, x_) for t, x_ in zip(ts, akv)]
    r2 = [r_ + _mm(a_, p_) for r_, a_, p_ in zip(rt_p, a_rb, ps)]
    y0 = [_mm(a_, q_) + x_ for a_, q_, x_ in zip(a_rb, qs, rkv)]
    ms = [jnp.where(eye, jnp.broadcast_to(ge, (pair_rows, LANES)), 0.0) + _mm_tn(p_, b_)
          for ge, p_, b_ in zip(g_end, ps, bg_p)]
    zs = [_mm_tn(q_, b_) + _mm_tn(v_, k_) for q_, b_, v_, k_ in zip(qs, bg_p, v_p, kg_p)]

    state = [s_scr[p] for p in range(c // LANES)]
    y_rows = []
    for ci in range(tt // CHUNK):
        y_pairs = []
        for p in range(c // LANES):
            i = ci * (c // LANES) + p
            y_st = _mm_nt(r2[i], state[p]) + y0[i]
            y_pairs.append(y_st[:CHUNK] + y_st[CHUNK:])
            state[p] = _mm(state[p], ms[i]) + zs[i]
        y_rows.append(jnp.concatenate(y_pairs, axis=1))
    for p in range(c // LANES):
        s_scr[p] = state[p]
    y = jnp.concatenate(y_rows, axis=0)

    mu = seg_sum(y) * (1.0 / HEAD_DIM)
    d = y - mu
    var = seg_sum(d * d) * (1.0 / HEAD_DIM)
    yn = d * lax.rsqrt(var + RWKV_LN_EPS) * lnw_ref[...] + lnb_ref[...]
    bonus = seg_sum(r * k * rk_ref[...]) * v
    y_ref[0] = ((yn + bonus) * g).astype(y_ref.dtype)


def _rwkv7(pa, prm, v_first):
    bsz, s, _ = pa.shape
    na = prm["mu"].shape[1]
    c = GROUP_WIDTH
    tt = RWKV_TILE
    has_vres = v_first is not None
    full = lambda shape: pl.BlockSpec(shape, lambda b, t: (0,) * len(shape))
    tile = lambda w: pl.BlockSpec((1, tt, w), lambda b, t: (b, t, 0))
    idx = jnp.arange(tt)
    cum_ones = ((idx[:, None] >= idx[None, :]) & (idx[:, None] // CHUNK == idx[None, :] // CHUNK)).astype(BF16)
    in_specs = [tile(na), full((1, na)), full((1, c)), full((LORA_PAD, c)), full((1, c)),
                full((LORA_PAD, c)), full((G_PAD, c)), full((1, c)), full((1, c)), full((1, c)),
                full((1, c)), full((1, c)), full((c, c)), full((tt, tt))]
    args = [pa, prm["mu"], prm["w0"], prm["w_up"], prm["a0"], prm["a_up"], prm["g_up"],
            prm["k_k"], prm["k_a"], prm["r_k"], prm["ln_w"], prm["ln_b"], prm["seg"], cum_ones]
    y_shape = jax.ShapeDtypeStruct((bsz, s, c), BF16)
    if has_vres:
        in_specs += [tile(c), full((LORA_PAD, c)), full((1, c))]
        args += [v_first, prm["vres_up"], prm["vres_bias"]]
        out_specs, out_shape = tile(c), y_shape
    else:
        out_specs = [tile(c), tile(c)]
        out_shape = [y_shape, jax.ShapeDtypeStruct((bsz, s, c), F32)]
    res = pl.pallas_call(
        functools.partial(_rwkv_kernel, has_vres=has_vres),
        grid=(bsz, s // tt),
        in_specs=in_specs,
        out_specs=out_specs,
        out_shape=out_shape,
        scratch_shapes=[pltpu.VMEM((1, na), F32), pltpu.VMEM((c // LANES, LANES, LANES), F32)],
        compiler_params=_params(("parallel", "arbitrary")),
        name="rwkv7_vres" if has_vres else "rwkv7",
    )(*args)
    if has_vres:
        return res, v_first
    return res[0], res[1]


def _rel_bucket(dist):
    max_exact = NUM_BUCKETS // 2
    d_f = jnp.maximum(dist, 1).astype(F32)
    large = max_exact + (jnp.log(d_f / max_exact) / math.log(MAX_DISTANCE / max_exact)
                         * (NUM_BUCKETS - max_exact)).astype(jnp.int32)
    large = jnp.minimum(large, NUM_BUCKETS - 1)
    return jnp.where(dist < max_exact, dist, large)


def _toeplitz(u, n_rows, n_cols):
    length = n_rows + n_cols - 1
    g = jnp.roll(jnp.flip(u, axis=-1), -(n_rows - 1), axis=-1)
    flat = jnp.tile(g, (1,) * (u.ndim - 1) + (n_rows,))[..., :n_rows * (length - 1)]
    return flat.reshape(u.shape[:-1] + (n_rows, length - 1))[..., :n_cols]


def _diff_bias_vectors(rel_tab, ns):
    n = jnp.arange(2 * SPAN)
    rel = jnp.where(n < SPAN, -n, 2 * SPAN - n)
    dist = jnp.arange(ns)[:, None] * SPAN + rel[None, :]
    u = rel_tab[_rel_bucket(jnp.maximum(dist, 0))] * LOG2E
    u = jnp.transpose(u, (2, 0, 1)).astype(F32)
    return jnp.broadcast_to(u[:, :, None, :], (H_D, ns, SUBLANES, 2 * SPAN))


def _band_bias(rel_tab, dil):
    steps = jnp.arange(3 * BLOCK - 1) - (BLOCK - 1)
    u = rel_tab[_rel_bucket(jnp.maximum(steps, 0) * dil)] * LOG2E
    t = _toeplitz(jnp.transpose(u, (1, 0)).astype(F32), BLOCK, 2 * BLOCK)
    return t.reshape(N_HEADS * HEAD_DIM // LANES, 2 * BLOCK, 2 * BLOCK)


def _pad_axis(t, size, axis):
    pad = [(0, 0)] * t.ndim
    pad[axis] = (0, size - t.shape[axis])
    return jnp.pad(t, pad)


def _layout_mixer_a(t, vres=None):
    c = GROUP_WIDTH
    o_w, o_a, o_g = 3 * c, 3 * c + W_LORA, 3 * c + W_LORA + A_LORA
    parts = [t[..., :o_w], _pad_axis(t[..., o_w:o_a], LORA_PAD, -1),
             _pad_axis(t[..., o_a:o_g], LORA_PAD, -1), _pad_axis(t[..., o_g:N_A_COLS], G_PAD, -1)]
    if vres is not None:
        parts.append(_pad_axis(vres, PROJ_TN, -1))
    return jnp.concatenate(parts, axis=-1)


def _scale_queries(w_qkv, scale):
    return jnp.concatenate([w_qkv[:, :GROUP_WIDTH] * scale, w_qkv[:, GROUP_WIDTH:]], axis=1)


def kernel(x, c, w_ada, b_ada, norm_gain, w_in, w_out, rel_bias, rwkv_mu, rwkv_w0, rwkv_w_up,
           rwkv_a0, rwkv_a_up, rwkv_g_up, rwkv_k_k, rwkv_k_a, rwkv_r_k, rwkv_ln_w, rwkv_ln_b,
           vres_down, vres_mu, vres_up, vres_bias, diff_lambda, diff_subln, ffn_w13, ffn_w2,
           final_gain):
    out_dtype = x.dtype
    bsz, s, d = x.shape
    cw = GROUP_WIDTH
    x = x.astype(F32)
    row = lambda t: t.reshape(1, -1).astype(F32)

    mod = _modulation(c.astype(F32), w_ada, b_ada)
    rel_tab_c = rel_bias[:, :N_HEADS].astype(F32)
    rel_tab_d = rel_bias[:, N_HEADS:].astype(F32)
    band_biases = [_band_bias(rel_tab_c, dil) for _, dil in DILATIONS]
    diff_bias = _diff_bias_vectors(rel_tab_d, s // SPAN)
    seg = (jnp.arange(cw)[:, None] // HEAD_DIM == jnp.arange(cw)[None, :] // HEAD_DIM).astype(BF16)
    logit_scale = LOG2E * HEAD_DIM ** -0.5

    v_first = None
    for l in range(DEPTH):
        sh1, sc1, g1, sh2, sc2, g2 = [mod[l, :, None, i * d:(i + 1) * d] for i in range(6)]
        gain1, gain2 = row(norm_gain[l, 0]), row(norm_gain[l, 1])
        vres_w = vres_down[l - 1] if l > 0 else None
        w_l = w_in[l].astype(F32)
        w_a = _layout_mixer_a(w_l[:, :N_A_COLS], vres_w)
        o_b, o_c, o_d = N_A_COLS, N_A_COLS + 3 * cw, N_A_COLS + 6 * cw
        w_all = _bf(jnp.concatenate([w_a, _scale_queries(w_l[:, o_b:o_c], logit_scale),
                                     _scale_queries(w_l[:, o_d:N_IN], logit_scale),
                                     _scale_queries(w_l[:, o_c:o_d], logit_scale)], axis=1))
        na_used = A_OFF_V + (LORA_PAD if l > 0 else 0)
        mu = _layout_mixer_a(rwkv_mu[l], vres_mu[l - 1] if l > 0 else None)[..., :na_used]

        pa, qkv_bd, qkv_c = _norm_proj(x, gain1, sc1, sh1, w_all, (w_a.shape[1], 6 * cw, 3 * cw),
                                       (F32, BF16, F32))

        prm = {
            "mu": row(mu), "w0": row(rwkv_w0[l]), "a0": row(rwkv_a0[l]),
            "w_up": _bf(_pad_axis(rwkv_w_up[l], LORA_PAD, 0)),
            "a_up": _bf(_pad_axis(rwkv_a_up[l], LORA_PAD, 0)),
            "g_up": _bf(_pad_axis(rwkv_g_up[l], G_PAD, 0)),
            "k_k": row(rwkv_k_k[l]), "k_a": row(rwkv_k_a[l]), "r_k": row(rwkv_r_k[l]),
            "ln_w": row(rwkv_ln_w[l]), "ln_b": row(rwkv_ln_b[l]), "seg": seg,
        }
        if l > 0:
            prm["vres_up"] = _bf(_pad_axis(vres_up[l - 1], LORA_PAD, 0))
            prm["vres_bias"] = row(vres_bias[l - 1])
        y_a, v_first = _rwkv7(pa, prm, v_first)

        y_b = _stick_breaking(qkv_bd, 0)
        y_c = _dilated_attention(qkv_c, band_biases)
        lam_init = 0.8 - 0.6 * math.exp(-0.3 * l)
        y_d = _diff_attention(qkv_bd, 3 * cw // LANES, diff_bias, diff_lambda[l].astype(F32),
                              row(diff_subln[l]), lam_init)

        x = _out_proj(y_a, y_b, y_c, y_d, _bf(w_out[l]), x, g1)
        x = _ffn(x, gain2, sc2, sh2, g2, _bf(ffn_w13[l]), _bf(ffn_w2[l]), row(final_gain),
                 final=(l == DEPTH - 1))
    return x.astype(out_dtype)
```

```python
import functools
import math

import jax
import jax.numpy as jnp
from jax import lax
from jax.experimental import pallas as pl
from jax.experimental.pallas import tpu as pltpu

F32 = jnp.float32
BF16 = jnp.bfloat16

D_MODEL = 2048
DEPTH = 2
HEAD_DIM = 64
GROUP_WIDTH = D_MODEL // 4
N_HEADS = GROUP_WIDTH // HEAD_DIM
DIFF_HALF = HEAD_DIM
DIFF_VDIM = 2 * DIFF_HALF
H_D = GROUP_WIDTH // DIFF_VDIM
W_LORA = max(32, int(round(D_MODEL ** 0.5 * 1.8 / 32)) * 32)
A_LORA = max(32, int(round(D_MODEL ** 0.5 * 1.8 / 32)) * 32)
V_LORA = max(32, int(round(D_MODEL ** 0.5 * 1.3 / 32)) * 32)
G_LORA = max(32, int(round(D_MODEL ** 0.8 / 32)) * 32)
N_A_COLS = 3 * GROUP_WIDTH + W_LORA + A_LORA + G_LORA
N_IN = N_A_COLS + 9 * GROUP_WIDTH
D_FF = ((8 * D_MODEL + 3 * 256 - 1) // (3 * 256)) * 256
DILATIONS = ((128, 1), (512, 4), (2048, 16))
BLOCK = 128
NUM_BUCKETS = 32
MAX_DISTANCE = 2048
NORM_EPS = 1e-6
RWKV_LN_EPS = 64e-5
SUBLN_EPS = 1e-5
LOG2E = math.log2(math.e)

LANES = 128
SUBLANES = 8
VMEM_LIMIT_BYTES = 56 * 1024 * 1024

LORA_PAD = 128
G_PAD = 512
A_OFF_W = 3 * GROUP_WIDTH
A_OFF_A = A_OFF_W + LORA_PAD
A_OFF_G = A_OFF_A + LORA_PAD
A_OFF_V = A_OFF_G + G_PAD
PROJ_TN = 768
FIRST_STEP_ROWS = 256
CHUNK = 64
RWKV_TILE = 256
SPAN_BLOCKS = 4
SPAN = SPAN_BLOCKS * BLOCK
DIL_TILES_PER_STEP = 4
NEG_BIG = -1e30
SB_UNDERFLOW_LOG2 = -160.0


def _bf(x):
    return x.astype(BF16)


def _mm(a, b):
    return jnp.dot(_bf(a), _bf(b), preferred_element_type=F32)


def _mm_nt(a, b):
    return lax.dot_general(_bf(a), _bf(b), (((1,), (1,)), ((), ())), preferred_element_type=F32)


def _mm_tn(a, b):
    return lax.dot_general(_bf(a), _bf(b), (((0,), (0,)), ((), ())), preferred_element_type=F32)


def _split_bf16(x, n):
    parts = []
    rem = x
    for i in range(n):
        p = _bf(rem)
        parts.append(p)
        if i + 1 < n:
            rem = rem - p.astype(F32)
    return parts


def _mm_split_lhs(a, b_bf16, n):
    out = None
    for p in _split_bf16(a, n):
        t = jnp.dot(p, b_bf16, preferred_element_type=F32)
        out = t if out is None else out + t
    return out


def _mm_split_rhs(a_bf16, b, n):
    out = None
    for p in _split_bf16(b, n):
        t = jnp.dot(a_bf16, p, preferred_element_type=F32)
        out = t if out is None else out + t
    return out


def _softplus(x):
    return jnp.maximum(x, 0.0) + jnp.log1p(jnp.exp(-jnp.abs(x)))


def _softplus2(x):
    return jnp.maximum(x, 0.0) + jnp.log2(1.0 + jnp.exp2(-jnp.abs(x)))


def _sigmoid(x):
    return jax.nn.sigmoid(x)


def _iota2(shape, dim):
    return lax.broadcasted_iota(jnp.int32, shape, dim)


def _params(sem, vmem=VMEM_LIMIT_BYTES):
    return pltpu.CompilerParams(dimension_semantics=sem, vmem_limit_bytes=vmem)


def _mod_kernel(c_ref, w_ref, b_ref, o_ref):
    c = c_ref[...]
    cond = c * _sigmoid(c)
    o_ref[0] = _mm(cond, w_ref[0]) + b_ref[0]


def _modulation(c, w_ada, b_ada):
    depth, d, n = w_ada.shape
    bsz = c.shape[0]
    tn = 1024
    return pl.pallas_call(
        _mod_kernel,
        grid=(depth, n // tn),
        in_specs=[
            pl.BlockSpec((bsz, d), lambda l, j: (0, 0)),
            pl.BlockSpec((1, d, tn), lambda l, j: (l, 0, j)),
            pl.BlockSpec((1, 1, tn), lambda l, j: (l, 0, j)),
        ],
        out_specs=pl.BlockSpec((1, bsz, tn), lambda l, j: (l, 0, j)),
        out_shape=jax.ShapeDtypeStruct((depth, bsz, n), F32),
        compiler_params=_params(("parallel", "parallel")),
        name="adaln_mod",
    )(c, w_ada, b_ada.reshape(depth, 1, n))


def _norm_mod(x, gain, sc, sh):
    ms = jnp.mean(x * x, axis=-1, keepdims=True)
    return x * lax.rsqrt(ms + NORM_EPS) * gain * (1.0 + sc) + sh


def _proj_kernel(x_ref, gain_ref, sc_ref, sh_ref, w_ref, *rest, splits):
    o_refs, h_scr = rest[:-1], rest[-1]
    j = pl.program_id(2)
    tm = h_scr.shape[0]

    @pl.when(j == 0)
    def _():
        for g in range(tm // FIRST_STEP_ROWS):
            rows = slice(g * FIRST_STEP_ROWS, (g + 1) * FIRST_STEP_ROWS)
            h = _bf(_norm_mod(x_ref[0, rows, :], gain_ref[...], sc_ref[0], sh_ref[0]))
            h_scr[rows, :] = h
            o_refs[0][0, rows, :] = jnp.dot(h, w_ref[...], preferred_element_type=F32).astype(o_refs[0].dtype)

    lo = 0
    for o_ref, n_tiles in zip(o_refs, splits):
        @pl.when((j >= max(lo, 1)) & (j < lo + n_tiles))
        def _(o_ref=o_ref):
            o_ref[0] = jnp.dot(h_scr[...], w_ref[...], preferred_element_type=F32).astype(o_ref.dtype)
        lo += n_tiles


def _norm_proj(x, gain, sc, sh, w, widths, dtypes):
    bsz, s, d = x.shape
    tm, tn = 1024, PROJ_TN
    splits = tuple(n // tn for n in widths)
    starts = [sum(splits[:k]) for k in range(len(splits))]

    def out_spec(k):
        return pl.BlockSpec((1, tm, tn),
                            lambda b, i, j: (b, i, jnp.clip(j - starts[k], 0, splits[k] - 1)))

    return pl.pallas_call(
        functools.partial(_proj_kernel, splits=splits),
        grid=(bsz, s // tm, sum(splits)),
        in_specs=[
            pl.BlockSpec((1, tm, d), lambda b, i, j: (b, i, 0)),
            pl.BlockSpec((1, d), lambda b, i, j: (0, 0)),
            pl.BlockSpec((1, 1, d), lambda b, i, j: (b, 0, 0)),
            pl.BlockSpec((1, 1, d), lambda b, i, j: (b, 0, 0)),
            pl.BlockSpec((d, tn), lambda b, i, j: (0, j)),
        ],
        out_specs=[out_spec(k) for k in range(len(widths))],
        out_shape=[jax.ShapeDtypeStruct((bsz, s, n), dt) for n, dt in zip(widths, dtypes)],
        scratch_shapes=[pltpu.VMEM((tm, d), BF16)],
        compiler_params=_params(("parallel", "parallel", "arbitrary")),
        name="in_proj",
    )(x, gain, sc, sh, w)


def _outproj_kernel(ya_ref, yb_ref, yc_ref, yd_ref, w_ref, x_ref, g_ref, o_ref):
    c = GROUP_WIDTH
    acc = jnp.dot(ya_ref[0], w_ref[0:c, :], preferred_element_type=F32)
    acc += jnp.dot(yb_ref[0], w_ref[c:2 * c, :], preferred_element_type=F32)
    acc += jnp.dot(yc_ref[0], w_ref[2 * c:3 * c, :], preferred_element_type=F32)
    acc += jnp.dot(yd_ref[0], w_ref[3 * c:4 * c, :], preferred_element_type=F32)
    o_ref[0] = x_ref[0] + g_ref[0] * acc


def _out_proj(ya, yb, yc, yd, w, x, g1):
    bsz, s, d = x.shape
    c = GROUP_WIDTH
    tm, tn = 1024, 1024
    yspec = pl.BlockSpec((1, tm, c), lambda b, i, j: (b, i, 0))
    return pl.pallas_call(
        _outproj_kernel,
        grid=(bsz, s // tm, d // tn),
        in_specs=[
            yspec, yspec, yspec, yspec,
            pl.BlockSpec((4 * c, tn), lambda b, i, j: (0, j)),
            pl.BlockSpec((1, tm, tn), lambda b, i, j: (b, i, j)),
            pl.BlockSpec((1, 1, tn), lambda b, i, j: (b, 0, j)),
        ],
        out_specs=pl.BlockSpec((1, tm, tn), lambda b, i, j: (b, i, j)),
        out_shape=jax.ShapeDtypeStruct((bsz, s, d), F32),
        compiler_params=_params(("parallel", "parallel", "arbitrary")),
        name="out_proj",
    )(ya, yb, yc, yd, w, x, g1)


def _ffn_kernel(x_ref, gain_ref, sc_ref, sh_ref, g_ref, w1_ref, w3_ref, w2_ref, fg_ref,
                o_ref, h_scr, acc_scr, *, final):
    f = pl.program_id(2)
    tm = h_scr.shape[0]

    def chunk(h):
        gate = jnp.dot(h, w1_ref[...], preferred_element_type=F32)
        up = jnp.dot(h, w3_ref[...], preferred_element_type=F32)
        act = gate * _sigmoid(gate) * up
        return jnp.dot(_bf(act), w2_ref[...], preferred_element_type=F32)

    @pl.when(f == 0)
    def _():
        for g in range(tm // FIRST_STEP_ROWS):
            rows = slice(g * FIRST_STEP_ROWS, (g + 1) * FIRST_STEP_ROWS)
            h = _bf(_norm_mod(x_ref[0, rows, :], gain_ref[...], sc_ref[0], sh_ref[0]))
            h_scr[rows, :] = h
            acc_scr[rows, :] = chunk(h)

    @pl.when(f > 0)
    def _():
        acc_scr[...] += chunk(h_scr[...])

    @pl.when(f == pl.num_programs(2) - 1)
    def _():
        xn = x_ref[0] + g_ref[0] * acc_scr[...]
        if final:
            ms = jnp.mean(xn * xn, axis=-1, keepdims=True)
            xn = xn * lax.rsqrt(ms + NORM_EPS) * fg_ref[...]
        o_ref[0] = xn


def _ffn(x, gain, sc, sh, g2, w13, w2, final_gain, final):
    bsz, s, d = x.shape
    dff = w2.shape[0]
    tm, tf = 512, 512
    nf = dff // tf
    vec = pl.BlockSpec((1, 1, d), lambda b, i, f: (b, 0, 0))
    return pl.pallas_call(
        functools.partial(_ffn_kernel, final=final),
        grid=(bsz, s // tm, nf),
        in_specs=[
            pl.BlockSpec((1, tm, d), lambda b, i, f: (b, i, 0)),
            pl.BlockSpec((1, d), lambda b, i, f: (0, 0)),
            vec, vec, vec,
            pl.BlockSpec((d, tf), lambda b, i, f: (0, f)),
            pl.BlockSpec((d, tf), lambda b, i, f: (0, f + nf)),
            pl.BlockSpec((tf, d), lambda b, i, f: (f, 0)),
            pl.BlockSpec((1, d), lambda b, i, f: (0, 0)),
        ],
        out_specs=pl.BlockSpec((1, tm, d), lambda b, i, f: (b, i, 0)),
        out_shape=jax.ShapeDtypeStruct((bsz, s, d), F32),
        scratch_shapes=[pltpu.VMEM((tm, d), BF16), pltpu.VMEM((tm, d), F32)],
        compiler_params=_params(("parallel", "parallel", "arbitrary")),
        name="ffn_final" if final else "ffn",
    )(x, gain, sc, sh, g2, w13, w13, w2, final_gain)


def _sb_kernel(q_ref, k_ref, v_ref, later_ref, o_ref):
    qs = pl.program_id(2)
    nh = LANES // HEAD_DIM
    later = later_ref[...]
    row = _iota2((SPAN, BLOCK), 0)
    col = _iota2((SPAN, BLOCK), 1)

    def span_tile(q, sl, span, run, masked):
        st = pl.multiple_of(span * SPAN, SPAN)
        z = lax.dot_general(q, k_ref[0, pl.ds(st, SPAN), sl], (((1,), (1,)), ((), ())),
                            preferred_element_type=F32)
        subs = []
        for j in range(SPAN_BLOCKS):
            zj = z[:, j * BLOCK:(j + 1) * BLOCK]
            spj = _softplus2(zj)
            l1m = -spj
            mask = None
            if masked:
                mask = (j * BLOCK + col) < row
                l1m = jnp.where(mask, l1m, 0.0)
            subs.append((zj - spj, l1m, mask))
        stacked = jnp.concatenate([_bf(l1m) for _, l1m, _ in subs], axis=0)
        within = jnp.dot(stacked, later, preferred_element_type=F32)
        atts = [None] * SPAN_BLOCKS
        for j in reversed(range(SPAN_BLOCKS)):
            log_beta, l1m, mask = subs[j]
            w = within[j * SPAN:(j + 1) * SPAN]
            att = jnp.exp2(log_beta + w + run)
            if masked:
                att = jnp.where(mask, att, 0.0)
            atts[j] = _bf(att)
            run = run + jnp.sum(l1m, axis=1, keepdims=True)
        contrib = jnp.dot(jnp.concatenate(atts, axis=1), v_ref[0, pl.ds(st, SPAN), sl],
                          preferred_element_type=F32)
        return contrib, run

    heads = []
    for h in range(nh):
        sl = slice(h * HEAD_DIM, (h + 1) * HEAD_DIM)
        heads.append((q_ref[0, :, sl], sl))

    def step(span, carry, masked):
        out = []
        for (qh, sl), (acc, run) in zip(heads, carry):
            contrib, run = span_tile(qh, sl, span, run, masked)
            out.append((acc + contrib, run))
        return tuple(out)

    init = tuple((jnp.zeros((SPAN, HEAD_DIM), F32), jnp.zeros((SPAN, 1), F32)) for _ in range(nh))
    carry = step(qs, init, True)

    def more(state):
        i, carry = state
        top = functools.reduce(jnp.maximum, [jnp.max(run) for _, run in carry])
        return (i < qs) & (top > SB_UNDERFLOW_LOG2)

    def body(state):
        i, carry = state
        return i + 1, step(qs - 1 - i, carry, False)

    _, carry = lax.while_loop(more, body, (jnp.int32(0), carry))
    for (_, sl), (acc, _) in zip(heads, carry):
        o_ref[0, :, sl] = acc.astype(o_ref.dtype)


def _stick_breaking(qkv, col0):
    bsz, s, _ = qkv.shape
    nhp = GROUP_WIDTH // LANES
    later = jnp.tril(jnp.ones((BLOCK, BLOCK), F32), -1).astype(BF16)
    return pl.pallas_call(
        _sb_kernel,
        grid=(bsz, nhp, s // SPAN),
        in_specs=[
            pl.BlockSpec((1, SPAN, LANES), lambda b, h, i: (b, i, col0 + h)),
            pl.BlockSpec((1, s, LANES), lambda b, h, i: (b, 0, col0 + nhp + h)),
            pl.BlockSpec((1, s, LANES), lambda b, h, i: (b, 0, col0 + 2 * nhp + h)),
            pl.BlockSpec((BLOCK, BLOCK), lambda b, h, i: (0, 0)),
        ],
        out_specs=pl.BlockSpec((1, SPAN, LANES), lambda b, h, i: (b, i, h)),
        out_shape=jax.ShapeDtypeStruct((bsz, s, GROUP_WIDTH), BF16),
        compiler_params=_params(("parallel", "parallel", "arbitrary")),
        name="stick_breaking",
    )(qkv, qkv, qkv, later)


def _diff_kernel(q_ref, k_ref, v_ref, vec_ref, lam_ref, subln_ref, o_ref, bias_scr, *, lam_init):
    qs = pl.program_id(2)
    causal = _iota2((SPAN, SPAN), 1) <= _iota2((SPAN, SPAN), 0)
    halves = [slice(c * DIFF_HALF, (c + 1) * DIFF_HALF) for c in range(2)]
    q = [q_ref[0, :, sl] for sl in halves]

    @pl.when((pl.program_id(1) == 0) & (qs == 0))
    def _():
        for dist in range(bias_scr.shape[0]):
            vec = jnp.broadcast_to(vec_ref[0, dist][0:1, :], (SPAN, 2 * SPAN))
            bias_scr[dist] = pltpu.roll(vec, 0, 1, stride=1, stride_axis=0)[:, :SPAN]

    def step(span, carry, masked):
        st = pl.multiple_of(span * SPAN, SPAN)
        bias = bias_scr[qs - span]
        v = v_ref[0, pl.ds(st, SPAN), :]
        out = []
        for c in range(2):
            m, l, acc = carry[c]
            s = lax.dot_general(q[c], k_ref[0, pl.ds(st, SPAN), halves[c]], (((1,), (1,)), ((), ())),
                                preferred_element_type=F32) + bias
            if masked:
                s = jnp.where(causal, s, NEG_BIG)
            m_new = jnp.maximum(m, jnp.max(s, axis=1, keepdims=True))
            alpha = jnp.exp2(m - m_new)
            p = jnp.exp2(s - m_new)
            l = alpha * l + jnp.sum(p, axis=1, keepdims=True)
            acc = alpha * acc + jnp.dot(_bf(p), v, preferred_element_type=F32)
            out.append((m_new, l, acc))
        return tuple(out)

    def init():
        return (jnp.full((SPAN, 1), NEG_BIG, F32), jnp.zeros((SPAN, 1), F32),
                jnp.zeros((SPAN, DIFF_VDIM), F32))

    carry = step(qs, (init(), init()), True)
    carry = lax.fori_loop(0, qs, lambda i, c: step(i, c, False), carry)

    lp = lam_ref[...]
    lam = (jnp.exp(jnp.sum(lp[0:1] * lp[1:2], axis=1, keepdims=True))
           - jnp.exp(jnp.sum(lp[2:3] * lp[3:4], axis=1, keepdims=True)) + lam_init)
    (_, l0, a0), (_, l1, a1) = carry
    o = a0 / l0 - lam * (a1 / l1)
    ms = jnp.mean(o * o, axis=-1, keepdims=True)
    o = o * lax.rsqrt(ms + SUBLN_EPS) * subln_ref[...] * (1.0 - lam_init)
    o_ref[0] = o.astype(o_ref.dtype)


def _diff_attention(qkv, col0, bias_vecs, lam_params, subln, lam_init):
    bsz, s, _ = qkv.shape
    ns = s // SPAN
    return pl.pallas_call(
        functools.partial(_diff_kernel, lam_init=lam_init),
        grid=(H_D, bsz, ns),
        in_specs=[
            pl.BlockSpec((1, SPAN, LANES), lambda h, b, i: (b, i, col0 + h)),
            pl.BlockSpec((1, s, LANES), lambda h, b, i: (b, 0, col0 + H_D + h)),
            pl.BlockSpec((1, s, LANES), lambda h, b, i: (b, 0, col0 + 2 * H_D + h)),
            pl.BlockSpec((1, ns, SUBLANES, 2 * SPAN), lambda h, b, i: (h, 0, 0, 0)),
            pl.BlockSpec((4, DIFF_HALF), lambda h, b, i: (0, 0)),
            pl.BlockSpec((1, DIFF_VDIM), lambda h, b, i: (0, 0)),
        ],
        out_specs=pl.BlockSpec((1, SPAN, LANES), lambda h, b, i: (b, i, h)),
        out_shape=jax.ShapeDtypeStruct((bsz, s, GROUP_WIDTH), BF16),
        scratch_shapes=[pltpu.VMEM((ns, SPAN, SPAN), F32)],
        compiler_params=_params(("arbitrary", "arbitrary", "arbitrary")),
        name="diff_attention",
    )(qkv, qkv, qkv, bias_vecs, lam_params, subln)


def _dil_kernel(q_ref, k_ref, v_ref, b1_ref, b4_ref, b16_ref, y_ref, acc_scr, m_scr, l_scr):
    s_len = q_ref.shape[1]
    first_head = _iota2((BLOCK, LANES), 1) < HEAD_DIM
    bias_refs = (b1_ref, b4_ref, b16_ref)

    def stack_heads(t):
        return jnp.concatenate([jnp.where(first_head, t, 0.0), jnp.where(first_head, 0.0, t)], axis=0)

    def unstack(t):
        return jnp.where(first_head, t[:BLOCK], t[BLOCK:])

    for pi, (window, dil) in enumerate(DILATIONS):
        n_back = window // dil
        nc = s_len // dil // BLOCK
        has_prev = nc > 1
        width = 2 * BLOCK if has_prev else BLOCK
        qi = _iota2((2 * BLOCK, width), 0) & (BLOCK - 1)
        ki = _iota2((2 * BLOCK, width), 1)
        steps = qi + (BLOCK if has_prev else 0) - ki
        band = (steps >= 0) & (steps <= n_back)
        bias = bias_refs[pi][0]
        if not has_prev:
            bias = bias[:, BLOCK:]

        def body(it, carry, dil=dil, nc=nc, has_prev=has_prev, band=band, bias=bias, pi=pi, ki=ki):
            loads = []
            for u in range(DIL_TILES_PER_STEP):
                idx = it * DIL_TILES_PER_STEP + u
                r, c = idx // nc, idx % nc
                rows = pl.ds(r + dil * BLOCK * c, BLOCK, stride=dil)
                q = stack_heads(q_ref[0, rows, :])
                k = k_ref[0, rows, :]
                v = v_ref[0, rows, :]
                if has_prev:
                    prows = pl.ds(r + dil * BLOCK * jnp.maximum(c - 1, 0), BLOCK, stride=dil)
                    k = jnp.concatenate([k_ref[0, prows, :], k], axis=0)
                    v = jnp.concatenate([v_ref[0, prows, :], v], axis=0)
                loads.append((rows, c, _bf(q), _bf(k), _bf(v)))
            ss = [lax.dot_general(q, k, (((1,), (1,)), ((), ())), preferred_element_type=F32) + bias
                  for _, _, q, k, _ in loads]
            if has_prev:
                ss = [jnp.where(band & (ki >= jnp.where(c > 0, 0, BLOCK)), s, NEG_BIG)
                      for s, (_, c, _, _, _) in zip(ss, loads)]
            else:
                ss = [jnp.where(band, s, NEG_BIG) for s in ss]
            ms = [jnp.max(s, axis=1, keepdims=True) for s in ss]
            ps = [jnp.exp2(s - m) for s, m in zip(ss, ms)]
            ls = [jnp.sum(p, axis=1, keepdims=True) for p in ps]
            os_ = [jnp.dot(_bf(p), v, preferred_element_type=F32) for p, (_, _, _, _, v) in zip(ps, loads)]
            for (rows, _, _, _, _), m, l, o in zip(loads, ms, ls, os_):
                acc_scr[pi, rows, :] = unstack(o)
                m_scr[pi, rows, :] = unstack(jnp.broadcast_to(m, (2 * BLOCK, LANES)))
                l_scr[pi, rows, :] = unstack(jnp.broadcast_to(l, (2 * BLOCK, LANES)))
            return carry

        lax.fori_loop(0, s_len // BLOCK // DIL_TILES_PER_STEP, body, 0)

    tm = 2 * BLOCK

    def merge(i, carry):
        rows = pl.ds(pl.multiple_of(i * tm, tm), tm)
        m1, m2, m3 = m_scr[0, rows, :], m_scr[1, rows, :], m_scr[2, rows, :]
        m = jnp.maximum(jnp.maximum(m1, m2), m3)
        e1, e2, e3 = jnp.exp2(m1 - m), jnp.exp2(m2 - m), jnp.exp2(m3 - m)
        num = e1 * acc_scr[0, rows, :] + e2 * acc_scr[1, rows, :] + e3 * acc_scr[2, rows, :]
        den = e1 * l_scr[0, rows, :] + e2 * l_scr[1, rows, :] + e3 * l_scr[2, rows, :]
        y_ref[0, rows, :] = (num / den).astype(y_ref.dtype)
        return carry

    lax.fori_loop(0, s_len // tm, merge, 0)


def _dilated_attention(qkv_c, biases):
    bsz, s, _ = qkv_c.shape
    nhp = GROUP_WIDTH // LANES
    bspec = pl.BlockSpec((1, 2 * BLOCK, 2 * BLOCK), lambda b, h: (h, 0, 0))
    scr = pltpu.VMEM((len(DILATIONS), s, LANES), F32)
    return pl.pallas_call(
        _dil_kernel,
        grid=(bsz, nhp),
        in_specs=[
            pl.BlockSpec((1, s, LANES), lambda b, h: (b, 0, h)),
            pl.BlockSpec((1, s, LANES), lambda b, h: (b, 0, nhp + h)),
            pl.BlockSpec((1, s, LANES), lambda b, h: (b, 0, 2 * nhp + h)),
            bspec, bspec, bspec,
        ],
        out_specs=pl.BlockSpec((1, s, LANES), lambda b, h: (b, 0, h)),
        out_shape=jax.ShapeDtypeStruct((bsz, s, GROUP_WIDTH), BF16),
        scratch_shapes=[scr, scr, scr],
        compiler_params=_params(("parallel", "parallel")),
        name="dilated_attention",
    )(qkv_c, qkv_c, qkv_c, *biases)


def _unit_lower_inverses(ns):
    size_all = 2 * CHUNK
    row = _iota2((size_all, size_all), 0)
    col = _iota2((size_all, size_all), 1)
    eye = jnp.where(row == col, 1.0, 0.0)
    base = 8
    same8 = (row >> 3) == (col >> 3)
    n8 = [jnp.where(same8, n, 0.0) for n in ns]
    n2 = [_mm(a, a) for a in n8]
    n4 = [_mm(a, a) for a in n2]
    ts = [eye + a for a in n8]
    ts = [t + _mm(t, b) for t, b in zip(ts, n2)]
    ts = [t + _mm(t, b) for t, b in zip(ts, n4)]
    size = 2 * base
    while size <= CHUNK:
        half = size // 2
        shift = size.bit_length() - 1
        off = (((row >> shift) == (col >> shift)) & ((row & (size - 1)) >= half)
               & ((col & (size - 1)) < half))
        us = [_mm(t, jnp.where(off, n, 0.0)) for t, n in zip(ts, ns)]
        ts = [t + _mm(u, t) for t, u in zip(ts, us)]
        size *= 2
    return ts


def _rwkv_kernel(*refs, has_vres):
    if has_vres:
        (pa_ref, mu_ref, w0_ref, wup_ref, a0_ref, aup_ref, gup_ref, kk_ref, ka_ref, rk_ref,
         lnw_ref, lnb_ref, seg_ref, cum_ref, vf_ref, vup_ref, vb_ref,
         y_ref, prev_scr, s_scr) = refs
    else:
        (pa_ref, mu_ref, w0_ref, wup_ref, a0_ref, aup_ref, gup_ref, kk_ref, ka_ref, rk_ref,
         lnw_ref, lnb_ref, seg_ref, cum_ref,
         y_ref, vf_out_ref, prev_scr, s_scr) = refs
    c = GROUP_WIDTH
    tt = pa_ref.shape[1]

    @pl.when(pl.program_id(1) == 0)
    def _():
        prev_scr[...] = jnp.zeros_like(prev_scr)
        s_scr[...] = jnp.zeros_like(s_scr)

    seg = seg_ref[...]
    seg_sum = lambda t: _mm_split_lhs(t, seg, 2)

    pa = pa_ref[0]
    rolled = pltpu.roll(pa, 1, 0)
    first = _iota2(pa.shape, 0) == 0
    prev = jnp.where(first, prev_scr[...], rolled)
    prev_scr[...] = pa[tt - 1:tt, :]
    sh = pa + (prev - pa) * mu_ref[...]
    r = sh[:, 0:c]
    k = sh[:, c:2 * c]
    v = sh[:, 2 * c:3 * c]
    w_lo = sh[:, A_OFF_W:A_OFF_W + LORA_PAD]
    a_lo = sh[:, A_OFF_A:A_OFF_A + LORA_PAD]
    g_lo = sh[:, A_OFF_G:A_OFF_G + G_PAD]
    if has_vres:
        pv = sh[:, A_OFF_V:A_OFF_V + LORA_PAD]
        v = v + (vf_ref[0] - v) * _sigmoid(vb_ref[...] + _mm(pv, vup_ref[...]))
    else:
        vf_out_ref[0] = v

    w_log = -_softplus(-(w0_ref[...] + _mm(jnp.tanh(w_lo), wup_ref[...]))) - 0.5
    lw = -jnp.exp(w_log)
    a = _sigmoid(a0_ref[...] + _mm(a_lo, aup_ref[...]))
    g = _mm(_sigmoid(g_lo), gup_ref[...])
    kkr = k * kk_ref[...]
    kk = kkr * lax.rsqrt(jnp.maximum(seg_sum(kkr * kkr), 1e-24))
    k = k * (1.0 + (a - 1.0) * ka_ref[...])

    cum = _mm_split_rhs(cum_ref[...], lw, 3)
    eg = jnp.exp(cum)
    eig = jnp.exp(-cum)
    at = -kk * jnp.exp(cum - lw)
    rt = r * eg
    bt = kk * a * eig
    kt = k * eig

    pair_rows = 2 * CHUNK
    first_head = _iota2((CHUNK, LANES), 1) < HEAD_DIM
    row = _iota2((pair_rows, pair_rows), 0)
    col = _iota2((pair_rows, pair_rows), 1)
    strict = (row & (CHUNK - 1)) > (col & (CHUNK - 1))
    lower = (row & (CHUNK - 1)) >= (col & (CHUNK - 1))
    eye = row == col

    def stack(t):
        return jnp.concatenate([jnp.where(first_head, t, 0.0), jnp.where(first_head, 0.0, t)], axis=0)

    probs = [(ci, p) for ci in range(tt // CHUNK) for p in range(c // LANES)]
    cut = lambda t, ci, p: t[ci * CHUNK:(ci + 1) * CHUNK, p * LANES:(p + 1) * LANES]
    g_end = [cut(eg, ci, p)[CHUNK - 1:CHUNK, :] for ci, p in probs]
    at_p = [stack(cut(at, ci, p)) for ci, p in probs]
    rt_p = [stack(cut(rt, ci, p)) for ci, p in probs]
    bt_p = [stack(cut(bt, ci, p)) for ci, p in probs]
    kt_p = [stack(cut(kt, ci, p)) for ci, p in probs]
    v_p = [stack(cut(v, ci, p)) for ci, p in probs]
    bg_p = [b_ * ge for b_, ge in zip(bt_p, g_end)]
    kg_p = [k_ * ge for k_, ge in zip(kt_p, g_end)]
    grams = [_mm_nt(jnp.concatenate([a_, r_], axis=0), jnp.concatenate([b_, k_], axis=0))
             for a_, r_, b_, k_ in zip(at_p, rt_p, bt_p, kt_p)]
    a_ab = [jnp.where(strict, gm[:pair_rows, :pair_rows], 0.0) for gm in grams]
    a_ak = [jnp.where(strict, gm[:pair_rows, pair_rows:], 0.0) for gm in grams]
    a_rb = [jnp.where(lower, gm[pair_rows:, :pair_rows], 0.0) for gm in grams]
    a_rk = [jnp.where(lower, gm[pair_rows:, pair_rows:], 0.0) for gm in grams]
    both_v = [_mm(jnp.concatenate([a_, r_], axis=0), v_) for a_, r_, v_ in zip(a_ak, a_rk, v_p)]
    akv = [x_[:pair_rows] for x_ in both_v]
    rkv = [x_[pair_rows:] for x_ in both_v]
    ts = _unit_lower_inverses(a_ab)
    pq = [_mm(t, jnp.concatenate([a_, x_], axis=1)) for t, a_, x_ in zip(ts, at_p, akv)]
    ps = [x_[:, :LANES] for x_ in pq]
    qs = [x_[:, LANES:] for x_ in pq]
    rb_pq = [_mm(a_, x_) for a_, x_ in zip(a_rb, pq)]
    r2 = [r_ + x_[:, :LANES] for r_, x_ in zip(rt_p, rb_pq)]
    y0 = [x_[:, LANES:] + k_ for x_, k_ in zip(rb_pq, rkv)]
    ms = [jnp.where(eye, jnp.broadcast_to(ge, (pair_rows, LANES)), 0.0) + _mm_tn(p_, b_)
          for ge, p_, b_ in zip(g_end, ps, bg_p)]
    zs = [_mm_tn(jnp.concatenate([q_, v_], axis=0), jnp.concatenate([b_, k_], axis=0))
          for q_, b_, v_, k_ in zip(qs, bg_p, v_p, kg_p)]

    state = [s_scr[p] for p in range(c // LANES)]
    y_rows = []
    for ci in range(tt // CHUNK):
        y_pairs = []
        for p in range(c // LANES):
            i = ci * (c // LANES) + p
            y_st = _mm_nt(r2[i], state[p]) + y0[i]
            y_pairs.append(y_st[:CHUNK] + y_st[CHUNK:])
            state[p] = _mm(state[p], ms[i]) + zs[i]
        y_rows.append(jnp.concatenate(y_pairs, axis=1))
    for p in range(c // LANES):
        s_scr[p] = state[p]
    y = jnp.concatenate(y_rows, axis=0)

    mu = seg_sum(y) * (1.0 / HEAD_DIM)
    d = y - mu
    var = seg_sum(d * d) * (1.0 / HEAD_DIM)
    yn = d * lax.rsqrt(var + RWKV_LN_EPS) * lnw_ref[...] + lnb_ref[...]
    bonus = seg_sum(r * k * rk_ref[...]) * v
    y_ref[0] = ((yn + bonus) * g).astype(y_ref.dtype)


def _rwkv7(pa, prm, v_first):
    bsz, s, _ = pa.shape
    na = prm["mu"].shape[1]
    c = GROUP_WIDTH
    tt = RWKV_TILE
    has_vres = v_first is not None
    full = lambda shape: pl.BlockSpec(shape, lambda b, t: (0,) * len(shape))
    tile = lambda w: pl.BlockSpec((1, tt, w), lambda b, t: (b, t, 0))
    idx = jnp.arange(tt)
    cum_ones = ((idx[:, None] >= idx[None, :]) & (idx[:, None] // CHUNK == idx[None, :] // CHUNK)).astype(BF16)
    in_specs = [tile(na), full((1, na)), full((1, c)), full((LORA_PAD, c)), full((1, c)),
                full((LORA_PAD, c)), full((G_PAD, c)), full((1, c)), full((1, c)), full((1, c)),
                full((1, c)), full((1, c)), full((c, c)), full((tt, tt))]
    args = [pa, prm["mu"], prm["w0"], prm["w_up"], prm["a0"], prm["a_up"], prm["g_up"],
            prm["k_k"], prm["k_a"], prm["r_k"], prm["ln_w"], prm["ln_b"], prm["seg"], cum_ones]
    y_shape = jax.ShapeDtypeStruct((bsz, s, c), BF16)
    if has_vres:
        in_specs += [tile(c), full((LORA_PAD, c)), full((1, c))]
        args += [v_first, prm["vres_up"], prm["vres_bias"]]
        out_specs, out_shape = tile(c), y_shape
    else:
        out_specs = [tile(c), tile(c)]
        out_shape = [y_shape, jax.ShapeDtypeStruct((bsz, s, c), F32)]
    res = pl.pallas_call(
        functools.partial(_rwkv_kernel, has_vres=has_vres),
        grid=(bsz, s // tt),
        in_specs=in_specs,
        out_specs=out_specs,
        out_shape=out_shape,
        scratch_shapes=[pltpu.VMEM((1, na), F32), pltpu.VMEM((c // LANES, LANES, LANES), F32)],
        compiler_params=_params(("parallel", "arbitrary")),
        name="rwkv7_vres" if has_vres else "rwkv7",
    )(*args)
    if has_vres:
        return res, v_first
    return res[0], res[1]


def _rel_bucket(dist):
    max_exact = NUM_BUCKETS // 2
    d_f = jnp.maximum(dist, 1).astype(F32)
    large = max_exact + (jnp.log(d_f / max_exact) / math.log(MAX_DISTANCE / max_exact)
                         * (NUM_BUCKETS - max_exact)).astype(jnp.int32)
    large = jnp.minimum(large, NUM_BUCKETS - 1)
    return jnp.where(dist < max_exact, dist, large)


def _toeplitz(u, n_rows, n_cols):
    length = n_rows + n_cols - 1
    g = jnp.roll(jnp.flip(u, axis=-1), -(n_rows - 1), axis=-1)
    flat = jnp.tile(g, (1,) * (u.ndim - 1) + (n_rows,))[..., :n_rows * (length - 1)]
    return flat.reshape(u.shape[:-1] + (n_rows, length - 1))[..., :n_cols]


def _diff_bias_vectors(rel_tab, ns):
    n = jnp.arange(2 * SPAN)
    rel = jnp.where(n < SPAN, -n, 2 * SPAN - n)
    dist = jnp.arange(ns)[:, None] * SPAN + rel[None, :]
    u = rel_tab[_rel_bucket(jnp.maximum(dist, 0))] * LOG2E
    u = jnp.transpose(u, (2, 0, 1)).astype(F32)
    return jnp.broadcast_to(u[:, :, None, :], (H_D, ns, SUBLANES, 2 * SPAN))


def _band_bias(rel_tab, dil):
    steps = jnp.arange(3 * BLOCK - 1) - (BLOCK - 1)
    u = rel_tab[_rel_bucket(jnp.maximum(steps, 0) * dil)] * LOG2E
    t = _toeplitz(jnp.transpose(u, (1, 0)).astype(F32), BLOCK, 2 * BLOCK)
    return t.reshape(N_HEADS * HEAD_DIM // LANES, 2 * BLOCK, 2 * BLOCK)


def _pad_axis(t, size, axis):
    pad = [(0, 0)] * t.ndim
    pad[axis] = (0, size - t.shape[axis])
    return jnp.pad(t, pad)


def _layout_mixer_a(t, vres=None):
    c = GROUP_WIDTH
    o_w, o_a, o_g = 3 * c, 3 * c + W_LORA, 3 * c + W_LORA + A_LORA
    parts = [t[..., :o_w], _pad_axis(t[..., o_w:o_a], LORA_PAD, -1),
             _pad_axis(t[..., o_a:o_g], LORA_PAD, -1), _pad_axis(t[..., o_g:N_A_COLS], G_PAD, -1)]
    if vres is not None:
        parts.append(_pad_axis(vres, PROJ_TN, -1))
    return jnp.concatenate(parts, axis=-1)


def _scale_queries(w_qkv, scale):
    return jnp.concatenate([w_qkv[:, :GROUP_WIDTH] * scale, w_qkv[:, GROUP_WIDTH:]], axis=1)


def kernel(x, c, w_ada, b_ada, norm_gain, w_in, w_out, rel_bias, rwkv_mu, rwkv_w0, rwkv_w_up,
           rwkv_a0, rwkv_a_up, rwkv_g_up, rwkv_k_k, rwkv_k_a, rwkv_r_k, rwkv_ln_w, rwkv_ln_b,
           vres_down, vres_mu, vres_up, vres_bias, diff_lambda, diff_subln, ffn_w13, ffn_w2,
           final_gain):
    out_dtype = x.dtype
    bsz, s, d = x.shape
    cw = GROUP_WIDTH
    x = x.astype(F32)
    row = lambda t: t.reshape(1, -1).astype(F32)

    mod = _modulation(c.astype(F32), w_ada, b_ada)
    rel_tab_c = rel_bias[:, :N_HEADS].astype(F32)
    rel_tab_d = rel_bias[:, N_HEADS:].astype(F32)
    band_biases = [_band_bias(rel_tab_c, dil) for _, dil in DILATIONS]
    diff_bias = _diff_bias_vectors(rel_tab_d, s // SPAN)
    seg = (jnp.arange(cw)[:, None] // HEAD_DIM == jnp.arange(cw)[None, :] // HEAD_DIM).astype(BF16)
    logit_scale = LOG2E * HEAD_DIM ** -0.5

    v_first = None
    for l in range(DEPTH):
        sh1, sc1, g1, sh2, sc2, g2 = [mod[l, :, None, i * d:(i + 1) * d] for i in range(6)]
        gain1, gain2 = row(norm_gain[l, 0]), row(norm_gain[l, 1])
        vres_w = vres_down[l - 1] if l > 0 else None
        w_l = w_in[l].astype(F32)
        w_a = _layout_mixer_a(w_l[:, :N_A_COLS], vres_w)
        o_b, o_c, o_d = N_A_COLS, N_A_COLS + 3 * cw, N_A_COLS + 6 * cw
        w_all = _bf(jnp.concatenate([w_a, _scale_queries(w_l[:, o_b:o_c], logit_scale),
                                     _scale_queries(w_l[:, o_d:N_IN], logit_scale),
                                     _scale_queries(w_l[:, o_c:o_d], logit_scale)], axis=1))
        na_used = A_OFF_V + (LORA_PAD if l > 0 else 0)
        mu = _layout_mixer_a(rwkv_mu[l], vres_mu[l - 1] if l > 0 else None)[..., :na_used]

        pa, qkv_bd, qkv_c = _norm_proj(x, gain1, sc1, sh1, w_all, (w_a.shape[1], 6 * cw, 3 * cw),
                                       (F32, BF16, F32))

        prm = {
            "mu": row(mu), "w0": row(rwkv_w0[l]), "a0": row(rwkv_a0[l]),
            "w_up": _bf(_pad_axis(rwkv_w_up[l], LORA_PAD, 0)),
            "a_up": _bf(_pad_axis(rwkv_a_up[l], LORA_PAD, 0)),
            "g_up": _bf(_pad_axis(rwkv_g_up[l], G_PAD, 0)),
            "k_k": row(rwkv_k_k[l]), "k_a": row(rwkv_k_a[l]), "r_k": row(rwkv_r_k[l]),
            "ln_w": row(rwkv_ln_w[l]), "ln_b": row(rwkv_ln_b[l]), "seg": seg,
        }
        if l > 0:
            prm["vres_up"] = _bf(_pad_axis(vres_up[l - 1], LORA_PAD, 0))
            prm["vres_bias"] = row(vres_bias[l - 1])
        y_a, v_first = _rwkv7(pa, prm, v_first)

        y_b = _stick_breaking(qkv_bd, 0)
        y_c = _dilated_attention(qkv_c, band_biases)
        lam_init = 0.8 - 0.6 * math.exp(-0.3 * l)
        y_d = _diff_attention(qkv_bd, 3 * cw // LANES, diff_bias, diff_lambda[l].astype(F32),
                              row(diff_subln[l]), lam_init)

        x = _out_proj(y_a, y_b, y_c, y_d, _bf(w_out[l]), x, g1)
        x = _ffn(x, gain2, sc2, sh2, g2, _bf(ffn_w13[l]), _bf(ffn_w2[l]), row(final_gain),
                 final=(l == DEPTH - 1))
    return x.astype(out_dtype)
```

```python
import functools
import math

import jax
import jax.numpy as jnp
from jax import lax
from jax.experimental import pallas as pl
from jax.experimental.pallas import tpu as pltpu

F32 = jnp.float32
BF16 = jnp.bfloat16

D_MODEL = 2048
DEPTH = 2
HEAD_DIM = 64
GROUP_WIDTH = D_MODEL // 4
N_HEADS = GROUP_WIDTH // HEAD_DIM
DIFF_HALF = HEAD_DIM
DIFF_VDIM = 2 * DIFF_HALF
H_D = GROUP_WIDTH // DIFF_VDIM
W_LORA = max(32, int(round(D_MODEL ** 0.5 * 1.8 / 32)) * 32)
A_LORA = max(32, int(round(D_MODEL ** 0.5 * 1.8 / 32)) * 32)
V_LORA = max(32, int(round(D_MODEL ** 0.5 * 1.3 / 32)) * 32)
G_LORA = max(32, int(round(D_MODEL ** 0.8 / 32)) * 32)
N_A_COLS = 3 * GROUP_WIDTH + W_LORA + A_LORA + G_LORA
N_IN = N_A_COLS + 9 * GROUP_WIDTH
D_FF = ((8 * D_MODEL + 3 * 256 - 1) // (3 * 256)) * 256
DILATIONS = ((128, 1), (512, 4), (2048, 16))
BLOCK = 128
NUM_BUCKETS = 32
MAX_DISTANCE = 2048
NORM_EPS = 1e-6
RWKV_LN_EPS = 64e-5
SUBLN_EPS = 1e-5
LOG2E = math.log2(math.e)

LANES = 128
SUBLANES = 8
VMEM_LIMIT_BYTES = 56 * 1024 * 1024
FFN_VMEM_LIMIT_BYTES = 60 * 1024 * 1024

LORA_PAD = 128
G_PAD = 512
A_OFF_W = 3 * GROUP_WIDTH
A_OFF_A = A_OFF_W + LORA_PAD
A_OFF_G = A_OFF_A + LORA_PAD
A_OFF_V = A_OFF_G + G_PAD
PROJ_TN = 768
FIRST_STEP_ROWS = 256
CHUNK = 64
RWKV_TILE = 256
SPAN_BLOCKS = 4
SPAN = SPAN_BLOCKS * BLOCK
DIL_TILES_PER_STEP = 4
NEG_BIG = -1e30
SB_UNDERFLOW_LOG2 = 160.0


def _bf(x):
    return x.astype(BF16)


def _mm(a, b):
    return jnp.dot(_bf(a), _bf(b), preferred_element_type=F32)


def _mm_nt(a, b):
    return lax.dot_general(_bf(a), _bf(b), (((1,), (1,)), ((), ())), preferred_element_type=F32)


def _mm_tn(a, b):
    return lax.dot_general(_bf(a), _bf(b), (((0,), (0,)), ((), ())), preferred_element_type=F32)


def _split_bf16(x, n):
    parts = []
    rem = x
    for i in range(n):
        p = _bf(rem)
        parts.append(p)
        if i + 1 < n:
            rem = rem - p.astype(F32)
    return parts


def _mm_split_lhs(a, b_bf16, n):
    out = None
    for p in _split_bf16(a, n):
        t = jnp.dot(p, b_bf16, preferred_element_type=F32)
        out = t if out is None else out + t
    return out


def _mm_split_rhs(a_bf16, b, n):
    out = None
    for p in _split_bf16(b, n):
        t = jnp.dot(a_bf16, p, preferred_element_type=F32)
        out = t if out is None else out + t
    return out


def _softplus(x):
    return jnp.maximum(x, 0.0) + jnp.log1p(jnp.exp(-jnp.abs(x)))


def _softplus2(x):
    sign_bit = jnp.uint32(0x80000000)
    neg_abs = lax.bitcast_convert_type(lax.bitcast_convert_type(x, jnp.uint32) | sign_bit, F32)
    return jnp.maximum(x, 0.0) + jnp.log2(1.0 + jnp.exp2(neg_abs))


def _sigmoid(x):
    return jax.nn.sigmoid(x)


def _iota2(shape, dim):
    return lax.broadcasted_iota(jnp.int32, shape, dim)


def _params(sem, vmem=VMEM_LIMIT_BYTES):
    return pltpu.CompilerParams(dimension_semantics=sem, vmem_limit_bytes=vmem)


def _mod_kernel(c_ref, w_ref, b_ref, o_ref):
    c = c_ref[...]
    cond = c * _sigmoid(c)
    o_ref[0] = _mm(cond, w_ref[0]) + b_ref[0]


def _modulation(c, w_ada, b_ada):
    depth, d, n = w_ada.shape
    bsz = c.shape[0]
    tn = 1024
    return pl.pallas_call(
        _mod_kernel,
        grid=(depth, n // tn),
        in_specs=[
            pl.BlockSpec((bsz, d), lambda l, j: (0, 0)),
            pl.BlockSpec((1, d, tn), lambda l, j: (l, 0, j)),
            pl.BlockSpec((1, 1, tn), lambda l, j: (l, 0, j)),
        ],
        out_specs=pl.BlockSpec((1, bsz, tn), lambda l, j: (l, 0, j)),
        out_shape=jax.ShapeDtypeStruct((depth, bsz, n), F32),
        compiler_params=_params(("parallel", "parallel")),
        name="adaln_mod",
    )(c, w_ada, b_ada.reshape(depth, 1, n))


def _norm_mod(x, gain, sc, sh):
    ms = jnp.mean(x * x, axis=-1, keepdims=True)
    return x * lax.rsqrt(ms + NORM_EPS) * gain * (1.0 + sc) + sh


def _proj_kernel(x_ref, gain_ref, sc_ref, sh_ref, w_ref, *rest, splits):
    o_refs, h_scr = rest[:-1], rest[-1]
    j = pl.program_id(2)
    tm = h_scr.shape[0]

    @pl.when(j == 0)
    def _():
        for g in range(tm // FIRST_STEP_ROWS):
            rows = slice(g * FIRST_STEP_ROWS, (g + 1) * FIRST_STEP_ROWS)
            h = _bf(_norm_mod(x_ref[0, rows, :], gain_ref[...], sc_ref[0], sh_ref[0]))
            h_scr[rows, :] = h
            o_refs[0][0, rows, :] = jnp.dot(h, w_ref[...], preferred_element_type=F32).astype(o_refs[0].dtype)

    lo = 0
    for o_ref, n_tiles in zip(o_refs, splits):
        @pl.when((j >= max(lo, 1)) & (j < lo + n_tiles))
        def _(o_ref=o_ref):
            o_ref[0] = jnp.dot(h_scr[...], w_ref[...], preferred_element_type=F32).astype(o_ref.dtype)
        lo += n_tiles


def _norm_proj(x, gain, sc, sh, w, widths, dtypes):
    bsz, s, d = x.shape
    tm, tn = 1024, PROJ_TN
    splits = tuple(n // tn for n in widths)
    starts = [sum(splits[:k]) for k in range(len(splits))]

    def out_spec(k):
        return pl.BlockSpec((1, tm, tn),
                            lambda b, i, j: (b, i, jnp.clip(j - starts[k], 0, splits[k] - 1)))

    return pl.pallas_call(
        functools.partial(_proj_kernel, splits=splits),
        grid=(bsz, s // tm, sum(splits)),
        in_specs=[
            pl.BlockSpec((1, tm, d), lambda b, i, j: (b, i, 0)),
            pl.BlockSpec((1, d), lambda b, i, j: (0, 0)),
            pl.BlockSpec((1, 1, d), lambda b, i, j: (b, 0, 0)),
            pl.BlockSpec((1, 1, d), lambda b, i, j: (b, 0, 0)),
            pl.BlockSpec((d, tn), lambda b, i, j: (0, j)),
        ],
        out_specs=[out_spec(k) for k in range(len(widths))],
        out_shape=[jax.ShapeDtypeStruct((bsz, s, n), dt) for n, dt in zip(widths, dtypes)],
        scratch_shapes=[pltpu.VMEM((tm, d), BF16)],
        compiler_params=_params(("parallel", "parallel", "arbitrary")),
        name="in_proj",
    )(x, gain, sc, sh, w)


def _outproj_kernel(ya_ref, yb_ref, yc_ref, yd_ref, w_ref, x_ref, g_ref, o_ref):
    c = GROUP_WIDTH
    acc = jnp.dot(ya_ref[0], w_ref[0:c, :], preferred_element_type=F32)
    acc += jnp.dot(yb_ref[0], w_ref[c:2 * c, :], preferred_element_type=F32)
    acc += jnp.dot(yc_ref[0], w_ref[2 * c:3 * c, :], preferred_element_type=F32)
    acc += jnp.dot(yd_ref[0], w_ref[3 * c:4 * c, :], preferred_element_type=F32)
    o_ref[0] = x_ref[0] + g_ref[0] * acc


def _out_proj(ya, yb, yc, yd, w, x, g1):
    bsz, s, d = x.shape
    c = GROUP_WIDTH
    tm, tn = 1024, 1024
    yspec = pl.BlockSpec((1, tm, c), lambda b, i, j: (b, i, 0))
    return pl.pallas_call(
        _outproj_kernel,
        grid=(bsz, s // tm, d // tn),
        in_specs=[
            yspec, yspec, yspec, yspec,
            pl.BlockSpec((4 * c, tn), lambda b, i, j: (0, j)),
            pl.BlockSpec((1, tm, tn), lambda b, i, j: (b, i, j)),
            pl.BlockSpec((1, 1, tn), lambda b, i, j: (b, 0, j)),
        ],
        out_specs=pl.BlockSpec((1, tm, tn), lambda b, i, j: (b, i, j)),
        out_shape=jax.ShapeDtypeStruct((bsz, s, d), F32),
        compiler_params=_params(("parallel", "parallel", "arbitrary")),
        name="out_proj",
    )(ya, yb, yc, yd, w, x, g1)


def _ffn_kernel(x_ref, gain_ref, sc_ref, sh_ref, g_ref, w1_ref, w3_ref, w2_ref, fg_ref,
                o_ref, h_scr, *, final):
    f = pl.program_id(2)
    tm = h_scr.shape[0]
    acc_scr = o_ref.at[0]

    def chunk(h):
        gate = jnp.dot(h, w1_ref[...], preferred_element_type=F32)
        up = jnp.dot(h, w3_ref[...], preferred_element_type=F32)
        act = gate * _sigmoid(gate) * up
        return jnp.dot(_bf(act), w2_ref[...], preferred_element_type=F32)

    @pl.when(f == 0)
    def _():
        for g in range(tm // FIRST_STEP_ROWS):
            rows = slice(g * FIRST_STEP_ROWS, (g + 1) * FIRST_STEP_ROWS)
            h = _bf(_norm_mod(x_ref[0, rows, :], gain_ref[...], sc_ref[0], sh_ref[0]))
            h_scr[rows, :] = h
            acc_scr[rows, :] = chunk(h)

    @pl.when(f > 0)
    def _():
        acc_scr[...] += chunk(h_scr[...])

    @pl.when(f == pl.num_programs(2) - 1)
    def _():
        xn = x_ref[0] + g_ref[0] * acc_scr[...]
        if final:
            ms = jnp.mean(xn * xn, axis=-1, keepdims=True)
            xn = xn * lax.rsqrt(ms + NORM_EPS) * fg_ref[...]
        o_ref[0] = xn


def _ffn(x, gain, sc, sh, g2, w13, w2, final_gain, final):
    bsz, s, d = x.shape
    dff = w2.shape[0]
    tm, tf = 1024, 512
    nf = dff // tf
    vec = pl.BlockSpec((1, 1, d), lambda b, i, f: (b, 0, 0))
    return pl.pallas_call(
        functools.partial(_ffn_kernel, final=final),
        grid=(bsz, s // tm, nf),
        in_specs=[
            pl.BlockSpec((1, tm, d), lambda b, i, f: (b, i, 0), pipeline_mode=pl.Buffered(1)),
            pl.BlockSpec((1, d), lambda b, i, f: (0, 0)),
            vec, vec, vec,
            pl.BlockSpec((d, tf), lambda b, i, f: (0, f)),
            pl.BlockSpec((d, tf), lambda b, i, f: (0, f + nf)),
            pl.BlockSpec((tf, d), lambda b, i, f: (f, 0)),
            pl.BlockSpec((1, d), lambda b, i, f: (0, 0)),
        ],
        out_specs=pl.BlockSpec((1, tm, d), lambda b, i, f: (b, i, 0)),
        out_shape=jax.ShapeDtypeStruct((bsz, s, d), F32),
        scratch_shapes=[pltpu.VMEM((tm, d), BF16)],
        compiler_params=_params(("parallel", "parallel", "arbitrary"), FFN_VMEM_LIMIT_BYTES),
        name="ffn_final" if final else "ffn",
    )(x, gain, sc, sh, g2, w13, w13, w2, final_gain)


def _sb_kernel(q_ref, k_ref, v_ref, later_ref, o_ref):
    qs = pl.program_id(2)
    nh = LANES // HEAD_DIM
    later = later_ref[...]
    row = _iota2((SPAN, BLOCK), 0)
    col = _iota2((SPAN, BLOCK), 1)

    def span_tile(q, sl, span, run, masked):
        st = pl.multiple_of(span * SPAN, SPAN)
        z = lax.dot_general(q, k_ref[0, pl.ds(st, SPAN), sl], (((1,), (1,)), ((), ())),
                            preferred_element_type=F32)
        subs = []
        for j in range(SPAN_BLOCKS):
            zj = z[:, j * BLOCK:(j + 1) * BLOCK]
            cost = _softplus2(zj)
            mask = None
            if masked:
                mask = (j * BLOCK + col) < row
                cost = jnp.where(mask, cost, 0.0)
            subs.append((zj, cost, mask))
        stacked = jnp.concatenate([_bf(cost) for _, cost, _ in subs], axis=0)
        within = jnp.dot(stacked, later, preferred_element_type=F32)
        atts = [None] * SPAN_BLOCKS
        for j in reversed(range(SPAN_BLOCKS)):
            zj, cost, mask = subs[j]
            w = within[j * SPAN:(j + 1) * SPAN]
            att = jnp.exp2(zj - cost - w - run)
            if masked:
                att = jnp.where(mask, att, 0.0)
            atts[j] = _bf(att)
            run = run + jnp.sum(cost, axis=1, keepdims=True)
        contrib = jnp.dot(jnp.concatenate(atts, axis=1), v_ref[0, pl.ds(st, SPAN), sl],
                          preferred_element_type=F32)
        return contrib, run

    heads = []
    for h in range(nh):
        sl = slice(h * HEAD_DIM, (h + 1) * HEAD_DIM)
        heads.append((q_ref[0, :, sl], sl))

    def step(span, carry, masked):
        out = []
        for (qh, sl), (acc, run) in zip(heads, carry):
            contrib, run = span_tile(qh, sl, span, run, masked)
            out.append((acc + contrib, run))
        return tuple(out)

    init = tuple((jnp.zeros((SPAN, HEAD_DIM), F32), jnp.zeros((SPAN, 1), F32)) for _ in range(nh))
    carry = step(qs, init, True)

    def more(state):
        i, carry = state
        least = functools.reduce(jnp.minimum, [jnp.min(run) for _, run in carry])
        return (i < qs) & (least < SB_UNDERFLOW_LOG2)

    def body(state):
        i, carry = state
        return i + 1, step(qs - 1 - i, carry, False)

    _, carry = lax.while_loop(more, body, (jnp.int32(0), carry))
    for (_, sl), (acc, _) in zip(heads, carry):
        o_ref[0, :, sl] = acc.astype(o_ref.dtype)


def _stick_breaking(qkv, col0):
    bsz, s, _ = qkv.shape
    nhp = GROUP_WIDTH // LANES
    later = jnp.tril(jnp.ones((BLOCK, BLOCK), F32), -1).astype(BF16)
    return pl.pallas_call(
        _sb_kernel,
        grid=(bsz, nhp, s // SPAN),
        in_specs=[
            pl.BlockSpec((1, SPAN, LANES), lambda b, h, i: (b, i, col0 + h)),
            pl.BlockSpec((1, s, LANES), lambda b, h, i: (b, 0, col0 + nhp + h)),
            pl.BlockSpec((1, s, LANES), lambda b, h, i: (b, 0, col0 + 2 * nhp + h)),
            pl.BlockSpec((BLOCK, BLOCK), lambda b, h, i: (0, 0)),
        ],
        out_specs=pl.BlockSpec((1, SPAN, LANES), lambda b, h, i: (b, i, h)),
        out_shape=jax.ShapeDtypeStruct((bsz, s, GROUP_WIDTH), BF16),
        compiler_params=_params(("parallel", "parallel", "arbitrary")),
        name="stick_breaking",
    )(qkv, qkv, qkv, later)


def _diff_kernel(q_ref, k_ref, v_ref, vec_ref, lam_ref, subln_ref, o_ref, bias_scr, *, lam_init):
    qs = pl.program_id(2)
    causal = _iota2((SPAN, SPAN), 1) <= _iota2((SPAN, SPAN), 0)
    halves = [slice(c * DIFF_HALF, (c + 1) * DIFF_HALF) for c in range(2)]
    q = [q_ref[0, :, sl] for sl in halves]

    @pl.when((pl.program_id(1) == 0) & (qs == 0))
    def _():
        for dist in range(bias_scr.shape[0]):
            vec = jnp.broadcast_to(vec_ref[0, dist][0:1, :], (SPAN, 2 * SPAN))
            bias_scr[dist] = pltpu.roll(vec, 0, 1, stride=1, stride_axis=0)[:, :SPAN]

    def step(span, carry, masked):
        st = pl.multiple_of(span * SPAN, SPAN)
        bias = bias_scr[qs - span]
        v = v_ref[0, pl.ds(st, SPAN), :]
        out = []
        for c in range(2):
            m, l, acc = carry[c]
            s = lax.dot_general(q[c], k_ref[0, pl.ds(st, SPAN), halves[c]], (((1,), (1,)), ((), ())),
                                preferred_element_type=F32) + bias
            if masked:
                s = jnp.where(causal, s, NEG_BIG)
            m_new = jnp.maximum(m, jnp.max(s, axis=1, keepdims=True))
            alpha = jnp.exp2(m - m_new)
            p = jnp.exp2(s - m_new)
            l = alpha * l + jnp.sum(p, axis=1, keepdims=True)
            acc = alpha * acc + jnp.dot(_bf(p), v, preferred_element_type=F32)
            out.append((m_new, l, acc))
        return tuple(out)

    def init():
        return (jnp.full((SPAN, 1), NEG_BIG, F32), jnp.zeros((SPAN, 1), F32),
                jnp.zeros((SPAN, DIFF_VDIM), F32))

    carry = step(qs, (init(), init()), True)
    carry = lax.fori_loop(0, qs, lambda i, c: step(i, c, False), carry)

    lp = lam_ref[...]
    lam = (jnp.exp(jnp.sum(lp[0:1] * lp[1:2], axis=1, keepdims=True))
           - jnp.exp(jnp.sum(lp[2:3] * lp[3:4], axis=1, keepdims=True)) + lam_init)
    (_, l0, a0), (_, l1, a1) = carry
    o = a0 / l0 - lam * (a1 / l1)
    ms = jnp.mean(o * o, axis=-1, keepdims=True)
    o = o * lax.rsqrt(ms + SUBLN_EPS) * subln_ref[...] * (1.0 - lam_init)
    o_ref[0] = o.astype(o_ref.dtype)


def _diff_attention(qkv, col0, bias_vecs, lam_params, subln, lam_init):
    bsz, s, _ = qkv.shape
    ns = s // SPAN
    return pl.pallas_call(
        functools.partial(_diff_kernel, lam_init=lam_init),
        grid=(H_D, bsz, ns),
        in_specs=[
            pl.BlockSpec((1, SPAN, LANES), lambda h, b, i: (b, i, col0 + h)),
            pl.BlockSpec((1, s, LANES), lambda h, b, i: (b, 0, col0 + H_D + h)),
            pl.BlockSpec((1, s, LANES), lambda h, b, i: (b, 0, col0 + 2 * H_D + h)),
            pl.BlockSpec((1, ns, SUBLANES, 2 * SPAN), lambda h, b, i: (h, 0, 0, 0)),
            pl.BlockSpec((4, DIFF_HALF), lambda h, b, i: (0, 0)),
            pl.BlockSpec((1, DIFF_VDIM), lambda h, b, i: (0, 0)),
        ],
        out_specs=pl.BlockSpec((1, SPAN, LANES), lambda h, b, i: (b, i, h)),
        out_shape=jax.ShapeDtypeStruct((bsz, s, GROUP_WIDTH), BF16),
        scratch_shapes=[pltpu.VMEM((ns, SPAN, SPAN), F32)],
        compiler_params=_params(("arbitrary", "arbitrary", "arbitrary")),
        name="diff_attention",
    )(qkv, qkv, qkv, bias_vecs, lam_params, subln)


def _dil_kernel(q_ref, k_ref, v_ref, b1_ref, b4_ref, b16_ref, y_ref, acc_scr, m_scr, l_scr):
    s_len = q_ref.shape[1]
    first_head = _iota2((BLOCK, LANES), 1) < HEAD_DIM
    bias_refs = (b1_ref, b4_ref, b16_ref)

    def stack_heads(t):
        return jnp.concatenate([jnp.where(first_head, t, 0.0), jnp.where(first_head, 0.0, t)], axis=0)

    def unstack(t):
        return jnp.where(first_head, t[:BLOCK], t[BLOCK:])

    for pi, (window, dil) in enumerate(DILATIONS):
        n_back = window // dil
        nc = s_len // dil // BLOCK
        has_prev = nc > 1
        width = 2 * BLOCK if has_prev else BLOCK
        qi = _iota2((2 * BLOCK, width), 0) & (BLOCK - 1)
        ki = _iota2((2 * BLOCK, width), 1)
        steps = qi + (BLOCK if has_prev else 0) - ki
        band = (steps >= 0) & (steps <= n_back)
        bias = bias_refs[pi][0]
        if not has_prev:
            bias = bias[:, BLOCK:]

        def body(it, carry, dil=dil, nc=nc, has_prev=has_prev, band=band, bias=bias, pi=pi, ki=ki):
            loads = []
            for u in range(DIL_TILES_PER_STEP):
                idx = it * DIL_TILES_PER_STEP + u
                r, c = idx // nc, idx % nc
                rows = pl.ds(r + dil * BLOCK * c, BLOCK, stride=dil)
                q = stack_heads(q_ref[0, rows, :])
                k = k_ref[0, rows, :]
                v = v_ref[0, rows, :]
                if has_prev:
                    prows = pl.ds(r + dil * BLOCK * jnp.maximum(c - 1, 0), BLOCK, stride=dil)
                    k = jnp.concatenate([k_ref[0, prows, :], k], axis=0)
                    v = jnp.concatenate([v_ref[0, prows, :], v], axis=0)
                loads.append((rows, c, _bf(q), _bf(k), _bf(v)))
            ss = [lax.dot_general(q, k, (((1,), (1,)), ((), ())), preferred_element_type=F32) + bias
                  for _, _, q, k, _ in loads]
            if has_prev:
                ss = [jnp.where(band & (ki >= jnp.where(c > 0, 0, BLOCK)), s, NEG_BIG)
                      for s, (_, c, _, _, _) in zip(ss, loads)]
            else:
                ss = [jnp.where(band, s, NEG_BIG) for s in ss]
            ms = [jnp.max(s, axis=1, keepdims=True) for s in ss]
            ps = [jnp.exp2(s - m) for s, m in zip(ss, ms)]
            ls = [jnp.sum(p, axis=1, keepdims=True) for p in ps]
            os_ = [jnp.dot(_bf(p), v, preferred_element_type=F32) for p, (_, _, _, _, v) in zip(ps, loads)]
            for (rows, _, _, _, _), m, l, o in zip(loads, ms, ls, os_):
                acc_scr[pi, rows, :] = unstack(o)
                m_scr[pi, rows, :] = unstack(jnp.broadcast_to(m, (2 * BLOCK, LANES)))
                l_scr[pi, rows, :] = unstack(jnp.broadcast_to(l, (2 * BLOCK, LANES)))
            return carry

        lax.fori_loop(0, s_len // BLOCK // DIL_TILES_PER_STEP, body, 0)

    tm = 2 * BLOCK

    def merge(i, carry):
        rows = pl.ds(pl.multiple_of(i * tm, tm), tm)
        m1, m2, m3 = m_scr[0, rows, :], m_scr[1, rows, :], m_scr[2, rows, :]
        m = jnp.maximum(jnp.maximum(m1, m2), m3)
        e1, e2, e3 = jnp.exp2(m1 - m), jnp.exp2(m2 - m), jnp.exp2(m3 - m)
        num = e1 * acc_scr[0, rows, :] + e2 * acc_scr[1, rows, :] + e3 * acc_scr[2, rows, :]
        den = e1 * l_scr[0, rows, :] + e2 * l_scr[1, rows, :] + e3 * l_scr[2, rows, :]
        y_ref[0, rows, :] = (num / den).astype(y_ref.dtype)
        return carry

    lax.fori_loop(0, s_len // tm, merge, 0)


def _dilated_attention(qkv_c, biases):
    bsz, s, _ = qkv_c.shape
    nhp = GROUP_WIDTH // LANES
    bspec = pl.BlockSpec((1, 2 * BLOCK, 2 * BLOCK), lambda b, h: (h, 0, 0))
    scr = pltpu.VMEM((len(DILATIONS), s, LANES), F32)
    return pl.pallas_call(
        _dil_kernel,
        grid=(bsz, nhp),
        in_specs=[
            pl.BlockSpec((1, s, LANES), lambda b, h: (b, 0, h)),
            pl.BlockSpec((1, s, LANES), lambda b, h: (b, 0, nhp + h)),
            pl.BlockSpec((1, s, LANES), lambda b, h: (b, 0, 2 * nhp + h)),
            bspec, bspec, bspec,
        ],
        out_specs=pl.BlockSpec((1, s, LANES), lambda b, h: (b, 0, h)),
        out_shape=jax.ShapeDtypeStruct((bsz, s, GROUP_WIDTH), BF16),
        scratch_shapes=[scr, scr, scr],
        compiler_params=_params(("parallel", "parallel")),
        name="dilated_attention",
    )(qkv_c, qkv_c, qkv_c, *biases)


def _unit_lower_inverses(ns):
    size_all = 2 * CHUNK
    row = _iota2((size_all, size_all), 0)
    col = _iota2((size_all, size_all), 1)
    eye = jnp.where(row == col, 1.0, 0.0)
    base = 8
    same8 = (row >> 3) == (col >> 3)
    n8 = [jnp.where(same8, n, 0.0) for n in ns]
    n2 = [_mm(a, a) for a in n8]
    n4 = [_mm(a, a) for a in n2]
    ts = [eye + a for a in n8]
    ts = [t + _mm(t, b) for t, b in zip(ts, n2)]
    ts = [t + _mm(t, b) for t, b in zip(ts, n4)]
    size = 2 * base
    while size <= CHUNK:
        half = size // 2
        shift = size.bit_length() - 1
        off = (((row >> shift) == (col >> shift)) & ((row & (size - 1)) >= half)
               & ((col & (size - 1)) < half))
        us = [_mm(t, jnp.where(off, n, 0.0)) for t, n in zip(ts, ns)]
        ts = [t + _mm(u, t) for t, u in zip(ts, us)]
        size *= 2
    return ts


def _rwkv_kernel(*refs, has_vres):
    if has_vres:
        (pa_ref, mu_ref, w0_ref, wup_ref, a0_ref, aup_ref, gup_ref, kk_ref, ka_ref, rk_ref,
         lnw_ref, lnb_ref, seg_ref, cum_ref, vf_ref, vup_ref, vb_ref,
         y_ref, prev_scr, s_scr) = refs
    else:
        (pa_ref, mu_ref, w0_ref, wup_ref, a0_ref, aup_ref, gup_ref, kk_ref, ka_ref, rk_ref,
         lnw_ref, lnb_ref, seg_ref, cum_ref,
         y_ref, vf_out_ref, prev_scr, s_scr) = refs
    c = GROUP_WIDTH
    tt = pa_ref.shape[1]

    @pl.when(pl.program_id(1) == 0)
    def _():
        prev_scr[...] = jnp.zeros_like(prev_scr)
        s_scr[...] = jnp.zeros_like(s_scr)

    seg = seg_ref[...]

    def seg_sum(t):
        return jnp.concatenate([jnp.dot(_bf(t[:, p * LANES:(p + 1) * LANES]), seg, preferred_element_type=F32)
                                for p in range(c // LANES)], axis=1)

    pa = pa_ref[0]
    rolled = pltpu.roll(pa, 1, 0)
    first = _iota2(pa.shape, 0) == 0
    prev = jnp.where(first, prev_scr[...], rolled)
    prev_scr[...] = pa[tt - 1:tt, :]
    sh = pa + (prev - pa) * mu_ref[...]
    r = sh[:, 0:c]
    k = sh[:, c:2 * c]
    v = sh[:, 2 * c:3 * c]
    w_lo = sh[:, A_OFF_W:A_OFF_W + LORA_PAD]
    a_lo = sh[:, A_OFF_A:A_OFF_A + LORA_PAD]
    g_lo = sh[:, A_OFF_G:A_OFF_G + G_PAD]
    if has_vres:
        pv = sh[:, A_OFF_V:A_OFF_V + LORA_PAD]
        v = v + (vf_ref[0] - v) * _sigmoid(vb_ref[...] + _mm(pv, vup_ref[...]))
    else:
        vf_out_ref[0] = v

    w_log = -_softplus(-(w0_ref[...] + _mm(jnp.tanh(w_lo), wup_ref[...]))) - 0.5
    lw = -jnp.exp(w_log)
    a = _sigmoid(a0_ref[...] + _mm(a_lo, aup_ref[...]))
    g = _mm(_sigmoid(g_lo), gup_ref[...])
    kkr = k * kk_ref[...]
    kk = kkr * lax.rsqrt(jnp.maximum(seg_sum(kkr * kkr), 1e-24))
    k = k * (1.0 + (a - 1.0) * ka_ref[...])

    cum = _mm_split_rhs(cum_ref[...], lw, 2)
    eg = jnp.exp(cum)
    eig = jnp.exp(-cum)
    at = -kk * jnp.exp(cum - lw)
    rt = r * eg
    bt = kk * a * eig
    kt = k * eig

    pair_rows = 2 * CHUNK
    first_head = _iota2((CHUNK, LANES), 1) < HEAD_DIM
    row = _iota2((pair_rows, pair_rows), 0)
    col = _iota2((pair_rows, pair_rows), 1)
    strict = (row & (CHUNK - 1)) > (col & (CHUNK - 1))
    lower = (row & (CHUNK - 1)) >= (col & (CHUNK - 1))
    eye = row == col

    def stack(t):
        return jnp.concatenate([jnp.where(first_head, t, 0.0), jnp.where(first_head, 0.0, t)], axis=0)

    probs = [(ci, p) for ci in range(tt // CHUNK) for p in range(c // LANES)]
    cut = lambda t, ci, p: t[ci * CHUNK:(ci + 1) * CHUNK, p * LANES:(p + 1) * LANES]
    g_end = [cut(eg, ci, p)[CHUNK - 1:CHUNK, :] for ci, p in probs]
    at_p = [stack(cut(at, ci, p)) for ci, p in probs]
    rt_p = [stack(cut(rt, ci, p)) for ci, p in probs]
    bt_p = [stack(cut(bt, ci, p)) for ci, p in probs]
    kt_p = [stack(cut(kt, ci, p)) for ci, p in probs]
    v_p = [stack(cut(v, ci, p)) for ci, p in probs]
    bg_p = [b_ * ge for b_, ge in zip(bt_p, g_end)]
    kg_p = [k_ * ge for k_, ge in zip(kt_p, g_end)]
    grams = [_mm_nt(jnp.concatenate([a_, r_], axis=0), jnp.concatenate([b_, k_], axis=0))
             for a_, r_, b_, k_ in zip(at_p, rt_p, bt_p, kt_p)]
    a_ab = [jnp.where(strict, gm[:pair_rows, :pair_rows], 0.0) for gm in grams]
    a_ak = [jnp.where(strict, gm[:pair_rows, pair_rows:], 0.0) for gm in grams]
    a_rb = [jnp.where(lower, gm[pair_rows:, :pair_rows], 0.0) for gm in grams]
    a_rk = [jnp.where(lower, gm[pair_rows:, pair_rows:], 0.0) for gm in grams]
    both_v = [_mm(jnp.concatenate([a_, r_], axis=0), v_) for a_, r_, v_ in zip(a_ak, a_rk, v_p)]
    akv = [x_[:pair_rows] for x_ in both_v]
    rkv = [x_[pair_rows:] for x_ in both_v]
    ts = _unit_lower_inverses(a_ab)
    pq = [_mm(t, jnp.concatenate([a_, x_], axis=1)) for t, a_, x_ in zip(ts, at_p, akv)]
    ps = [x_[:, :LANES] for x_ in pq]
    qs = [x_[:, LANES:] for x_ in pq]
    rb_pq = [_mm(a_, x_) for a_, x_ in zip(a_rb, pq)]
    r2 = [r_ + x_[:, :LANES] for r_, x_ in zip(rt_p, rb_pq)]
    y0 = [x_[:, LANES:] + k_ for x_, k_ in zip(rb_pq, rkv)]
    ms = [jnp.where(eye, jnp.broadcast_to(ge, (pair_rows, LANES)), 0.0) + _mm_tn(p_, b_)
          for ge, p_, b_ in zip(g_end, ps, bg_p)]
    zs = [_mm_tn(jnp.concatenate([q_, v_], axis=0), jnp.concatenate([b_, k_], axis=0))
          for q_, b_, v_, k_ in zip(qs, bg_p, v_p, kg_p)]

    state = [s_scr[p] for p in range(c // LANES)]
    y_rows = []
    for ci in range(tt // CHUNK):
        y_pairs = []
        for p in range(c // LANES):
            i = ci * (c // LANES) + p
            y_st = _mm_nt(r2[i], state[p]) + y0[i]
            y_pairs.append(y_st[:CHUNK] + y_st[CHUNK:])
            state[p] = _mm(state[p], ms[i]) + zs[i]
        y_rows.append(jnp.concatenate(y_pairs, axis=1))
    for p in range(c // LANES):
        s_scr[p] = state[p]
    y = jnp.concatenate(y_rows, axis=0)

    mu = seg_sum(y) * (1.0 / HEAD_DIM)
    d = y - mu
    var = seg_sum(d * d) * (1.0 / HEAD_DIM)
    yn = d * lax.rsqrt(var + RWKV_LN_EPS) * lnw_ref[...] + lnb_ref[...]
    bonus = seg_sum(r * k * rk_ref[...]) * v
    y_ref[0] = ((yn + bonus) * g).astype(y_ref.dtype)


def _rwkv7(pa, prm, v_first):
    bsz, s, _ = pa.shape
    na = prm["mu"].shape[1]
    c = GROUP_WIDTH
    tt = RWKV_TILE
    has_vres = v_first is not None
    full = lambda shape: pl.BlockSpec(shape, lambda b, t: (0,) * len(shape))
    tile = lambda w: pl.BlockSpec((1, tt, w), lambda b, t: (b, t, 0))
    idx = jnp.arange(tt)
    cum_ones = ((idx[:, None] >= idx[None, :]) & (idx[:, None] // CHUNK == idx[None, :] // CHUNK)).astype(BF16)
    in_specs = [tile(na), full((1, na)), full((1, c)), full((LORA_PAD, c)), full((1, c)),
                full((LORA_PAD, c)), full((G_PAD, c)), full((1, c)), full((1, c)), full((1, c)),
                full((1, c)), full((1, c)), full((LANES, LANES)), full((tt, tt))]
    args = [pa, prm["mu"], prm["w0"], prm["w_up"], prm["a0"], prm["a_up"], prm["g_up"],
            prm["k_k"], prm["k_a"], prm["r_k"], prm["ln_w"], prm["ln_b"], prm["seg"], cum_ones]
    y_shape = jax.ShapeDtypeStruct((bsz, s, c), BF16)
    if has_vres:
        in_specs += [tile(c), full((LORA_PAD, c)), full((1, c))]
        args += [v_first, prm["vres_up"], prm["vres_bias"]]
        out_specs, out_shape = tile(c), y_shape
    else:
        out_specs = [tile(c), tile(c)]
        out_shape = [y_shape, jax.ShapeDtypeStruct((bsz, s, c), F32)]
    res = pl.pallas_call(
        functools.partial(_rwkv_kernel, has_vres=has_vres),
        grid=(bsz, s // tt),
        in_specs=in_specs,
        out_specs=out_specs,
        out_shape=out_shape,
        scratch_shapes=[pltpu.VMEM((1, na), F32), pltpu.VMEM((c // LANES, LANES, LANES), F32)],
        compiler_params=_params(("parallel", "arbitrary")),
        name="rwkv7_vres" if has_vres else "rwkv7",
    )(*args)
    if has_vres:
        return res, v_first
    return res[0], res[1]


def _rel_bucket(dist):
    max_exact = NUM_BUCKETS // 2
    d_f = jnp.maximum(dist, 1).astype(F32)
    large = max_exact + (jnp.log(d_f / max_exact) / math.log(MAX_DISTANCE / max_exact)
                         * (NUM_BUCKETS - max_exact)).astype(jnp.int32)
    large = jnp.minimum(large, NUM_BUCKETS - 1)
    return jnp.where(dist < max_exact, dist, large)


def _toeplitz(u, n_rows, n_cols):
    length = n_rows + n_cols - 1
    g = jnp.roll(jnp.flip(u, axis=-1), -(n_rows - 1), axis=-1)
    flat = jnp.tile(g, (1,) * (u.ndim - 1) + (n_rows,))[..., :n_rows * (length - 1)]
    return flat.reshape(u.shape[:-1] + (n_rows, length - 1))[..., :n_cols]


def _diff_bias_vectors(rel_tab, ns):
    n = jnp.arange(2 * SPAN)
    rel = jnp.where(n < SPAN, -n, 2 * SPAN - n)
    dist = jnp.arange(ns)[:, None] * SPAN + rel[None, :]
    u = rel_tab[_rel_bucket(jnp.maximum(dist, 0))] * LOG2E
    u = jnp.transpose(u, (2, 0, 1)).astype(F32)
    return jnp.broadcast_to(u[:, :, None, :], (H_D, ns, SUBLANES, 2 * SPAN))


def _band_bias(rel_tab, dil):
    steps = jnp.arange(3 * BLOCK - 1) - (BLOCK - 1)
    u = rel_tab[_rel_bucket(jnp.maximum(steps, 0) * dil)] * LOG2E
    t = _toeplitz(jnp.transpose(u, (1, 0)).astype(F32), BLOCK, 2 * BLOCK)
    return t.reshape(N_HEADS * HEAD_DIM // LANES, 2 * BLOCK, 2 * BLOCK)


def _pad_axis(t, size, axis):
    pad = [(0, 0)] * t.ndim
    pad[axis] = (0, size - t.shape[axis])
    return jnp.pad(t, pad)


def _layout_mixer_a(t, vres=None):
    c = GROUP_WIDTH
    o_w, o_a, o_g = 3 * c, 3 * c + W_LORA, 3 * c + W_LORA + A_LORA
    parts = [t[..., :o_w], _pad_axis(t[..., o_w:o_a], LORA_PAD, -1),
             _pad_axis(t[..., o_a:o_g], LORA_PAD, -1), _pad_axis(t[..., o_g:N_A_COLS], G_PAD, -1)]
    if vres is not None:
        parts.append(_pad_axis(vres, PROJ_TN, -1))
    return jnp.concatenate(parts, axis=-1)


def _scale_queries(w_qkv, scale):
    return jnp.concatenate([w_qkv[:, :GROUP_WIDTH] * scale, w_qkv[:, GROUP_WIDTH:]], axis=1)


def kernel(x, c, w_ada, b_ada, norm_gain, w_in, w_out, rel_bias, rwkv_mu, rwkv_w0, rwkv_w_up,
           rwkv_a0, rwkv_a_up, rwkv_g_up, rwkv_k_k, rwkv_k_a, rwkv_r_k, rwkv_ln_w, rwkv_ln_b,
           vres_down, vres_mu, vres_up, vres_bias, diff_lambda, diff_subln, ffn_w13, ffn_w2,
           final_gain):
    out_dtype = x.dtype
    bsz, s, d = x.shape
    cw = GROUP_WIDTH
    x = x.astype(F32)
    row = lambda t: t.reshape(1, -1).astype(F32)

    mod = _modulation(c.astype(F32), w_ada, b_ada)
    rel_tab_c = rel_bias[:, :N_HEADS].astype(F32)
    rel_tab_d = rel_bias[:, N_HEADS:].astype(F32)
    band_biases = [_band_bias(rel_tab_c, dil) for _, dil in DILATIONS]
    diff_bias = _diff_bias_vectors(rel_tab_d, s // SPAN)
    lane = jnp.arange(LANES)
    seg = (lane[:, None] // HEAD_DIM == lane[None, :] // HEAD_DIM).astype(BF16)
    logit_scale = LOG2E * HEAD_DIM ** -0.5

    v_first = None
    for l in range(DEPTH):
        sh1, sc1, g1, sh2, sc2, g2 = [mod[l, :, None, i * d:(i + 1) * d] for i in range(6)]
        gain1, gain2 = row(norm_gain[l, 0]), row(norm_gain[l, 1])
        vres_w = vres_down[l - 1] if l > 0 else None
        w_l = w_in[l].astype(F32)
        w_a = _layout_mixer_a(w_l[:, :N_A_COLS], vres_w)
        o_b, o_c, o_d = N_A_COLS, N_A_COLS + 3 * cw, N_A_COLS + 6 * cw
        w_all = _bf(jnp.concatenate([w_a, _scale_queries(w_l[:, o_b:o_c], logit_scale),
                                     _scale_queries(w_l[:, o_d:N_IN], logit_scale),
                                     _scale_queries(w_l[:, o_c:o_d], logit_scale)], axis=1))
        na_used = A_OFF_V + (LORA_PAD if l > 0 else 0)
        mu = _layout_mixer_a(rwkv_mu[l], vres_mu[l - 1] if l > 0 else None)[..., :na_used]

        pa, qkv_bd, qkv_c = _norm_proj(x, gain1, sc1, sh1, w_all, (w_a.shape[1], 6 * cw, 3 * cw),
                                       (F32, BF16, F32))

        prm = {
            "mu": row(mu), "w0": row(rwkv_w0[l]), "a0": row(rwkv_a0[l]),
            "w_up": _bf(_pad_axis(rwkv_w_up[l], LORA_PAD, 0)),
            "a_up": _bf(_pad_axis(rwkv_a_up[l], LORA_PAD, 0)),
            "g_up": _bf(_pad_axis(rwkv_g_up[l], G_PAD, 0)),
            "k_k": row(rwkv_k_k[l]), "k_a": row(rwkv_k_a[l]), "r_k": row(rwkv_r_k[l]),
            "ln_w": row(rwkv_ln_w[l]), "ln_b": row(rwkv_ln_b[l]), "seg": seg,
        }
        if l > 0:
            prm["vres_up"] = _bf(_pad_axis(vres_up[l - 1], LORA_PAD, 0))
            prm["vres_bias"] = row(vres_bias[l - 1])
        y_a, v_first = _rwkv7(pa, prm, v_first)

        y_b = _stick_breaking(qkv_bd, 0)
        y_c = _dilated_attention(qkv_c, band_biases)
        lam_init = 0.8 - 0.6 * math.exp(-0.3 * l)
        y_d = _diff_attention(qkv_bd, 3 * cw // LANES, diff_bias, diff_lambda[l].astype(F32),
                              row(diff_subln[l]), lam_init)

        x = _out_proj(y_a, y_b, y_c, y_d, _bf(w_out[l]), x, g1)
        x = _ffn(x, gain2, sc2, sh2, g2, _bf(ffn_w13[l]), _bf(ffn_w2[l]), row(final_gain),
                 final=(l == DEPTH - 1))
    return x.astype(out_dtype)
```

```python
import functools
import math

import jax
import jax.numpy as jnp
from jax import lax
from jax.experimental import pallas as pl
from jax.experimental.pallas import tpu as pltpu

F32 = jnp.float32
BF16 = jnp.bfloat16

D_MODEL = 2048
DEPTH = 2
HEAD_DIM = 64
GROUP_WIDTH = D_MODEL // 4
N_HEADS = GROUP_WIDTH // HEAD_DIM
DIFF_HALF = HEAD_DIM
DIFF_VDIM = 2 * DIFF_HALF
H_D = GROUP_WIDTH // DIFF_VDIM
W_LORA = max(32, int(round(D_MODEL ** 0.5 * 1.8 / 32)) * 32)
A_LORA = max(32, int(round(D_MODEL ** 0.5 * 1.8 / 32)) * 32)
V_LORA = max(32, int(round(D_MODEL ** 0.5 * 1.3 / 32)) * 32)
G_LORA = max(32, int(round(D_MODEL ** 0.8 / 32)) * 32)
N_A_COLS = 3 * GROUP_WIDTH + W_LORA + A_LORA + G_LORA
N_IN = N_A_COLS + 9 * GROUP_WIDTH
D_FF = ((8 * D_MODEL + 3 * 256 - 1) // (3 * 256)) * 256
DILATIONS = ((128, 1), (512, 4), (2048, 16))
BLOCK = 128
NUM_BUCKETS = 32
MAX_DISTANCE = 2048
NORM_EPS = 1e-6
RWKV_LN_EPS = 64e-5
SUBLN_EPS = 1e-5
LOG2E = math.log2(math.e)

LANES = 128
SUBLANES = 8
VMEM_LIMIT_BYTES = 56 * 1024 * 1024

LORA_PAD = 128
G_PAD = 512
A_OFF_W = 3 * GROUP_WIDTH
A_OFF_A = A_OFF_W + LORA_PAD
A_OFF_G = A_OFF_A + LORA_PAD
A_OFF_V = A_OFF_G + G_PAD
PROJ_TN = 768
FIRST_STEP_ROWS = 256
CHUNK = 64
RWKV_TILE = 256
SPAN_BLOCKS = 4
SPAN = SPAN_BLOCKS * BLOCK
DIL_TILES_PER_STEP = 4
NEG_BIG = -1e30
SB_ROW_GROUPS = 2
SB_UNDERFLOW_LOG2 = 160.0


def _bf(x):
    return x.astype(BF16)


def _mm(a, b):
    return jnp.dot(_bf(a), _bf(b), preferred_element_type=F32)


def _mm_nt(a, b):
    return lax.dot_general(_bf(a), _bf(b), (((1,), (1,)), ((), ())), preferred_element_type=F32)


def _mm_tn(a, b):
    return lax.dot_general(_bf(a), _bf(b), (((0,), (0,)), ((), ())), preferred_element_type=F32)


def _split_bf16(x, n):
    parts = []
    rem = x
    for i in range(n):
        p = _bf(rem)
        parts.append(p)
        if i + 1 < n:
            rem = rem - p.astype(F32)
    return parts


def _mm_split_lhs(a, b_bf16, n):
    out = None
    for p in _split_bf16(a, n):
        t = jnp.dot(p, b_bf16, preferred_element_type=F32)
        out = t if out is None else out + t
    return out


def _mm_split_rhs(a_bf16, b, n):
    out = None
    for p in _split_bf16(b, n):
        t = jnp.dot(a_bf16, p, preferred_element_type=F32)
        out = t if out is None else out + t
    return out


def _softplus(x):
    return jnp.maximum(x, 0.0) + jnp.log1p(jnp.exp(-jnp.abs(x)))


def _softplus2(x):
    sign_bit = jnp.uint32(0x80000000)
    neg_abs = lax.bitcast_convert_type(lax.bitcast_convert_type(x, jnp.uint32) | sign_bit, F32)
    return jnp.maximum(x, 0.0) + jnp.log2(1.0 + jnp.exp2(neg_abs))


def _sigmoid(x):
    return jax.nn.sigmoid(x)


def _iota2(shape, dim):
    return lax.broadcasted_iota(jnp.int32, shape, dim)


def _params(sem, vmem=VMEM_LIMIT_BYTES):
    return pltpu.CompilerParams(dimension_semantics=sem, vmem_limit_bytes=vmem)


def _mod_kernel(c_ref, w_ref, b_ref, o_ref):
    c = c_ref[...]
    cond = c * _sigmoid(c)
    o_ref[0] = _mm(cond, w_ref[0]) + b_ref[0]


def _modulation(c, w_ada, b_ada):
    depth, d, n = w_ada.shape
    bsz = c.shape[0]
    tn = 1024
    return pl.pallas_call(
        _mod_kernel,
        grid=(depth, n // tn),
        in_specs=[
            pl.BlockSpec((bsz, d), lambda l, j: (0, 0)),
            pl.BlockSpec((1, d, tn), lambda l, j: (l, 0, j)),
            pl.BlockSpec((1, 1, tn), lambda l, j: (l, 0, j)),
        ],
        out_specs=pl.BlockSpec((1, bsz, tn), lambda l, j: (l, 0, j)),
        out_shape=jax.ShapeDtypeStruct((depth, bsz, n), F32),
        compiler_params=_params(("parallel", "parallel")),
        name="adaln_mod",
    )(c, w_ada, b_ada.reshape(depth, 1, n))


def _norm_mod(x, gain, sc, sh):
    ms = jnp.mean(x * x, axis=-1, keepdims=True)
    return x * lax.rsqrt(ms + NORM_EPS) * gain * (1.0 + sc) + sh


def _proj_kernel(x_ref, gain_ref, sc_ref, sh_ref, w_ref, *rest, splits):
    o_refs, h_scr = rest[:-1], rest[-1]
    j = pl.program_id(2)
    tm = h_scr.shape[0]

    @pl.when(j == 0)
    def _():
        for g in range(tm // FIRST_STEP_ROWS):
            rows = slice(g * FIRST_STEP_ROWS, (g + 1) * FIRST_STEP_ROWS)
            h = _bf(_norm_mod(x_ref[0, rows, :], gain_ref[...], sc_ref[0], sh_ref[0]))
            h_scr[rows, :] = h
            o_refs[0][0, rows, :] = jnp.dot(h, w_ref[...], preferred_element_type=F32).astype(o_refs[0].dtype)

    lo = 0
    for o_ref, n_tiles in zip(o_refs, splits):
        @pl.when((j >= max(lo, 1)) & (j < lo + n_tiles))
        def _(o_ref=o_ref):
            o_ref[0] = jnp.dot(h_scr[...], w_ref[...], preferred_element_type=F32).astype(o_ref.dtype)
        lo += n_tiles


def _norm_proj(x, gain, sc, sh, w, widths, dtypes):
    bsz, s, d = x.shape
    tm, tn = 1024, PROJ_TN
    splits = tuple(n // tn for n in widths)
    starts = [sum(splits[:k]) for k in range(len(splits))]

    def out_spec(k):
        return pl.BlockSpec((1, tm, tn),
                            lambda b, i, j: (b, i, jnp.clip(j - starts[k], 0, splits[k] - 1)))

    return pl.pallas_call(
        functools.partial(_proj_kernel, splits=splits),
        grid=(bsz, s // tm, sum(splits)),
        in_specs=[
            pl.BlockSpec((1, tm, d), lambda b, i, j: (b, i, 0)),
            pl.BlockSpec((1, d), lambda b, i, j: (0, 0)),
            pl.BlockSpec((1, 1, d), lambda b, i, j: (b, 0, 0)),
            pl.BlockSpec((1, 1, d), lambda b, i, j: (b, 0, 0)),
            pl.BlockSpec((d, tn), lambda b, i, j: (0, j)),
        ],
        out_specs=[out_spec(k) for k in range(len(widths))],
        out_shape=[jax.ShapeDtypeStruct((bsz, s, n), dt) for n, dt in zip(widths, dtypes)],
        scratch_shapes=[pltpu.VMEM((tm, d), BF16)],
        compiler_params=_params(("parallel", "parallel", "arbitrary")),
        name="in_proj",
    )(x, gain, sc, sh, w)


def _outproj_kernel(ya_ref, yb_ref, yc_ref, yd_ref, w_ref, x_ref, g_ref, o_ref):
    c = GROUP_WIDTH
    acc = jnp.dot(ya_ref[0], w_ref[0:c, :], preferred_element_type=F32)
    acc += jnp.dot(yb_ref[0], w_ref[c:2 * c, :], preferred_element_type=F32)
    acc += jnp.dot(yc_ref[0], w_ref[2 * c:3 * c, :], preferred_element_type=F32)
    acc += jnp.dot(yd_ref[0], w_ref[3 * c:4 * c, :], preferred_element_type=F32)
    o_ref[0] = x_ref[0] + g_ref[0] * acc


def _out_proj(ya, yb, yc, yd, w, x, g1):
    bsz, s, d = x.shape
    c = GROUP_WIDTH
    tm, tn = 512, d
    yspec = pl.BlockSpec((1, tm, c), lambda b, i, j: (b, i, 0))
    return pl.pallas_call(
        _outproj_kernel,
        grid=(bsz, s // tm, d // tn),
        in_specs=[
            yspec, yspec, yspec, yspec,
            pl.BlockSpec((4 * c, tn), lambda b, i, j: (0, j)),
            pl.BlockSpec((1, tm, tn), lambda b, i, j: (b, i, j)),
            pl.BlockSpec((1, 1, tn), lambda b, i, j: (b, 0, j)),
        ],
        out_specs=pl.BlockSpec((1, tm, tn), lambda b, i, j: (b, i, j)),
        out_shape=jax.ShapeDtypeStruct((bsz, s, d), F32),
        compiler_params=_params(("parallel", "parallel", "arbitrary")),
        name="out_proj",
    )(ya, yb, yc, yd, w, x, g1)


def _ffn_kernel(x_ref, gain_ref, sc_ref, sh_ref, g_ref, w1_ref, w3_ref, w2_ref, fg_ref,
                o_ref, h_scr, *, final):
    f = pl.program_id(2)
    tm = h_scr.shape[0]
    acc_scr = o_ref.at[0]

    def chunk(h):
        gate = jnp.dot(h, w1_ref[...], preferred_element_type=F32)
        up = jnp.dot(h, w3_ref[...], preferred_element_type=F32)
        act = gate * _sigmoid(gate) * up
        return jnp.dot(_bf(act), w2_ref[...], preferred_element_type=F32)

    @pl.when(f == 0)
    def _():
        for g in range(tm // FIRST_STEP_ROWS):
            rows = slice(g * FIRST_STEP_ROWS, (g + 1) * FIRST_STEP_ROWS)
            h = _bf(_norm_mod(x_ref[0, rows, :], gain_ref[...], sc_ref[0], sh_ref[0]))
            h_scr[rows, :] = h
            acc_scr[rows, :] = chunk(h)

    @pl.when(f > 0)
    def _():
        acc_scr[...] += chunk(h_scr[...])

    @pl.when(f == pl.num_programs(2) - 1)
    def _():
        xn = x_ref[0] + g_ref[0] * acc_scr[...]
        if final:
            ms = jnp.mean(xn * xn, axis=-1, keepdims=True)
            xn = xn * lax.rsqrt(ms + NORM_EPS) * fg_ref[...]
        o_ref[0] = xn


def _ffn(x, gain, sc, sh, g2, w13, w2, final_gain, final):
    bsz, s, d = x.shape
    dff = w2.shape[0]
    tm, tf = 512, 512
    nf = dff // tf
    vec = pl.BlockSpec((1, 1, d), lambda b, i, f: (b, 0, 0))
    return pl.pallas_call(
        functools.partial(_ffn_kernel, final=final),
        grid=(bsz, s // tm, nf),
        in_specs=[
            pl.BlockSpec((1, tm, d), lambda b, i, f: (b, i, 0)),
            pl.BlockSpec((1, d), lambda b, i, f: (0, 0)),
            vec, vec, vec,
            pl.BlockSpec((d, tf), lambda b, i, f: (0, f)),
            pl.BlockSpec((d, tf), lambda b, i, f: (0, f + nf)),
            pl.BlockSpec((tf, d), lambda b, i, f: (f, 0)),
            pl.BlockSpec((1, d), lambda b, i, f: (0, 0)),
        ],
        out_specs=pl.BlockSpec((1, tm, d), lambda b, i, f: (b, i, 0)),
        out_shape=jax.ShapeDtypeStruct((bsz, s, d), F32),
        scratch_shapes=[pltpu.VMEM((tm, d), BF16)],
        compiler_params=_params(("parallel", "parallel", "arbitrary")),
        name="ffn_final" if final else "ffn",
    )(x, gain, sc, sh, g2, w13, w13, w2, final_gain)


def _sb_kernel(q_ref, k_ref, v_ref, later_ref, o_ref):
    qs = pl.program_id(2)
    nh = LANES // HEAD_DIM
    later = later_ref[...]
    row = _iota2((SPAN, BLOCK), 0)
    col = _iota2((SPAN, BLOCK), 1)

    def span_tile(q, sl, span, run, masked):
        st = pl.multiple_of(span * SPAN, SPAN)
        n_rows = q.shape[0]
        z = lax.dot_general(q, k_ref[0, pl.ds(st, SPAN), sl], (((1,), (1,)), ((), ())),
                            preferred_element_type=F32)
        subs = []
        for j in range(SPAN_BLOCKS):
            zj = z[:, j * BLOCK:(j + 1) * BLOCK]
            cost = _softplus2(zj)
            mask = None
            if masked:
                mask = (j * BLOCK + col) < row
                cost = jnp.where(mask, cost, 0.0)
            subs.append((zj, cost, mask))
        stacked = jnp.concatenate([_bf(cost) for _, cost, _ in subs], axis=0)
        within = jnp.dot(stacked, later, preferred_element_type=F32)
        atts = [None] * SPAN_BLOCKS
        for j in reversed(range(SPAN_BLOCKS)):
            zj, cost, mask = subs[j]
            w = within[j * n_rows:(j + 1) * n_rows]
            att = jnp.exp2(zj - cost - w - run)
            if masked:
                att = jnp.where(mask, att, 0.0)
            atts[j] = _bf(att)
            run = run + jnp.sum(cost, axis=1, keepdims=True)
        contrib = jnp.dot(jnp.concatenate(atts, axis=1), v_ref[0, pl.ds(st, SPAN), sl],
                          preferred_element_type=F32)
        return contrib, run

    lanes = [slice(h * HEAD_DIM, (h + 1) * HEAD_DIM) for h in range(nh)]

    def step(span, rows, carry, masked):
        out = []
        for sl, (acc, run) in zip(lanes, carry):
            contrib, run = span_tile(q_ref[0, rows, sl], sl, span, run, masked)
            out.append((acc + contrib, run))
        return tuple(out)

    init = tuple((jnp.zeros((SPAN, HEAD_DIM), F32), jnp.zeros((SPAN, 1), F32)) for _ in range(nh))
    diag = step(qs, slice(None), init, True)

    group = SPAN // SB_ROW_GROUPS
    outs = [[] for _ in range(nh)]
    for g in range(SB_ROW_GROUPS):
        rows = slice(g * group, (g + 1) * group)

        def more(state):
            i, carry = state
            least = functools.reduce(jnp.minimum, [jnp.min(run) for _, run in carry])
            return (i < qs) & (least < SB_UNDERFLOW_LOG2)

        def body(state, rows=rows):
            i, carry = state
            return i + 1, step(qs - 1 - i, rows, carry, False)

        start = tuple((acc[rows], run[rows]) for acc, run in diag)
        _, carry = lax.while_loop(more, body, (jnp.int32(0), start))
        for h, (acc, _) in enumerate(carry):
            outs[h].append(acc)
    for sl, parts in zip(lanes, outs):
        o_ref[0, :, sl] = jnp.concatenate(parts, axis=0).astype(o_ref.dtype)


def _stick_breaking(qkv, col0):
    bsz, s, _ = qkv.shape
    nhp = GROUP_WIDTH // LANES
    later = jnp.tril(jnp.ones((BLOCK, BLOCK), F32), -1).astype(BF16)
    return pl.pallas_call(
        _sb_kernel,
        grid=(bsz, nhp, s // SPAN),
        in_specs=[
            pl.BlockSpec((1, SPAN, LANES), lambda b, h, i: (b, i, col0 + h)),
            pl.BlockSpec((1, s, LANES), lambda b, h, i: (b, 0, col0 + nhp + h)),
            pl.BlockSpec((1, s, LANES), lambda b, h, i: (b, 0, col0 + 2 * nhp + h)),
            pl.BlockSpec((BLOCK, BLOCK), lambda b, h, i: (0, 0)),
        ],
        out_specs=pl.BlockSpec((1, SPAN, LANES), lambda b, h, i: (b, i, h)),
        out_shape=jax.ShapeDtypeStruct((bsz, s, GROUP_WIDTH), BF16),
        compiler_params=_params(("parallel", "parallel", "arbitrary")),
        name="stick_breaking",
    )(qkv, qkv, qkv, later)


def _diff_kernel(q_ref, k_ref, v_ref, vec_ref, lam_ref, subln_ref, o_ref, bias_scr, *, lam_init):
    qs = pl.program_id(2)
    causal = _iota2((SPAN, SPAN), 1) <= _iota2((SPAN, SPAN), 0)
    halves = [slice(c * DIFF_HALF, (c + 1) * DIFF_HALF) for c in range(2)]
    q = [q_ref[0, :, sl] for sl in halves]

    @pl.when((pl.program_id(1) == 0) & (qs == 0))
    def _():
        for dist in range(bias_scr.shape[0]):
            vec = jnp.broadcast_to(vec_ref[0, dist][0:1, :], (SPAN, 2 * SPAN))
            bias_scr[dist] = pltpu.roll(vec, 0, 1, stride=1, stride_axis=0)[:, :SPAN]

    def step(span, carry, masked):
        st = pl.multiple_of(span * SPAN, SPAN)
        bias = bias_scr[qs - span]
        v = v_ref[0, pl.ds(st, SPAN), :]
        out = []
        for c in range(2):
            m, l, acc = carry[c]
            s = lax.dot_general(q[c], k_ref[0, pl.ds(st, SPAN), halves[c]], (((1,), (1,)), ((), ())),
                                preferred_element_type=F32) + bias
            if masked:
                s = jnp.where(causal, s, NEG_BIG)
            m_new = jnp.maximum(m, jnp.max(s, axis=1, keepdims=True))
            alpha = jnp.exp2(m - m_new)
            p = jnp.exp2(s - m_new)
            l = alpha * l + jnp.sum(p, axis=1, keepdims=True)
            acc = alpha * acc + jnp.dot(_bf(p), v, preferred_element_type=F32)
            out.append((m_new, l, acc))
        return tuple(out)

    def init():
        return (jnp.full((SPAN, 1), NEG_BIG, F32), jnp.zeros((SPAN, 1), F32),
                jnp.zeros((SPAN, DIFF_VDIM), F32))

    carry = step(qs, (init(), init()), True)
    carry = lax.fori_loop(0, qs, lambda i, c: step(i, c, False), carry)

    lp = lam_ref[...]
    lam = (jnp.exp(jnp.sum(lp[0:1] * lp[1:2], axis=1, keepdims=True))
           - jnp.exp(jnp.sum(lp[2:3] * lp[3:4], axis=1, keepdims=True)) + lam_init)
    (_, l0, a0), (_, l1, a1) = carry
    o = a0 / l0 - lam * (a1 / l1)
    ms = jnp.mean(o * o, axis=-1, keepdims=True)
    o = o * lax.rsqrt(ms + SUBLN_EPS) * subln_ref[...] * (1.0 - lam_init)
    o_ref[0] = o.astype(o_ref.dtype)


def _diff_attention(qkv, col0, bias_vecs, lam_params, subln, lam_init):
    bsz, s, _ = qkv.shape
    ns = s // SPAN
    return pl.pallas_call(
        functools.partial(_diff_kernel, lam_init=lam_init),
        grid=(H_D, bsz, ns),
        in_specs=[
            pl.BlockSpec((1, SPAN, LANES), lambda h, b, i: (b, i, col0 + h)),
            pl.BlockSpec((1, s, LANES), lambda h, b, i: (b, 0, col0 + H_D + h)),
            pl.BlockSpec((1, s, LANES), lambda h, b, i: (b, 0, col0 + 2 * H_D + h)),
            pl.BlockSpec((1, ns, SUBLANES, 2 * SPAN), lambda h, b, i: (h, 0, 0, 0)),
            pl.BlockSpec((4, DIFF_HALF), lambda h, b, i: (0, 0)),
            pl.BlockSpec((1, DIFF_VDIM), lambda h, b, i: (0, 0)),
        ],
        out_specs=pl.BlockSpec((1, SPAN, LANES), lambda h, b, i: (b, i, h)),
        out_shape=jax.ShapeDtypeStruct((bsz, s, GROUP_WIDTH), BF16),
        scratch_shapes=[pltpu.VMEM((ns, SPAN, SPAN), F32)],
        compiler_params=_params(("arbitrary", "arbitrary", "arbitrary")),
        name="diff_attention",
    )(qkv, qkv, qkv, bias_vecs, lam_params, subln)


def _dil_kernel(q_ref, k_ref, v_ref, b1_ref, b4_ref, b16_ref, y_ref, acc_scr, m_scr, l_scr):
    s_len = q_ref.shape[1]
    first_head = _iota2((BLOCK, LANES), 1) < HEAD_DIM
    bias_refs = (b1_ref, b4_ref, b16_ref)

    def stack_heads(t):
        return jnp.concatenate([jnp.where(first_head, t, 0.0), jnp.where(first_head, 0.0, t)], axis=0)

    def unstack(t):
        return jnp.where(first_head, t[:BLOCK], t[BLOCK:])

    for pi, (window, dil) in enumerate(DILATIONS):
        n_back = window // dil
        nc = s_len // dil // BLOCK
        has_prev = nc > 1
        assert not has_prev or nc % DIL_TILES_PER_STEP == 0
        width = 2 * BLOCK if has_prev else BLOCK
        qi = _iota2((2 * BLOCK, width), 0) & (BLOCK - 1)
        ki = _iota2((2 * BLOCK, width), 1)
        steps = qi + (BLOCK if has_prev else 0) - ki
        band = (steps >= 0) & (steps <= n_back)
        bias = bias_refs[pi][0]
        if not has_prev:
            bias = bias[:, BLOCK:]

        def body(it, carry, dil=dil, nc=nc, has_prev=has_prev, band=band, bias=bias, pi=pi, ki=ki):
            idx0 = it * DIL_TILES_PER_STEP
            tiles = []
            for u in range(DIL_TILES_PER_STEP):
                r, c = (idx0 + u) // nc, (idx0 + u) % nc
                tiles.append((pl.ds(r + dil * BLOCK * c, BLOCK, stride=dil), c))
            kb = [_bf(k_ref[0, rows, :]) for rows, _ in tiles]
            vb = [_bf(v_ref[0, rows, :]) for rows, _ in tiles]
            if has_prev:
                r0, c0 = idx0 // nc, idx0 % nc
                prows = pl.ds(r0 + dil * BLOCK * jnp.maximum(c0 - 1, 0), BLOCK, stride=dil)
                kb = [_bf(k_ref[0, prows, :])] + kb
                vb = [_bf(v_ref[0, prows, :])] + vb
            loads = []
            for u, (rows, c) in enumerate(tiles):
                q = _bf(stack_heads(q_ref[0, rows, :]))
                if has_prev:
                    k = jnp.concatenate([kb[u], kb[u + 1]], axis=0)
                    v = jnp.concatenate([vb[u], vb[u + 1]], axis=0)
                else:
                    k, v = kb[u], vb[u]
                loads.append((rows, c, q, k, v))
            ss = [lax.dot_general(q, k, (((1,), (1,)), ((), ())), preferred_element_type=F32) + bias
                  for _, _, q, k, _ in loads]
            if has_prev:
                ss = [jnp.where(band & (ki >= jnp.where(c > 0, 0, BLOCK)), s, NEG_BIG)
                      for s, (_, c, _, _, _) in zip(ss, loads)]
            else:
                ss = [jnp.where(band, s, NEG_BIG) for s in ss]
            ms = [jnp.max(s, axis=1, keepdims=True) for s in ss]
            ps = [jnp.exp2(s - m) for s, m in zip(ss, ms)]
            ls = [jnp.sum(p, axis=1, keepdims=True) for p in ps]
            os_ = [jnp.dot(_bf(p), v, preferred_element_type=F32) for p, (_, _, _, _, v) in zip(ps, loads)]
            for (rows, _, _, _, _), m, l, o in zip(loads, ms, ls, os_):
                acc_scr[pi, rows, :] = unstack(o)
                m_scr[pi, rows, :] = unstack(jnp.broadcast_to(m, (2 * BLOCK, LANES)))
                l_scr[pi, rows, :] = unstack(jnp.broadcast_to(l, (2 * BLOCK, LANES)))
            return carry

        lax.fori_loop(0, s_len // BLOCK // DIL_TILES_PER_STEP, body, 0)

    tm = 2 * BLOCK

    def merge(i, carry):
        rows = pl.ds(pl.multiple_of(i * tm, tm), tm)
        m1, m2, m3 = m_scr[0, rows, :], m_scr[1, rows, :], m_scr[2, rows, :]
        m = jnp.maximum(jnp.maximum(m1, m2), m3)
        e1, e2, e3 = jnp.exp2(m1 - m), jnp.exp2(m2 - m), jnp.exp2(m3 - m)
        num = e1 * acc_scr[0, rows, :] + e2 * acc_scr[1, rows, :] + e3 * acc_scr[2, rows, :]
        den = e1 * l_scr[0, rows, :] + e2 * l_scr[1, rows, :] + e3 * l_scr[2, rows, :]
        y_ref[0, rows, :] = (num / den).astype(y_ref.dtype)
        return carry

    lax.fori_loop(0, s_len // tm, merge, 0)


def _dilated_attention(qkv_c, biases):
    bsz, s, _ = qkv_c.shape
    nhp = GROUP_WIDTH // LANES
    bspec = pl.BlockSpec((1, 2 * BLOCK, 2 * BLOCK), lambda b, h: (h, 0, 0))
    scr = pltpu.VMEM((len(DILATIONS), s, LANES), F32)
    return pl.pallas_call(
        _dil_kernel,
        grid=(bsz, nhp),
        in_specs=[
            pl.BlockSpec((1, s, LANES), lambda b, h: (b, 0, h)),
            pl.BlockSpec((1, s, LANES), lambda b, h: (b, 0, nhp + h)),
            pl.BlockSpec((1, s, LANES), lambda b, h: (b, 0, 2 * nhp + h)),
            bspec, bspec, bspec,
        ],
        out_specs=pl.BlockSpec((1, s, LANES), lambda b, h: (b, 0, h)),
        out_shape=jax.ShapeDtypeStruct((bsz, s, GROUP_WIDTH), BF16),
        scratch_shapes=[scr, scr, scr],
        compiler_params=_params(("parallel", "parallel")),
        name="dilated_attention",
    )(qkv_c, qkv_c, qkv_c, *biases)


def _unit_lower_inverses(ns):
    size_all = 2 * CHUNK
    row = _iota2((size_all, size_all), 0)
    col = _iota2((size_all, size_all), 1)
    eye = jnp.where(row == col, 1.0, 0.0)
    base = 8
    same8 = (row >> 3) == (col >> 3)
    n8 = [jnp.where(same8, n, 0.0) for n in ns]
    n2 = [_mm(a, a) for a in n8]
    n4 = [_mm(a, a) for a in n2]
    ts = [eye + a for a in n8]
    ts = [t + _mm(t, b) for t, b in zip(ts, n2)]
    ts = [t + _mm(t, b) for t, b in zip(ts, n4)]
    size = 2 * base
    while size <= CHUNK:
        half = size // 2
        shift = size.bit_length() - 1
        off = (((row >> shift) == (col >> shift)) & ((row & (size - 1)) >= half)
               & ((col & (size - 1)) < half))
        us = [_mm(t, jnp.where(off, n, 0.0)) for t, n in zip(ts, ns)]
        ts = [t + _mm(u, t) for t, u in zip(ts, us)]
        size *= 2
    return ts


def _rwkv_kernel(*refs, has_vres):
    if has_vres:
        (pa_ref, mu_ref, w0_ref, wup_ref, a0_ref, aup_ref, gup_ref, kk_ref, ka_ref, rk_ref,
         lnw_ref, lnb_ref, seg_ref, cum_ref, vf_ref, vup_ref, vb_ref,
         y_ref, prev_scr, s_scr) = refs
    else:
        (pa_ref, mu_ref, w0_ref, wup_ref, a0_ref, aup_ref, gup_ref, kk_ref, ka_ref, rk_ref,
         lnw_ref, lnb_ref, seg_ref, cum_ref,
         y_ref, vf_out_ref, prev_scr, s_scr) = refs
    c = GROUP_WIDTH
    tt = pa_ref.shape[1]

    @pl.when(pl.program_id(1) == 0)
    def _():
        prev_scr[...] = jnp.zeros_like(prev_scr)
        s_scr[...] = jnp.zeros_like(s_scr)

    seg = seg_ref[...]

    def seg_sum(t):
        return jnp.concatenate([jnp.dot(_bf(t[:, p * LANES:(p + 1) * LANES]), seg, preferred_element_type=F32)
                                for p in range(c // LANES)], axis=1)

    pa = pa_ref[0]
    rolled = pltpu.roll(pa, 1, 0)
    first = _iota2(pa.shape, 0) == 0
    prev = jnp.where(first, prev_scr[...], rolled)
    prev_scr[...] = pa[tt - 1:tt, :]
    sh = pa + (prev - pa) * mu_ref[...]
    r = sh[:, 0:c]
    k = sh[:, c:2 * c]
    v = sh[:, 2 * c:3 * c]
    w_lo = sh[:, A_OFF_W:A_OFF_W + LORA_PAD]
    a_lo = sh[:, A_OFF_A:A_OFF_A + LORA_PAD]
    g_lo = sh[:, A_OFF_G:A_OFF_G + G_PAD]
    if has_vres:
        pv = sh[:, A_OFF_V:A_OFF_V + LORA_PAD]
        v = v + (vf_ref[0] - v) * _sigmoid(vb_ref[...] + _mm(pv, vup_ref[...]))
    else:
        vf_out_ref[0] = v

    w_log = -_softplus(-(w0_ref[...] + _mm(jnp.tanh(w_lo), wup_ref[...]))) - 0.5
    lw = -jnp.exp(w_log)
    a = _sigmoid(a0_ref[...] + _mm(a_lo, aup_ref[...]))
    g = _mm(_sigmoid(g_lo), gup_ref[...])
    kkr = k * kk_ref[...]
    kk = kkr * lax.rsqrt(jnp.maximum(seg_sum(kkr * kkr), 1e-24))
    k = k * (1.0 + (a - 1.0) * ka_ref[...])

    cum = _mm_split_rhs(cum_ref[...], lw, 2)
    eg = jnp.exp(cum)
    eig = jnp.exp(-cum)
    at = -kk * jnp.exp(cum - lw)
    rt = r * eg
    bt = kk * a * eig
    kt = k * eig

    pair_rows = 2 * CHUNK
    first_head = _iota2((CHUNK, LANES), 1) < HEAD_DIM
    row = _iota2((pair_rows, pair_rows), 0)
    col = _iota2((pair_rows, pair_rows), 1)
    strict = (row & (CHUNK - 1)) > (col & (CHUNK - 1))
    lower = (row & (CHUNK - 1)) >= (col & (CHUNK - 1))
    eye = row == col

    def stack(t):
        return jnp.concatenate([jnp.where(first_head, t, 0.0), jnp.where(first_head, 0.0, t)], axis=0)

    probs = [(ci, p) for ci in range(tt // CHUNK) for p in range(c // LANES)]
    cut = lambda t, ci, p: t[ci * CHUNK:(ci + 1) * CHUNK, p * LANES:(p + 1) * LANES]
    g_end = [cut(eg, ci, p)[CHUNK - 1:CHUNK, :] for ci, p in probs]
    at_p = [stack(cut(at, ci, p)) for ci, p in probs]
    rt_p = [stack(cut(rt, ci, p)) for ci, p in probs]
    bt_p = [stack(cut(bt, ci, p)) for ci, p in probs]
    kt_p = [stack(cut(kt, ci, p)) for ci, p in probs]
    v_p = [stack(cut(v, ci, p)) for ci, p in probs]
    bg_p = [b_ * ge for b_, ge in zip(bt_p, g_end)]
    kg_p = [k_ * ge for k_, ge in zip(kt_p, g_end)]
    grams = [_mm_nt(jnp.concatenate([a_, r_], axis=0), jnp.concatenate([b_, k_], axis=0))
             for a_, r_, b_, k_ in zip(at_p, rt_p, bt_p, kt_p)]
    a_ab = [jnp.where(strict, gm[:pair_rows, :pair_rows], 0.0) for gm in grams]
    a_ak = [jnp.where(strict, gm[:pair_rows, pair_rows:], 0.0) for gm in grams]
    a_rb = [jnp.where(lower, gm[pair_rows:, :pair_rows], 0.0) for gm in grams]
    a_rk = [jnp.where(lower, gm[pair_rows:, pair_rows:], 0.0) for gm in grams]
    both_v = [_mm(jnp.concatenate([a_, r_], axis=0), v_) for a_, r_, v_ in zip(a_ak, a_rk, v_p)]
    akv = [x_[:pair_rows] for x_ in both_v]
    rkv = [x_[pair_rows:] for x_ in both_v]
    ts = _unit_lower_inverses(a_ab)
    pq = [_mm(t, jnp.concatenate([a_, x_], axis=1)) for t, a_, x_ in zip(ts, at_p, akv)]
    ps = [x_[:, :LANES] for x_ in pq]
    qs = [x_[:, LANES:] for x_ in pq]
    rb_pq = [_mm(a_, x_) for a_, x_ in zip(a_rb, pq)]
    r2 = [r_ + x_[:, :LANES] for r_, x_ in zip(rt_p, rb_pq)]
    y0 = [x_[:, LANES:] + k_ for x_, k_ in zip(rb_pq, rkv)]
    ms = [jnp.where(eye, jnp.broadcast_to(ge, (pair_rows, LANES)), 0.0) + _mm_tn(p_, b_)
          for ge, p_, b_ in zip(g_end, ps, bg_p)]
    zs = [_mm_tn(jnp.concatenate([q_, v_], axis=0), jnp.concatenate([b_, k_], axis=0))
          for q_, b_, v_, k_ in zip(qs, bg_p, v_p, kg_p)]

    state = [s_scr[p] for p in range(c // LANES)]
    y_rows = []
    for ci in range(tt // CHUNK):
        y_pairs = []
        for p in range(c // LANES):
            i = ci * (c // LANES) + p
            y_st = _mm_nt(r2[i], state[p]) + y0[i]
            y_pairs.append(y_st[:CHUNK] + y_st[CHUNK:])
            state[p] = _mm(state[p], ms[i]) + zs[i]
        y_rows.append(jnp.concatenate(y_pairs, axis=1))
    for p in range(c // LANES):
        s_scr[p] = state[p]
    y = jnp.concatenate(y_rows, axis=0)

    mu = seg_sum(y) * (1.0 / HEAD_DIM)
    d = y - mu
    var = seg_sum(d * d) * (1.0 / HEAD_DIM)
    yn = d * lax.rsqrt(var + RWKV_LN_EPS) * lnw_ref[...] + lnb_ref[...]
    bonus = seg_sum(r * k * rk_ref[...]) * v
    y_ref[0] = ((yn + bonus) * g).astype(y_ref.dtype)


def _rwkv7(pa, prm, v_first):
    bsz, s, _ = pa.shape
    na = prm["mu"].shape[1]
    c = GROUP_WIDTH
    tt = RWKV_TILE
    has_vres = v_first is not None
    full = lambda shape: pl.BlockSpec(shape, lambda b, t: (0,) * len(shape))
    tile = lambda w: pl.BlockSpec((1, tt, w), lambda b, t: (b, t, 0))
    idx = jnp.arange(tt)
    cum_ones = ((idx[:, None] >= idx[None, :]) & (idx[:, None] // CHUNK == idx[None, :] // CHUNK)).astype(BF16)
    in_specs = [tile(na), full((1, na)), full((1, c)), full((LORA_PAD, c)), full((1, c)),
                full((LORA_PAD, c)), full((G_PAD, c)), full((1, c)), full((1, c)), full((1, c)),
                full((1, c)), full((1, c)), full((LANES, LANES)), full((tt, tt))]
    args = [pa, prm["mu"], prm["w0"], prm["w_up"], prm["a0"], prm["a_up"], prm["g_up"],
            prm["k_k"], prm["k_a"], prm["r_k"], prm["ln_w"], prm["ln_b"], prm["seg"], cum_ones]
    y_shape = jax.ShapeDtypeStruct((bsz, s, c), BF16)
    if has_vres:
        in_specs += [tile(c), full((LORA_PAD, c)), full((1, c))]
        args += [v_first, prm["vres_up"], prm["vres_bias"]]
        out_specs, out_shape = tile(c), y_shape
    else:
        out_specs = [tile(c), tile(c)]
        out_shape = [y_shape, jax.ShapeDtypeStruct((bsz, s, c), F32)]
    res = pl.pallas_call(
        functools.partial(_rwkv_kernel, has_vres=has_vres),
        grid=(bsz, s // tt),
        in_specs=in_specs,
        out_specs=out_specs,
        out_shape=out_shape,
        scratch_shapes=[pltpu.VMEM((1, na), F32), pltpu.VMEM((c // LANES, LANES, LANES), F32)],
        compiler_params=_params(("parallel", "arbitrary")),
        name="rwkv7_vres" if has_vres else "rwkv7",
    )(*args)
    if has_vres:
        return res, v_first
    return res[0], res[1]


def _rel_bucket(dist):
    max_exact = NUM_BUCKETS // 2
    d_f = jnp.maximum(dist, 1).astype(F32)
    large = max_exact + (jnp.log(d_f / max_exact) / math.log(MAX_DISTANCE / max_exact)
                         * (NUM_BUCKETS - max_exact)).astype(jnp.int32)
    large = jnp.minimum(large, NUM_BUCKETS - 1)
    return jnp.where(dist < max_exact, dist, large)


def _toeplitz(u, n_rows, n_cols):
    length = n_rows + n_cols - 1
    g = jnp.roll(jnp.flip(u, axis=-1), -(n_rows - 1), axis=-1)
    flat = jnp.tile(g, (1,) * (u.ndim - 1) + (n_rows,))[..., :n_rows * (length - 1)]
    return flat.reshape(u.shape[:-1] + (n_rows, length - 1))[..., :n_cols]


def _diff_bias_vectors(rel_tab, ns):
    n = jnp.arange(2 * SPAN)
    rel = jnp.where(n < SPAN, -n, 2 * SPAN - n)
    dist = jnp.arange(ns)[:, None] * SPAN + rel[None, :]
    u = rel_tab[_rel_bucket(jnp.maximum(dist, 0))] * LOG2E
    u = jnp.transpose(u, (2, 0, 1)).astype(F32)
    return jnp.broadcast_to(u[:, :, None, :], (H_D, ns, SUBLANES, 2 * SPAN))


def _band_bias(rel_tab, dil):
    steps = jnp.arange(3 * BLOCK - 1) - (BLOCK - 1)
    u = rel_tab[_rel_bucket(jnp.maximum(steps, 0) * dil)] * LOG2E
    t = _toeplitz(jnp.transpose(u, (1, 0)).astype(F32), BLOCK, 2 * BLOCK)
    return t.reshape(N_HEADS * HEAD_DIM // LANES, 2 * BLOCK, 2 * BLOCK)


def _pad_axis(t, size, axis):
    pad = [(0, 0)] * t.ndim
    pad[axis] = (0, size - t.shape[axis])
    return jnp.pad(t, pad)


def _layout_mixer_a(t, vres=None):
    c = GROUP_WIDTH
    o_w, o_a, o_g = 3 * c, 3 * c + W_LORA, 3 * c + W_LORA + A_LORA
    parts = [t[..., :o_w], _pad_axis(t[..., o_w:o_a], LORA_PAD, -1),
             _pad_axis(t[..., o_a:o_g], LORA_PAD, -1), _pad_axis(t[..., o_g:N_A_COLS], G_PAD, -1)]
    if vres is not None:
        parts.append(_pad_axis(vres, PROJ_TN, -1))
    return jnp.concatenate(parts, axis=-1)


def _scale_queries(w_qkv, scale):
    return jnp.concatenate([w_qkv[:, :GROUP_WIDTH] * scale, w_qkv[:, GROUP_WIDTH:]], axis=1)


def kernel(x, c, w_ada, b_ada, norm_gain, w_in, w_out, rel_bias, rwkv_mu, rwkv_w0, rwkv_w_up,
           rwkv_a0, rwkv_a_up, rwkv_g_up, rwkv_k_k, rwkv_k_a, rwkv_r_k, rwkv_ln_w, rwkv_ln_b,
           vres_down, vres_mu, vres_up, vres_bias, diff_lambda, diff_subln, ffn_w13, ffn_w2,
           final_gain):
    out_dtype = x.dtype
    bsz, s, d = x.shape
    cw = GROUP_WIDTH
    x = x.astype(F32)
    row = lambda t: t.reshape(1, -1).astype(F32)

    mod = _modulation(c.astype(F32), w_ada, b_ada)
    rel_tab_c = rel_bias[:, :N_HEADS].astype(F32)
    rel_tab_d = rel_bias[:, N_HEADS:].astype(F32)
    band_biases = [_band_bias(rel_tab_c, dil) for _, dil in DILATIONS]
    diff_bias = _diff_bias_vectors(rel_tab_d, s // SPAN)
    lane = jnp.arange(LANES)
    seg = (lane[:, None] // HEAD_DIM == lane[None, :] // HEAD_DIM).astype(BF16)
    logit_scale = LOG2E * HEAD_DIM ** -0.5

    v_first = None
    for l in range(DEPTH):
        sh1, sc1, g1, sh2, sc2, g2 = [mod[l, :, None, i * d:(i + 1) * d] for i in range(6)]
        gain1, gain2 = row(norm_gain[l, 0]), row(norm_gain[l, 1])
        vres_w = vres_down[l - 1] if l > 0 else None
        w_l = w_in[l].astype(F32)
        w_a = _layout_mixer_a(w_l[:, :N_A_COLS], vres_w)
        o_b, o_c, o_d = N_A_COLS, N_A_COLS + 3 * cw, N_A_COLS + 6 * cw
        w_all = _bf(jnp.concatenate([w_a, _scale_queries(w_l[:, o_b:o_c], logit_scale),
                                     _scale_queries(w_l[:, o_d:N_IN], logit_scale),
                                     _scale_queries(w_l[:, o_c:o_d], logit_scale)], axis=1))
        na_used = A_OFF_V + (LORA_PAD if l > 0 else 0)
        mu = _layout_mixer_a(rwkv_mu[l], vres_mu[l - 1] if l > 0 else None)[..., :na_used]

        pa, qkv_bd, qkv_c = _norm_proj(x, gain1, sc1, sh1, w_all, (w_a.shape[1], 6 * cw, 3 * cw),
                                       (F32, BF16, F32))

        prm = {
            "mu": row(mu), "w0": row(rwkv_w0[l]), "a0": row(rwkv_a0[l]),
            "w_up": _bf(_pad_axis(rwkv_w_up[l], LORA_PAD, 0)),
            "a_up": _bf(_pad_axis(rwkv_a_up[l], LORA_PAD, 0)),
            "g_up": _bf(_pad_axis(rwkv_g_up[l], G_PAD, 0)),
            "k_k": row(rwkv_k_k[l]), "k_a": row(rwkv_k_a[l]), "r_k": row(rwkv_r_k[l]),
            "ln_w": row(rwkv_ln_w[l]), "ln_b": row(rwkv_ln_b[l]), "seg": seg,
        }
        if l > 0:
            prm["vres_up"] = _bf(_pad_axis(vres_up[l - 1], LORA_PAD, 0))
            prm["vres_bias"] = row(vres_bias[l - 1])
        y_a, v_first = _rwkv7(pa, prm, v_first)

        y_b = _stick_breaking(qkv_bd, 0)
        y_c = _dilated_attention(qkv_c, band_biases)
        lam_init = 0.8 - 0.6 * math.exp(-0.3 * l)
        y_d = _diff_attention(qkv_bd, 3 * cw // LANES, diff_bias, diff_lambda[l].astype(F32),
                              row(diff_subln[l]), lam_init)

        x = _out_proj(y_a, y_b, y_c, y_d, _bf(w_out[l]), x, g1)
        x = _ffn(x, gain2, sc2, sh2, g2, _bf(ffn_w13[l]), _bf(ffn_w2[l]), row(final_gain),
                 final=(l == DEPTH - 1))
    return x.astype(out_dtype)
```

```python
import functools
import math

import jax
import jax.numpy as jnp
from jax import lax
from jax.experimental import pallas as pl
from jax.experimental.pallas import tpu as pltpu

F32 = jnp.float32
BF16 = jnp.bfloat16

D_MODEL = 2048
DEPTH = 2
HEAD_DIM = 64
GROUP_WIDTH = D_MODEL // 4
N_HEADS = GROUP_WIDTH // HEAD_DIM
DIFF_HALF = HEAD_DIM
DIFF_VDIM = 2 * DIFF_HALF
H_D = GROUP_WIDTH // DIFF_VDIM
W_LORA = max(32, int(round(D_MODEL ** 0.5 * 1.8 / 32)) * 32)
A_LORA = max(32, int(round(D_MODEL ** 0.5 * 1.8 / 32)) * 32)
V_LORA = max(32, int(round(D_MODEL ** 0.5 * 1.3 / 32)) * 32)
G_LORA = max(32, int(round(D_MODEL ** 0.8 / 32)) * 32)
N_A_COLS = 3 * GROUP_WIDTH + W_LORA + A_LORA + G_LORA
N_IN = N_A_COLS + 9 * GROUP_WIDTH
D_FF = ((8 * D_MODEL + 3 * 256 - 1) // (3 * 256)) * 256
DILATIONS = ((128, 1), (512, 4), (2048, 16))
BLOCK = 128
NUM_BUCKETS = 32
MAX_DISTANCE = 2048
NORM_EPS = 1e-6
RWKV_LN_EPS = 64e-5
SUBLN_EPS = 1e-5
LOG2E = math.log2(math.e)

LANES = 128
SUBLANES = 8
VMEM_LIMIT_BYTES = 56 * 1024 * 1024

LORA_PAD = 128
G_PAD = 512
A_OFF_W = 3 * GROUP_WIDTH
A_OFF_A = A_OFF_W + LORA_PAD
A_OFF_G = A_OFF_A + LORA_PAD
NA = A_OFF_G + G_PAD
A_VRES_TILE = A_OFF_G + (G_LORA // LANES) * LANES
VRES_LANE0 = G_LORA % LANES
assert G_LORA + V_LORA <= G_PAD and VRES_LANE0 + V_LORA <= LANES
PROJ_TN = 768
FFN_TF = 512
FIRST_STEP_ROWS = 256
CHUNK = 64
RWKV_TILE = 256
SPAN_BLOCKS = 4
SPAN = SPAN_BLOCKS * BLOCK
DIL_TILES_PER_STEP = 4
NEG_BIG = -1e30
SB_ROW_GROUPS = 2
SB_UNDERFLOW_LOG2 = 160.0


def _bf(x):
    return x.astype(BF16)


def _mm(a, b):
    return jnp.dot(_bf(a), _bf(b), preferred_element_type=F32)


def _mm_nt(a, b):
    return lax.dot_general(_bf(a), _bf(b), (((1,), (1,)), ((), ())), preferred_element_type=F32)


def _mm_tn(a, b):
    return lax.dot_general(_bf(a), _bf(b), (((0,), (0,)), ((), ())), preferred_element_type=F32)


def _split_bf16(x, n):
    parts = []
    rem = x
    for i in range(n):
        p = _bf(rem)
        parts.append(p)
        if i + 1 < n:
            rem = rem - p.astype(F32)
    return parts


def _mm_split_lhs(a, b_bf16, n):
    out = None
    for p in _split_bf16(a, n):
        t = jnp.dot(p, b_bf16, preferred_element_type=F32)
        out = t if out is None else out + t
    return out


def _mm_split_rhs(a_bf16, b, n):
    out = None
    for p in _split_bf16(b, n):
        t = jnp.dot(a_bf16, p, preferred_element_type=F32)
        out = t if out is None else out + t
    return out


def _softplus(x):
    return jnp.maximum(x, 0.0) + jnp.log1p(jnp.exp(-jnp.abs(x)))


def _softplus2(x):
    sign_bit = jnp.uint32(0x80000000)
    neg_abs = lax.bitcast_convert_type(lax.bitcast_convert_type(x, jnp.uint32) | sign_bit, F32)
    return jnp.maximum(x, 0.0) + jnp.log2(1.0 + jnp.exp2(neg_abs))


def _sigmoid(x):
    return jax.nn.sigmoid(x)


def _iota2(shape, dim):
    return lax.broadcasted_iota(jnp.int32, shape, dim)


def _params(sem, vmem=VMEM_LIMIT_BYTES):
    return pltpu.CompilerParams(dimension_semantics=sem, vmem_limit_bytes=vmem)


def _mod_kernel(c_ref, w_ref, b_ref, o_ref):
    c = c_ref[...]
    cond = c * _sigmoid(c)
    o_ref[0] = _mm(cond, w_ref[0]) + b_ref[0]


def _modulation(c, w_ada, b_ada):
    depth, d, n = w_ada.shape
    bsz = c.shape[0]
    tn = 1024
    return pl.pallas_call(
        _mod_kernel,
        grid=(depth, n // tn),
        in_specs=[
            pl.BlockSpec((bsz, d), lambda l, j: (0, 0)),
            pl.BlockSpec((1, d, tn), lambda l, j: (l, 0, j)),
            pl.BlockSpec((1, 1, tn), lambda l, j: (l, 0, j)),
        ],
        out_specs=pl.BlockSpec((1, bsz, tn), lambda l, j: (l, 0, j)),
        out_shape=jax.ShapeDtypeStruct((depth, bsz, n), F32),
        compiler_params=_params(("parallel", "parallel")),
        name="adaln_mod",
    )(c, w_ada, b_ada.reshape(depth, 1, n))


def _norm_mod(x, gain, sc, sh):
    ms = jnp.mean(x * x, axis=-1, keepdims=True)
    return x * lax.rsqrt(ms + NORM_EPS) * gain * (1.0 + sc) + sh


def _proj_kernel(x_ref, gain_ref, sc_ref, sh_ref, w_ref, *rest, splits):
    o_refs, h_scr = rest[:-1], rest[-1]
    j = pl.program_id(2)
    tm = h_scr.shape[0]

    @pl.when(j == 0)
    def _():
        for g in range(tm // FIRST_STEP_ROWS):
            rows = slice(g * FIRST_STEP_ROWS, (g + 1) * FIRST_STEP_ROWS)
            h = _bf(_norm_mod(x_ref[0, rows, :], gain_ref[...], sc_ref[0], sh_ref[0]))
            h_scr[rows, :] = h
            o_refs[0][0, rows, :] = jnp.dot(h, w_ref[...], preferred_element_type=F32).astype(o_refs[0].dtype)

    lo = 0
    for o_ref, n_tiles in zip(o_refs, splits):
        @pl.when((j >= max(lo, 1)) & (j < lo + n_tiles))
        def _(o_ref=o_ref):
            o_ref[0] = jnp.dot(h_scr[...], w_ref[...], preferred_element_type=F32).astype(o_ref.dtype)
        lo += n_tiles


def _norm_proj(x, gain, sc, sh, w, widths, dtypes):
    bsz, s, d = x.shape
    tm, tn = 1024, PROJ_TN
    splits = tuple(n // tn for n in widths)
    starts = [sum(splits[:k]) for k in range(len(splits))]

    def out_spec(k):
        return pl.BlockSpec((1, tm, tn),
                            lambda b, i, j: (b, i, jnp.clip(j - starts[k], 0, splits[k] - 1)))

    return pl.pallas_call(
        functools.partial(_proj_kernel, splits=splits),
        grid=(bsz, s // tm, sum(splits)),
        in_specs=[
            pl.BlockSpec((1, tm, d), lambda b, i, j: (b, i, 0)),
            pl.BlockSpec((1, d), lambda b, i, j: (0, 0)),
            pl.BlockSpec((1, 1, d), lambda b, i, j: (b, 0, 0)),
            pl.BlockSpec((1, 1, d), lambda b, i, j: (b, 0, 0)),
            pl.BlockSpec((d, tn), lambda b, i, j: (0, j)),
        ],
        out_specs=[out_spec(k) for k in range(len(widths))],
        out_shape=[jax.ShapeDtypeStruct((bsz, s, n), dt) for n, dt in zip(widths, dtypes)],
        scratch_shapes=[pltpu.VMEM((tm, d), BF16)],
        compiler_params=_params(("parallel", "parallel", "arbitrary")),
        name="in_proj",
    )(x, gain, sc, sh, w)


def _outproj_kernel(ya_ref, yb_ref, yc_ref, yd_ref, w_ref, x_ref, g_ref, o_ref):
    c = GROUP_WIDTH
    acc = jnp.dot(ya_ref[0], w_ref[0:c, :], preferred_element_type=F32)
    acc += jnp.dot(yb_ref[0], w_ref[c:2 * c, :], preferred_element_type=F32)
    acc += jnp.dot(yc_ref[0], w_ref[2 * c:3 * c, :], preferred_element_type=F32)
    acc += jnp.dot(yd_ref[0], w_ref[3 * c:4 * c, :], preferred_element_type=F32)
    o_ref[0] = x_ref[0] + g_ref[0] * acc


def _out_proj(ya, yb, yc, yd, w, x, g1):
    bsz, s, d = x.shape
    c = GROUP_WIDTH
    tm, tn = 512, d
    yspec = pl.BlockSpec((1, tm, c), lambda b, i, j: (b, i, 0))
    return pl.pallas_call(
        _outproj_kernel,
        grid=(bsz, s // tm, d // tn),
        in_specs=[
            yspec, yspec, yspec, yspec,
            pl.BlockSpec((4 * c, tn), lambda b, i, j: (0, j)),
            pl.BlockSpec((1, tm, tn), lambda b, i, j: (b, i, j)),
            pl.BlockSpec((1, 1, tn), lambda b, i, j: (b, 0, j)),
        ],
        out_specs=pl.BlockSpec((1, tm, tn), lambda b, i, j: (b, i, j)),
        out_shape=jax.ShapeDtypeStruct((bsz, s, d), F32),
        compiler_params=_params(("parallel", "parallel", "arbitrary")),
        name="out_proj",
    )(ya, yb, yc, yd, w, x, g1)


def _ffn_kernel(x_ref, gain_ref, sc_ref, sh_ref, g_ref, w13_ref, w2_ref, fg_ref,
                o_ref, h_scr, *, final):
    f = pl.program_id(2)
    tm = h_scr.shape[0]
    tf = w2_ref.shape[0]
    acc_scr = o_ref.at[0]

    def chunk(h):
        gate_up = jnp.dot(h, w13_ref[...], preferred_element_type=F32)
        gate, up = gate_up[:, :tf], gate_up[:, tf:]
        act = gate * _sigmoid(gate) * up
        return jnp.dot(_bf(act), w2_ref[...], preferred_element_type=F32)

    @pl.when(f == 0)
    def _():
        for g in range(tm // FIRST_STEP_ROWS):
            rows = slice(g * FIRST_STEP_ROWS, (g + 1) * FIRST_STEP_ROWS)
            h = _bf(_norm_mod(x_ref[0, rows, :], gain_ref[...], sc_ref[0], sh_ref[0]))
            h_scr[rows, :] = h
            acc_scr[rows, :] = chunk(h)

    @pl.when(f > 0)
    def _():
        acc_scr[...] += chunk(h_scr[...])

    @pl.when(f == pl.num_programs(2) - 1)
    def _():
        xn = x_ref[0] + g_ref[0] * acc_scr[...]
        if final:
            ms = jnp.mean(xn * xn, axis=-1, keepdims=True)
            xn = xn * lax.rsqrt(ms + NORM_EPS) * fg_ref[...]
        o_ref[0] = xn


def _interleave_gate_up(w13):
    d, two_f = w13.shape
    return w13.reshape(d, 2, two_f // 2 // FFN_TF, FFN_TF).transpose(0, 2, 1, 3).reshape(d, two_f)


def _ffn(x, gain, sc, sh, g2, w13, w2, final_gain, final):
    bsz, s, d = x.shape
    dff = w2.shape[0]
    tm, tf = 512, FFN_TF
    nf = dff // tf
    vec = pl.BlockSpec((1, 1, d), lambda b, i, f: (b, 0, 0))
    return pl.pallas_call(
        functools.partial(_ffn_kernel, final=final),
        grid=(bsz, s // tm, nf),
        in_specs=[
            pl.BlockSpec((1, tm, d), lambda b, i, f: (b, i, 0)),
            pl.BlockSpec((1, d), lambda b, i, f: (0, 0)),
            vec, vec, vec,
            pl.BlockSpec((d, 2 * tf), lambda b, i, f: (0, f)),
            pl.BlockSpec((tf, d), lambda b, i, f: (f, 0)),
            pl.BlockSpec((1, d), lambda b, i, f: (0, 0)),
        ],
        out_specs=pl.BlockSpec((1, tm, d), lambda b, i, f: (b, i, 0)),
        out_shape=jax.ShapeDtypeStruct((bsz, s, d), F32),
        scratch_shapes=[pltpu.VMEM((tm, d), BF16)],
        compiler_params=_params(("parallel", "parallel", "arbitrary")),
        name="ffn_final" if final else "ffn",
    )(x, gain, sc, sh, g2, w13, w2, final_gain)


def _sb_kernel(q_ref, k_ref, v_ref, later_ref, o_ref):
    qs = pl.program_id(2)
    nh = LANES // HEAD_DIM
    later = later_ref[...]
    row = _iota2((SPAN, BLOCK), 0)
    col = _iota2((SPAN, BLOCK), 1)

    def span_tile(q, sl, span, run, masked):
        st = pl.multiple_of(span * SPAN, SPAN)
        n_rows = q.shape[0]
        z = lax.dot_general(q, k_ref[0, pl.ds(st, SPAN), sl], (((1,), (1,)), ((), ())),
                            preferred_element_type=F32)
        subs = []
        for j in range(SPAN_BLOCKS):
            zj = z[:, j * BLOCK:(j + 1) * BLOCK]
            cost = _softplus2(zj)
            mask = None
            if masked:
                mask = (j * BLOCK + col) < row
                cost = jnp.where(mask, cost, 0.0)
            subs.append((zj, cost, mask))
        stacked = jnp.concatenate([_bf(cost) for _, cost, _ in subs], axis=0)
        within = jnp.dot(stacked, later, preferred_element_type=F32)
        atts = [None] * SPAN_BLOCKS
        for j in reversed(range(SPAN_BLOCKS)):
            zj, cost, mask = subs[j]
            w = within[j * n_rows:(j + 1) * n_rows]
            att = jnp.exp2(zj - cost - w - run)
            if masked:
                att = jnp.where(mask, att, 0.0)
            atts[j] = _bf(att)
            run = run + jnp.sum(cost, axis=1, keepdims=True)
        contrib = jnp.dot(jnp.concatenate(atts, axis=1), v_ref[0, pl.ds(st, SPAN), sl],
                          preferred_element_type=F32)
        return contrib, run

    lanes = [slice(h * HEAD_DIM, (h + 1) * HEAD_DIM) for h in range(nh)]

    def step(span, rows, carry, masked):
        out = []
        for sl, (acc, run) in zip(lanes, carry):
            contrib, run = span_tile(q_ref[0, rows, sl], sl, span, run, masked)
            out.append((acc + contrib, run))
        return tuple(out)

    init = tuple((jnp.zeros((SPAN, HEAD_DIM), F32), jnp.zeros((SPAN, 1), F32)) for _ in range(nh))
    diag = step(qs, slice(None), init, True)

    group = SPAN // SB_ROW_GROUPS
    outs = [[] for _ in range(nh)]
    for g in range(SB_ROW_GROUPS):
        rows = slice(g * group, (g + 1) * group)

        def more(state):
            i, carry = state
            least = functools.reduce(jnp.minimum, [jnp.min(run) for _, run in carry])
            return (i < qs) & (least < SB_UNDERFLOW_LOG2)

        def body(state, rows=rows):
            i, carry = state
            return i + 1, step(qs - 1 - i, rows, carry, False)

        start = tuple((acc[rows], run[rows]) for acc, run in diag)
        _, carry = lax.while_loop(more, body, (jnp.int32(0), start))
        for h, (acc, _) in enumerate(carry):
            outs[h].append(acc)
    for sl, parts in zip(lanes, outs):
        o_ref[0, :, sl] = jnp.concatenate(parts, axis=0).astype(o_ref.dtype)


def _stick_breaking(qkv, col0):
    bsz, s, _ = qkv.shape
    nhp = GROUP_WIDTH // LANES
    later = jnp.tril(jnp.ones((BLOCK, BLOCK), F32), -1).astype(BF16)
    return pl.pallas_call(
        _sb_kernel,
        grid=(bsz, nhp, s // SPAN),
        in_specs=[
            pl.BlockSpec((1, SPAN, LANES), lambda b, h, i: (b, i, col0 + h)),
            pl.BlockSpec((1, s, LANES), lambda b, h, i: (b, 0, col0 + nhp + h)),
            pl.BlockSpec((1, s, LANES), lambda b, h, i: (b, 0, col0 + 2 * nhp + h)),
            pl.BlockSpec((BLOCK, BLOCK), lambda b, h, i: (0, 0)),
        ],
        out_specs=pl.BlockSpec((1, SPAN, LANES), lambda b, h, i: (b, i, h)),
        out_shape=jax.ShapeDtypeStruct((bsz, s, GROUP_WIDTH), BF16),
        compiler_params=_params(("parallel", "parallel", "arbitrary")),
        name="stick_breaking",
    )(qkv, qkv, qkv, later)


def _diff_kernel(q_ref, k_ref, v_ref, vec_ref, lam_ref, subln_ref, o_ref, bias_scr, *, lam_init):
    qs = pl.program_id(2)
    causal = _iota2((SPAN, SPAN), 1) <= _iota2((SPAN, SPAN), 0)
    halves = [slice(c * DIFF_HALF, (c + 1) * DIFF_HALF) for c in range(2)]
    q = [q_ref[0, :, sl] for sl in halves]

    @pl.when((pl.program_id(1) == 0) & (qs == 0))
    def _():
        for dist in range(bias_scr.shape[0]):
            vec = jnp.broadcast_to(vec_ref[0, dist][0:1, :], (SPAN, 2 * SPAN))
            bias_scr[dist] = pltpu.roll(vec, 0, 1, stride=1, stride_axis=0)[:, :SPAN]

    def step(span, carry, masked):
        st = pl.multiple_of(span * SPAN, SPAN)
        bias = bias_scr[qs - span]
        v = v_ref[0, pl.ds(st, SPAN), :]
        out = []
        for c in range(2):
            m, l, acc = carry[c]
            s = lax.dot_general(q[c], k_ref[0, pl.ds(st, SPAN), halves[c]], (((1,), (1,)), ((), ())),
                                preferred_element_type=F32) + bias
            if masked:
                s = jnp.where(causal, s, NEG_BIG)
            m_new = jnp.maximum(m, jnp.max(s, axis=1, keepdims=True))
            alpha = jnp.exp2(m - m_new)
            p = jnp.exp2(s - m_new)
            l = alpha * l + jnp.sum(p, axis=1, keepdims=True)
            acc = alpha * acc + jnp.dot(_bf(p), v, preferred_element_type=F32)
            out.append((m_new, l, acc))
        return tuple(out)

    def init():
        return (jnp.full((SPAN, 1), NEG_BIG, F32), jnp.zeros((SPAN, 1), F32),
                jnp.zeros((SPAN, DIFF_VDIM), F32))

    carry = step(qs, (init(), init()), True)
    carry = lax.fori_loop(0, qs, lambda i, c: step(i, c, False), carry)

    lp = lam_ref[...]
    lam = (jnp.exp(jnp.sum(lp[0:1] * lp[1:2], axis=1, keepdims=True))
           - jnp.exp(jnp.sum(lp[2:3] * lp[3:4], axis=1, keepdims=True)) + lam_init)
    (_, l0, a0), (_, l1, a1) = carry
    o = a0 / l0 - lam * (a1 / l1)
    ms = jnp.mean(o * o, axis=-1, keepdims=True)
    o = o * lax.rsqrt(ms + SUBLN_EPS) * subln_ref[...] * (1.0 - lam_init)
    o_ref[0] = o.astype(o_ref.dtype)


def _diff_attention(qkv, col0, bias_vecs, lam_params, subln, lam_init):
    bsz, s, _ = qkv.shape
    ns = s // SPAN
    return pl.pallas_call(
        functools.partial(_diff_kernel, lam_init=lam_init),
        grid=(H_D, bsz, ns),
        in_specs=[
            pl.BlockSpec((1, SPAN, LANES), lambda h, b, i: (b, i, col0 + h)),
            pl.BlockSpec((1, s, LANES), lambda h, b, i: (b, 0, col0 + H_D + h)),
            pl.BlockSpec((1, s, LANES), lambda h, b, i: (b, 0, col0 + 2 * H_D + h)),
            pl.BlockSpec((1, ns, SUBLANES, 2 * SPAN), lambda h, b, i: (h, 0, 0, 0)),
            pl.BlockSpec((4, DIFF_HALF), lambda h, b, i: (0, 0)),
            pl.BlockSpec((1, DIFF_VDIM), lambda h, b, i: (0, 0)),
        ],
        out_specs=pl.BlockSpec((1, SPAN, LANES), lambda h, b, i: (b, i, h)),
        out_shape=jax.ShapeDtypeStruct((bsz, s, GROUP_WIDTH), BF16),
        scratch_shapes=[pltpu.VMEM((ns, SPAN, SPAN), F32)],
        compiler_params=_params(("arbitrary", "arbitrary", "arbitrary")),
        name="diff_attention",
    )(qkv, qkv, qkv, bias_vecs, lam_params, subln)


def _dil_kernel(q_ref, k_ref, v_ref, b1_ref, b4_ref, b16_ref, y_ref, acc_scr, m_scr, l_scr):
    s_len = q_ref.shape[1]
    first_head = _iota2((BLOCK, LANES), 1) < HEAD_DIM
    bias_refs = (b1_ref, b4_ref, b16_ref)

    def stack_heads(t):
        return jnp.concatenate([jnp.where(first_head, t, 0.0), jnp.where(first_head, 0.0, t)], axis=0)

    def unstack(t):
        return jnp.where(first_head, t[:BLOCK], t[BLOCK:])

    for pi, (window, dil) in enumerate(DILATIONS):
        n_back = window // dil
        nc = s_len // dil // BLOCK
        has_prev = nc > 1
        assert not has_prev or nc % DIL_TILES_PER_STEP == 0
        width = 2 * BLOCK if has_prev else BLOCK
        qi = _iota2((2 * BLOCK, width), 0) & (BLOCK - 1)
        ki = _iota2((2 * BLOCK, width), 1)
        steps = qi + (BLOCK if has_prev else 0) - ki
        band = (steps >= 0) & (steps <= n_back)
        bias = bias_refs[pi][0]
        if not has_prev:
            bias = bias[:, BLOCK:]

        def body(it, carry, dil=dil, nc=nc, has_prev=has_prev, band=band, bias=bias, pi=pi, ki=ki):
            idx0 = it * DIL_TILES_PER_STEP
            tiles = []
            for u in range(DIL_TILES_PER_STEP):
                r, c = (idx0 + u) // nc, (idx0 + u) % nc
                tiles.append((pl.ds(r + dil * BLOCK * c, BLOCK, stride=dil), c))
            kb = [_bf(k_ref[0, rows, :]) for rows, _ in tiles]
            vb = [_bf(v_ref[0, rows, :]) for rows, _ in tiles]
            if has_prev:
                r0, c0 = idx0 // nc, idx0 % nc
                prows = pl.ds(r0 + dil * BLOCK * jnp.maximum(c0 - 1, 0), BLOCK, stride=dil)
                kb = [_bf(k_ref[0, prows, :])] + kb
                vb = [_bf(v_ref[0, prows, :])] + vb
            loads = []
            for u, (rows, c) in enumerate(tiles):
                q = _bf(stack_heads(q_ref[0, rows, :]))
                if has_prev:
                    k = jnp.concatenate([kb[u], kb[u + 1]], axis=0)
                    v = jnp.concatenate([vb[u], vb[u + 1]], axis=0)
                else:
                    k, v = kb[u], vb[u]
                loads.append((rows, c, q, k, v))
            ss = [lax.dot_general(q, k, (((1,), (1,)), ((), ())), preferred_element_type=F32) + bias
                  for _, _, q, k, _ in loads]
            if has_prev:
                ss = [jnp.where(band & (ki >= jnp.where(c > 0, 0, BLOCK)), s, NEG_BIG)
                      for s, (_, c, _, _, _) in zip(ss, loads)]
            else:
                ss = [jnp.where(band, s, NEG_BIG) for s in ss]
            ms = [jnp.max(s, axis=1, keepdims=True) for s in ss]
            ps = [jnp.exp2(s - m) for s, m in zip(ss, ms)]
            ls = [jnp.sum(p, axis=1, keepdims=True) for p in ps]
            os_ = [jnp.dot(_bf(p), v, preferred_element_type=F32) for p, (_, _, _, _, v) in zip(ps, loads)]
            for (rows, _, _, _, _), m, l, o in zip(loads, ms, ls, os_):
                acc_scr[pi, rows, :] = unstack(o)
                m_scr[pi, rows, :] = unstack(jnp.broadcast_to(m, (2 * BLOCK, LANES)))
                l_scr[pi, rows, :] = unstack(jnp.broadcast_to(l, (2 * BLOCK, LANES)))
            return carry

        lax.fori_loop(0, s_len // BLOCK // DIL_TILES_PER_STEP, body, 0)

    tm = 2 * BLOCK

    def merge(i, carry):
        rows = pl.ds(pl.multiple_of(i * tm, tm), tm)
        m1, m2, m3 = m_scr[0, rows, :], m_scr[1, rows, :], m_scr[2, rows, :]
        m = jnp.maximum(jnp.maximum(m1, m2), m3)
        e1, e2, e3 = jnp.exp2(m1 - m), jnp.exp2(m2 - m), jnp.exp2(m3 - m)
        num = e1 * acc_scr[0, rows, :] + e2 * acc_scr[1, rows, :] + e3 * acc_scr[2, rows, :]
        den = e1 * l_scr[0, rows, :] + e2 * l_scr[1, rows, :] + e3 * l_scr[2, rows, :]
        y_ref[0, rows, :] = (num / den).astype(y_ref.dtype)
        return carry

    lax.fori_loop(0, s_len // tm, merge, 0)


def _dilated_attention(qkv_c, biases):
    bsz, s, _ = qkv_c.shape
    nhp = GROUP_WIDTH // LANES
    bspec = pl.BlockSpec((1, 2 * BLOCK, 2 * BLOCK), lambda b, h: (h, 0, 0))
    scr = pltpu.VMEM((len(DILATIONS), s, LANES), F32)
    return pl.pallas_call(
        _dil_kernel,
        grid=(bsz, nhp),
        in_specs=[
            pl.BlockSpec((1, s, LANES), lambda b, h: (b, 0, h)),
            pl.BlockSpec((1, s, LANES), lambda b, h: (b, 0, nhp + h)),
            pl.BlockSpec((1, s, LANES), lambda b, h: (b, 0, 2 * nhp + h)),
            bspec, bspec, bspec,
        ],
        out_specs=pl.BlockSpec((1, s, LANES), lambda b, h: (b, 0, h)),
        out_shape=jax.ShapeDtypeStruct((bsz, s, GROUP_WIDTH), BF16),
        scratch_shapes=[scr, scr, scr],
        compiler_params=_params(("parallel", "parallel")),
        name="dilated_attention",
    )(qkv_c, qkv_c, qkv_c, *biases)


def _unit_lower_inverses(ns):
    size_all = 2 * CHUNK
    row = _iota2((size_all, size_all), 0)
    col = _iota2((size_all, size_all), 1)
    eye = jnp.where(row == col, 1.0, 0.0)
    base = 8
    same8 = (row >> 3) == (col >> 3)
    n8 = [jnp.where(same8, n, 0.0) for n in ns]
    n2 = [_mm(a, a) for a in n8]
    n4 = [_mm(a, a) for a in n2]
    ts = [eye + a for a in n8]
    ts = [t + _mm(t, b) for t, b in zip(ts, n2)]
    ts = [t + _mm(t, b) for t, b in zip(ts, n4)]
    size = 2 * base
    while size <= CHUNK:
        half = size // 2
        shift = size.bit_length() - 1
        off = (((row >> shift) == (col >> shift)) & ((row & (size - 1)) >= half)
               & ((col & (size - 1)) < half))
        us = [_mm(t, jnp.where(off, n, 0.0)) for t, n in zip(ts, ns)]
        ts = [t + _mm(u, t) for t, u in zip(ts, us)]
        size *= 2
    return ts


def _rwkv_kernel(*refs, has_vres):
    if has_vres:
        (pa_ref, mu_ref, w0_ref, wup_ref, a0_ref, aup_ref, gup_ref, kk_ref, ka_ref, rk_ref,
         lnw_ref, lnb_ref, seg_ref, cum_ref, vf_ref, vup_ref, vb_ref,
         y_ref, prev_scr, s_scr) = refs
    else:
        (pa_ref, mu_ref, w0_ref, wup_ref, a0_ref, aup_ref, gup_ref, kk_ref, ka_ref, rk_ref,
         lnw_ref, lnb_ref, seg_ref, cum_ref,
         y_ref, vf_out_ref, prev_scr, s_scr) = refs
    c = GROUP_WIDTH
    tt = pa_ref.shape[1]

    @pl.when(pl.program_id(1) == 0)
    def _():
        prev_scr[...] = jnp.zeros_like(prev_scr)
        s_scr[...] = jnp.zeros_like(s_scr)

    seg = seg_ref[...]

    def seg_sum(t):
        return jnp.concatenate([jnp.dot(_bf(t[:, p * LANES:(p + 1) * LANES]), seg, preferred_element_type=F32)
                                for p in range(c // LANES)], axis=1)

    pa = pa_ref[0]
    rolled = pltpu.roll(pa, 1, 0)
    first = _iota2(pa.shape, 0) == 0
    prev = jnp.where(first, prev_scr[...], rolled)
    prev_scr[...] = pa[tt - 1:tt, :]
    sh = pa + (prev - pa) * mu_ref[...]
    r = sh[:, 0:c]
    k = sh[:, c:2 * c]
    v = sh[:, 2 * c:3 * c]
    w_lo = sh[:, A_OFF_W:A_OFF_W + LORA_PAD]
    a_lo = sh[:, A_OFF_A:A_OFF_A + LORA_PAD]
    g_lo = sh[:, A_OFF_G:A_OFF_G + G_PAD]
    if has_vres:
        pv = sh[:, A_VRES_TILE:A_VRES_TILE + LANES]
        v = v + (vf_ref[0] - v) * _sigmoid(vb_ref[...] + _mm(pv, vup_ref[...]))
    else:
        vf_out_ref[0] = v

    w_log = -_softplus(-(w0_ref[...] + _mm(jnp.tanh(w_lo), wup_ref[...]))) - 0.5
    lw = -jnp.exp(w_log)
    a = _sigmoid(a0_ref[...] + _mm(a_lo, aup_ref[...]))
    g = _mm(_sigmoid(g_lo), gup_ref[...])
    kkr = k * kk_ref[...]
    kk = kkr * lax.rsqrt(jnp.maximum(seg_sum(kkr * kkr), 1e-24))
    k = k * (1.0 + (a - 1.0) * ka_ref[...])

    cum = _mm_split_rhs(cum_ref[...], lw, 2)
    eg = jnp.exp(cum)
    eig = jnp.exp(-cum)
    at = -kk * jnp.exp(cum - lw)
    rt = r * eg
    bt = kk * a * eig
    kt = k * eig

    pair_rows = 2 * CHUNK
    first_head = _iota2((CHUNK, LANES), 1) < HEAD_DIM
    row = _iota2((pair_rows, pair_rows), 0)
    col = _iota2((pair_rows, pair_rows), 1)
    strict = (row & (CHUNK - 1)) > (col & (CHUNK - 1))
    lower = (row & (CHUNK - 1)) >= (col & (CHUNK - 1))
    eye = row == col

    def stack(t):
        return jnp.concatenate([jnp.where(first_head, t, 0.0), jnp.where(first_head, 0.0, t)], axis=0)

    probs = [(ci, p) for ci in range(tt // CHUNK) for p in range(c // LANES)]
    cut = lambda t, ci, p: t[ci * CHUNK:(ci + 1) * CHUNK, p * LANES:(p + 1) * LANES]
    g_end = [cut(eg, ci, p)[CHUNK - 1:CHUNK, :] for ci, p in probs]
    at_p = [stack(cut(at, ci, p)) for ci, p in probs]
    rt_p = [stack(cut(rt, ci, p)) for ci, p in probs]
    bt_p = [stack(cut(bt, ci, p)) for ci, p in probs]
    kt_p = [stack(cut(kt, ci, p)) for ci, p in probs]
    v_p = [stack(cut(v, ci, p)) for ci, p in probs]
    bg_p = [b_ * ge for b_, ge in zip(bt_p, g_end)]
    kg_p = [k_ * ge for k_, ge in zip(kt_p, g_end)]
    grams = [_mm_nt(jnp.concatenate([a_, r_], axis=0), jnp.concatenate([b_, k_], axis=0))
             for a_, r_, b_, k_ in zip(at_p, rt_p, bt_p, kt_p)]
    a_ab = [jnp.where(strict, gm[:pair_rows, :pair_rows], 0.0) for gm in grams]
    a_ak = [jnp.where(strict, gm[:pair_rows, pair_rows:], 0.0) for gm in grams]
    a_rb = [jnp.where(lower, gm[pair_rows:, :pair_rows], 0.0) for gm in grams]
    a_rk = [jnp.where(lower, gm[pair_rows:, pair_rows:], 0.0) for gm in grams]
    both_v = [_mm(jnp.concatenate([a_, r_], axis=0), v_) for a_, r_, v_ in zip(a_ak, a_rk, v_p)]
    akv = [x_[:pair_rows] for x_ in both_v]
    rkv = [x_[pair_rows:] for x_ in both_v]
    ts = _unit_lower_inverses(a_ab)
    pq = [_mm(t, jnp.concatenate([a_, x_], axis=1)) for t, a_, x_ in zip(ts, at_p, akv)]
    ps = [x_[:, :LANES] for x_ in pq]
    qs = [x_[:, LANES:] for x_ in pq]
    rb_pq = [_mm(a_, x_) for a_, x_ in zip(a_rb, pq)]
    r2 = [r_ + x_[:, :LANES] for r_, x_ in zip(rt_p, rb_pq)]
    y0 = [x_[:, LANES:] + k_ for x_, k_ in zip(rb_pq, rkv)]
    ms = [jnp.where(eye, jnp.broadcast_to(ge, (pair_rows, LANES)), 0.0) + _mm_tn(p_, b_)
          for ge, p_, b_ in zip(g_end, ps, bg_p)]
    zs = [_mm_tn(jnp.concatenate([q_, v_], axis=0), jnp.concatenate([b_, k_], axis=0))
          for q_, b_, v_, k_ in zip(qs, bg_p, v_p, kg_p)]

    state = [s_scr[p] for p in range(c // LANES)]
    y_rows = []
    for ci in range(tt // CHUNK):
        y_pairs = []
        for p in range(c // LANES):
            i = ci * (c // LANES) + p
            y_st = _mm_nt(r2[i], state[p]) + y0[i]
            y_pairs.append(y_st[:CHUNK] + y_st[CHUNK:])
            state[p] = _mm(state[p], ms[i]) + zs[i]
        y_rows.append(jnp.concatenate(y_pairs, axis=1))
    for p in range(c // LANES):
        s_scr[p] = state[p]
    y = jnp.concatenate(y_rows, axis=0)

    mu = seg_sum(y) * (1.0 / HEAD_DIM)
    d = y - mu
    var = seg_sum(d * d) * (1.0 / HEAD_DIM)
    yn = d * lax.rsqrt(var + RWKV_LN_EPS) * lnw_ref[...] + lnb_ref[...]
    bonus = seg_sum(r * k * rk_ref[...]) * v
    y_ref[0] = ((yn + bonus) * g).astype(y_ref.dtype)


def _rwkv7(pa, prm, v_first):
    bsz, s, _ = pa.shape
    na = prm["mu"].shape[1]
    c = GROUP_WIDTH
    tt = RWKV_TILE
    has_vres = v_first is not None
    full = lambda shape: pl.BlockSpec(shape, lambda b, t: (0,) * len(shape))
    tile = lambda w: pl.BlockSpec((1, tt, w), lambda b, t: (b, t, 0))
    idx = jnp.arange(tt)
    cum_ones = ((idx[:, None] >= idx[None, :]) & (idx[:, None] // CHUNK == idx[None, :] // CHUNK)).astype(BF16)
    in_specs = [tile(na), full((1, na)), full((1, c)), full((LORA_PAD, c)), full((1, c)),
                full((LORA_PAD, c)), full((G_PAD, c)), full((1, c)), full((1, c)), full((1, c)),
                full((1, c)), full((1, c)), full((LANES, LANES)), full((tt, tt))]
    args = [pa, prm["mu"], prm["w0"], prm["w_up"], prm["a0"], prm["a_up"], prm["g_up"],
            prm["k_k"], prm["k_a"], prm["r_k"], prm["ln_w"], prm["ln_b"], prm["seg"], cum_ones]
    y_shape = jax.ShapeDtypeStruct((bsz, s, c), BF16)
    if has_vres:
        in_specs += [tile(c), full((LORA_PAD, c)), full((1, c))]
        args += [v_first, prm["vres_up"], prm["vres_bias"]]
        out_specs, out_shape = tile(c), y_shape
    else:
        out_specs = [tile(c), tile(c)]
        out_shape = [y_shape, jax.ShapeDtypeStruct((bsz, s, c), F32)]
    res = pl.pallas_call(
        functools.partial(_rwkv_kernel, has_vres=has_vres),
        grid=(bsz, s // tt),
        in_specs=in_specs,
        out_specs=out_specs,
        out_shape=out_shape,
        scratch_shapes=[pltpu.VMEM((1, na), F32), pltpu.VMEM((c // LANES, LANES, LANES), F32)],
        compiler_params=_params(("parallel", "arbitrary")),
        name="rwkv7_vres" if has_vres else "rwkv7",
    )(*args)
    if has_vres:
        return res, v_first
    return res[0], res[1]


def _rel_bucket(dist):
    max_exact = NUM_BUCKETS // 2
    d_f = jnp.maximum(dist, 1).astype(F32)
    large = max_exact + (jnp.log(d_f / max_exact) / math.log(MAX_DISTANCE / max_exact)
                         * (NUM_BUCKETS - max_exact)).astype(jnp.int32)
    large = jnp.minimum(large, NUM_BUCKETS - 1)
    return jnp.where(dist < max_exact, dist, large)


def _toeplitz(u, n_rows, n_cols):
    length = n_rows + n_cols - 1
    g = jnp.roll(jnp.flip(u, axis=-1), -(n_rows - 1), axis=-1)
    flat = jnp.tile(g, (1,) * (u.ndim - 1) + (n_rows,))[..., :n_rows * (length - 1)]
    return flat.reshape(u.shape[:-1] + (n_rows, length - 1))[..., :n_cols]


def _diff_bias_vectors(rel_tab, ns):
    n = jnp.arange(2 * SPAN)
    rel = jnp.where(n < SPAN, -n, 2 * SPAN - n)
    dist = jnp.arange(ns)[:, None] * SPAN + rel[None, :]
    u = rel_tab[_rel_bucket(jnp.maximum(dist, 0))] * LOG2E
    u = jnp.transpose(u, (2, 0, 1)).astype(F32)
    return jnp.broadcast_to(u[:, :, None, :], (H_D, ns, SUBLANES, 2 * SPAN))


def _band_bias(rel_tab, dil):
    steps = jnp.arange(3 * BLOCK - 1) - (BLOCK - 1)
    u = rel_tab[_rel_bucket(jnp.maximum(steps, 0) * dil)] * LOG2E
    t = _toeplitz(jnp.transpose(u, (1, 0)).astype(F32), BLOCK, 2 * BLOCK)
    return t.reshape(N_HEADS * HEAD_DIM // LANES, 2 * BLOCK, 2 * BLOCK)


def _pad_axis(t, size, axis):
    pad = [(0, 0)] * t.ndim
    pad[axis] = (0, size - t.shape[axis])
    return jnp.pad(t, pad)


def _layout_mixer_a(t, vres=None):
    c = GROUP_WIDTH
    o_w, o_a, o_g = 3 * c, 3 * c + W_LORA, 3 * c + W_LORA + A_LORA
    gate = t[..., o_g:N_A_COLS]
    if vres is not None:
        gate = jnp.concatenate([gate, vres], axis=-1)
    parts = [t[..., :o_w], _pad_axis(t[..., o_w:o_a], LORA_PAD, -1),
             _pad_axis(t[..., o_a:o_g], LORA_PAD, -1), _pad_axis(gate, G_PAD, -1)]
    return jnp.concatenate(parts, axis=-1)


def _scale_queries(w_qkv, scale):
    return jnp.concatenate([w_qkv[:, :GROUP_WIDTH] * scale, w_qkv[:, GROUP_WIDTH:]], axis=1)


def kernel(x, c, w_ada, b_ada, norm_gain, w_in, w_out, rel_bias, rwkv_mu, rwkv_w0, rwkv_w_up,
           rwkv_a0, rwkv_a_up, rwkv_g_up, rwkv_k_k, rwkv_k_a, rwkv_r_k, rwkv_ln_w, rwkv_ln_b,
           vres_down, vres_mu, vres_up, vres_bias, diff_lambda, diff_subln, ffn_w13, ffn_w2,
           final_gain):
    out_dtype = x.dtype
    bsz, s, d = x.shape
    cw = GROUP_WIDTH
    x = x.astype(F32)
    row = lambda t: t.reshape(1, -1).astype(F32)

    mod = _modulation(c.astype(F32), w_ada, b_ada)
    rel_tab_c = rel_bias[:, :N_HEADS].astype(F32)
    rel_tab_d = rel_bias[:, N_HEADS:].astype(F32)
    band_biases = [_band_bias(rel_tab_c, dil) for _, dil in DILATIONS]
    diff_bias = _diff_bias_vectors(rel_tab_d, s // SPAN)
    lane = jnp.arange(LANES)
    seg = (lane[:, None] // HEAD_DIM == lane[None, :] // HEAD_DIM).astype(BF16)
    logit_scale = LOG2E * HEAD_DIM ** -0.5

    v_first = None
    for l in range(DEPTH):
        sh1, sc1, g1, sh2, sc2, g2 = [mod[l, :, None, i * d:(i + 1) * d] for i in range(6)]
        gain1, gain2 = row(norm_gain[l, 0]), row(norm_gain[l, 1])
        vres_w = vres_down[l - 1] if l > 0 else None
        w_l = w_in[l].astype(F32)
        w_a = _layout_mixer_a(w_l[:, :N_A_COLS], vres_w)
        o_b, o_c, o_d = N_A_COLS, N_A_COLS + 3 * cw, N_A_COLS + 6 * cw
        w_all = _bf(jnp.concatenate([w_a, _scale_queries(w_l[:, o_b:o_c], logit_scale),
                                     _scale_queries(w_l[:, o_d:N_IN], logit_scale),
                                     _scale_queries(w_l[:, o_c:o_d], logit_scale)], axis=1))
        mu = _layout_mixer_a(rwkv_mu[l], vres_mu[l - 1] if l > 0 else None)

        pa, qkv_bd, qkv_c = _norm_proj(x, gain1, sc1, sh1, w_all, (NA, 6 * cw, 3 * cw), (F32, BF16, F32))

        prm = {
            "mu": row(mu), "w0": row(rwkv_w0[l]), "a0": row(rwkv_a0[l]),
            "w_up": _bf(_pad_axis(rwkv_w_up[l], LORA_PAD, 0)),
            "a_up": _bf(_pad_axis(rwkv_a_up[l], LORA_PAD, 0)),
            "g_up": _bf(_pad_axis(rwkv_g_up[l], G_PAD, 0)),
            "k_k": row(rwkv_k_k[l]), "k_a": row(rwkv_k_a[l]), "r_k": row(rwkv_r_k[l]),
            "ln_w": row(rwkv_ln_w[l]), "ln_b": row(rwkv_ln_b[l]), "seg": seg,
        }
        if l > 0:
            prm["vres_up"] = _bf(jnp.pad(vres_up[l - 1], ((VRES_LANE0, LANES - VRES_LANE0 - V_LORA), (0, 0))))
            prm["vres_bias"] = row(vres_bias[l - 1])
        y_a, v_first = _rwkv7(pa, prm, v_first)

        y_b = _stick_breaking(qkv_bd, 0)
        y_c = _dilated_attention(qkv_c, band_biases)
        lam_init = 0.8 - 0.6 * math.exp(-0.3 * l)
        y_d = _diff_attention(qkv_bd, 3 * cw // LANES, diff_bias, diff_lambda[l].astype(F32),
                              row(diff_subln[l]), lam_init)

        x = _out_proj(y_a, y_b, y_c, y_d, _bf(w_out[l]), x, g1)
        x = _ffn(x, gain2, sc2, sh2, g2, _bf(_interleave_gate_up(ffn_w13[l])), _bf(ffn_w2[l]), row(final_gain),
                 final=(l == DEPTH - 1))
    return x.astype(out_dtype)
```

```python
import functools
import math

import jax
import jax.numpy as jnp
from jax import lax
from jax.experimental import pallas as pl
from jax.experimental.pallas import tpu as pltpu

F32 = jnp.float32
BF16 = jnp.bfloat16

D_MODEL = 2048
DEPTH = 2
HEAD_DIM = 64
GROUP_WIDTH = D_MODEL // 4
N_HEADS = GROUP_WIDTH // HEAD_DIM
DIFF_HALF = HEAD_DIM
DIFF_VDIM = 2 * DIFF_HALF
H_D = GROUP_WIDTH // DIFF_VDIM
W_LORA = max(32, int(round(D_MODEL ** 0.5 * 1.8 / 32)) * 32)
A_LORA = max(32, int(round(D_MODEL ** 0.5 * 1.8 / 32)) * 32)
V_LORA = max(32, int(round(D_MODEL ** 0.5 * 1.3 / 32)) * 32)
G_LORA = max(32, int(round(D_MODEL ** 0.8 / 32)) * 32)
N_A_COLS = 3 * GROUP_WIDTH + W_LORA + A_LORA + G_LORA
N_IN = N_A_COLS + 9 * GROUP_WIDTH
D_FF = ((8 * D_MODEL + 3 * 256 - 1) // (3 * 256)) * 256
DILATIONS = ((128, 1), (512, 4), (2048, 16))
BLOCK = 128
NUM_BUCKETS = 32
MAX_DISTANCE = 2048
NORM_EPS = 1e-6
RWKV_LN_EPS = 64e-5
SUBLN_EPS = 1e-5
LOG2E = math.log2(math.e)

LANES = 128
SUBLANES = 8
VMEM_LIMIT_BYTES = 56 * 1024 * 1024

LORA_PAD = 128
G_PAD = 512
A_OFF_W = 3 * GROUP_WIDTH
A_OFF_A = A_OFF_W + LORA_PAD
A_OFF_G = A_OFF_A + LORA_PAD
NA = A_OFF_G + G_PAD
A_VRES_TILE = A_OFF_G + (G_LORA // LANES) * LANES
VRES_LANE0 = G_LORA % LANES
assert G_LORA + V_LORA <= G_PAD and VRES_LANE0 + V_LORA <= LANES
PROJ_TN = 768
FIRST_STEP_ROWS = 256
CHUNK = 64
RWKV_TILE = 256
SPAN_BLOCKS = 4
SPAN = SPAN_BLOCKS * BLOCK
DIL_TILES_PER_STEP = 4
NEG_BIG = -1e30
SB_ROW_GROUPS = 2
SB_UNDERFLOW_LOG2 = 160.0


def _bf(x):
    return x.astype(BF16)


def _mm(a, b):
    return jnp.dot(_bf(a), _bf(b), preferred_element_type=F32)


def _mm_nt(a, b):
    return lax.dot_general(_bf(a), _bf(b), (((1,), (1,)), ((), ())), preferred_element_type=F32)


def _mm_tn(a, b):
    return lax.dot_general(_bf(a), _bf(b), (((0,), (0,)), ((), ())), preferred_element_type=F32)


def _split_bf16(x, n):
    parts = []
    rem = x
    for i in range(n):
        p = _bf(rem)
        parts.append(p)
        if i + 1 < n:
            rem = rem - p.astype(F32)
    return parts


def _mm_split_lhs(a, b_bf16, n):
    out = None
    for p in _split_bf16(a, n):
        t = jnp.dot(p, b_bf16, preferred_element_type=F32)
        out = t if out is None else out + t
    return out


def _mm_split_rhs(a_bf16, b, n):
    out = None
    for p in _split_bf16(b, n):
        t = jnp.dot(a_bf16, p, preferred_element_type=F32)
        out = t if out is None else out + t
    return out


def _softplus(x):
    return jnp.maximum(x, 0.0) + jnp.log1p(jnp.exp(-jnp.abs(x)))


def _softplus2(x):
    sign_bit = jnp.uint32(0x80000000)
    neg_abs = lax.bitcast_convert_type(lax.bitcast_convert_type(x, jnp.uint32) | sign_bit, F32)
    return jnp.maximum(x, 0.0) + jnp.log2(1.0 + jnp.exp2(neg_abs))


def _sigmoid(x):
    return jax.nn.sigmoid(x)


def _iota2(shape, dim):
    return lax.broadcasted_iota(jnp.int32, shape, dim)


def _params(sem, vmem=VMEM_LIMIT_BYTES):
    return pltpu.CompilerParams(dimension_semantics=sem, vmem_limit_bytes=vmem)


def _mod_kernel(c_ref, w_ref, b_ref, o_ref):
    c = c_ref[...]
    cond = c * _sigmoid(c)
    o_ref[0] = _mm(cond, w_ref[0]) + b_ref[0]


def _modulation(c, w_ada, b_ada):
    depth, d, n = w_ada.shape
    bsz = c.shape[0]
    tn = 1024
    return pl.pallas_call(
        _mod_kernel,
        grid=(depth, n // tn),
        in_specs=[
            pl.BlockSpec((bsz, d), lambda l, j: (0, 0)),
            pl.BlockSpec((1, d, tn), lambda l, j: (l, 0, j)),
            pl.BlockSpec((1, 1, tn), lambda l, j: (l, 0, j)),
        ],
        out_specs=pl.BlockSpec((1, bsz, tn), lambda l, j: (l, 0, j)),
        out_shape=jax.ShapeDtypeStruct((depth, bsz, n), F32),
        compiler_params=_params(("parallel", "parallel")),
        name="adaln_mod",
    )(c, w_ada, b_ada.reshape(depth, 1, n))


def _norm_mod(x, gain, sc, sh):
    ms = jnp.mean(x * x, axis=-1, keepdims=True)
    return x * lax.rsqrt(ms + NORM_EPS) * gain * (1.0 + sc) + sh


def _proj_kernel(x_ref, gain_ref, sc_ref, sh_ref, w_ref, *rest, splits):
    o_refs, h_scr = rest[:-1], rest[-1]
    j = pl.program_id(2)
    tm = h_scr.shape[0]

    @pl.when(j == 0)
    def _():
        for g in range(tm // FIRST_STEP_ROWS):
            rows = slice(g * FIRST_STEP_ROWS, (g + 1) * FIRST_STEP_ROWS)
            h = _bf(_norm_mod(x_ref[0, rows, :], gain_ref[...], sc_ref[0], sh_ref[0]))
            h_scr[rows, :] = h
            o_refs[0][0, rows, :] = jnp.dot(h, w_ref[...], preferred_element_type=F32).astype(o_refs[0].dtype)

    lo = 0
    for o_ref, n_tiles in zip(o_refs, splits):
        @pl.when((j >= max(lo, 1)) & (j < lo + n_tiles))
        def _(o_ref=o_ref):
            o_ref[0] = jnp.dot(h_scr[...], w_ref[...], preferred_element_type=F32).astype(o_ref.dtype)
        lo += n_tiles


def _norm_proj(x, gain, sc, sh, w, widths, dtypes):
    bsz, s, d = x.shape
    tm, tn = 1024, PROJ_TN
    splits = tuple(n // tn for n in widths)
    starts = [sum(splits[:k]) for k in range(len(splits))]

    def out_spec(k):
        return pl.BlockSpec((1, tm, tn),
                            lambda b, i, j: (b, i, jnp.clip(j - starts[k], 0, splits[k] - 1)))

    return pl.pallas_call(
        functools.partial(_proj_kernel, splits=splits),
        grid=(bsz, s // tm, sum(splits)),
        in_specs=[
            pl.BlockSpec((1, tm, d), lambda b, i, j: (b, i, 0)),
            pl.BlockSpec((1, d), lambda b, i, j: (0, 0)),
            pl.BlockSpec((1, 1, d), lambda b, i, j: (b, 0, 0)),
            pl.BlockSpec((1, 1, d), lambda b, i, j: (b, 0, 0)),
            pl.BlockSpec((d, tn), lambda b, i, j: (0, j)),
        ],
        out_specs=[out_spec(k) for k in range(len(widths))],
        out_shape=[jax.ShapeDtypeStruct((bsz, s, n), dt) for n, dt in zip(widths, dtypes)],
        scratch_shapes=[pltpu.VMEM((tm, d), BF16)],
        compiler_params=_params(("parallel", "parallel", "arbitrary")),
        name="in_proj",
    )(x, gain, sc, sh, w)


def _outproj_kernel(ya_ref, yb_ref, yc_ref, yd_ref, w_ref, x_ref, g_ref, o_ref):
    c = GROUP_WIDTH
    acc = jnp.dot(ya_ref[0], w_ref[0:c, :], preferred_element_type=F32)
    acc += jnp.dot(yb_ref[0], w_ref[c:2 * c, :], preferred_element_type=F32)
    acc += jnp.dot(yc_ref[0], w_ref[2 * c:3 * c, :], preferred_element_type=F32)
    acc += jnp.dot(yd_ref[0], w_ref[3 * c:4 * c, :], preferred_element_type=F32)
    o_ref[0] = x_ref[0] + g_ref[0] * acc


def _out_proj(ya, yb, yc, yd, w, x, g1):
    bsz, s, d = x.shape
    c = GROUP_WIDTH
    tm, tn = 512, d
    yspec = pl.BlockSpec((1, tm, c), lambda b, i, j: (b, i, 0))
    return pl.pallas_call(
        _outproj_kernel,
        grid=(bsz, s // tm, d // tn),
        in_specs=[
            yspec, yspec, yspec, yspec,
            pl.BlockSpec((4 * c, tn), lambda b, i, j: (0, j)),
            pl.BlockSpec((1, tm, tn), lambda b, i, j: (b, i, j)),
            pl.BlockSpec((1, 1, tn), lambda b, i, j: (b, 0, j)),
        ],
        out_specs=pl.BlockSpec((1, tm, tn), lambda b, i, j: (b, i, j)),
        out_shape=jax.ShapeDtypeStruct((bsz, s, d), F32),
        compiler_params=_params(("parallel", "parallel", "arbitrary")),
        name="out_proj",
    )(ya, yb, yc, yd, w, x, g1)


def _ffn_kernel(x_ref, gain_ref, sc_ref, sh_ref, g_ref, w1_ref, w3_ref, w2_ref, fg_ref,
                o_ref, h_scr, *, final):
    f = pl.program_id(2)
    tm = h_scr.shape[0]
    acc_scr = o_ref.at[0]

    def chunk(h):
        gate = jnp.dot(h, w1_ref[...], preferred_element_type=F32)
        up = jnp.dot(h, w3_ref[...], preferred_element_type=F32)
        act = gate * _sigmoid(gate) * up
        return jnp.dot(_bf(act), w2_ref[...], preferred_element_type=F32)

    @pl.when(f == 0)
    def _():
        for g in range(tm // FIRST_STEP_ROWS):
            rows = slice(g * FIRST_STEP_ROWS, (g + 1) * FIRST_STEP_ROWS)
            h = _bf(_norm_mod(x_ref[0, rows, :], gain_ref[...], sc_ref[0], sh_ref[0]))
            h_scr[rows, :] = h
            acc_scr[rows, :] = chunk(h)

    @pl.when(f > 0)
    def _():
        acc_scr[...] += chunk(h_scr[...])

    @pl.when(f == pl.num_programs(2) - 1)
    def _():
        xn = x_ref[0] + g_ref[0] * acc_scr[...]
        if final:
            ms = jnp.mean(xn * xn, axis=-1, keepdims=True)
            xn = xn * lax.rsqrt(ms + NORM_EPS) * fg_ref[...]
        o_ref[0] = xn


def _ffn(x, gain, sc, sh, g2, w13, w2, final_gain, final):
    bsz, s, d = x.shape
    dff = w2.shape[0]
    tm, tf = 512, 512
    nf = dff // tf
    vec = pl.BlockSpec((1, 1, d), lambda b, i, f: (b, 0, 0))
    return pl.pallas_call(
        functools.partial(_ffn_kernel, final=final),
        grid=(bsz, s // tm, nf),
        in_specs=[
            pl.BlockSpec((1, tm, d), lambda b, i, f: (b, i, 0)),
            pl.BlockSpec((1, d), lambda b, i, f: (0, 0)),
            vec, vec, vec,
            pl.BlockSpec((d, tf), lambda b, i, f: (0, f)),
            pl.BlockSpec((d, tf), lambda b, i, f: (0, f + nf)),
            pl.BlockSpec((tf, d), lambda b, i, f: (f, 0)),
            pl.BlockSpec((1, d), lambda b, i, f: (0, 0)),
        ],
        out_specs=pl.BlockSpec((1, tm, d), lambda b, i, f: (b, i, 0)),
        out_shape=jax.ShapeDtypeStruct((bsz, s, d), F32),
        scratch_shapes=[pltpu.VMEM((tm, d), BF16)],
        compiler_params=_params(("parallel", "parallel", "arbitrary")),
        name="ffn_final" if final else "ffn",
    )(x, gain, sc, sh, g2, w13, w13, w2, final_gain)


def _sb_kernel(q_ref, k_ref, v_ref, later_ref, o_ref):
    qs = pl.program_id(2)
    nh = LANES // HEAD_DIM
    later = later_ref[...]
    row = _iota2((SPAN, BLOCK), 0)
    col = _iota2((SPAN, BLOCK), 1)

    def span_tile(q, sl, span, run, masked):
        st = pl.multiple_of(span * SPAN, SPAN)
        n_rows = q.shape[0]
        z = lax.dot_general(q, k_ref[0, pl.ds(st, SPAN), sl], (((1,), (1,)), ((), ())),
                            preferred_element_type=F32)
        subs = []
        for j in range(SPAN_BLOCKS):
            zj = z[:, j * BLOCK:(j + 1) * BLOCK]
            cost = _softplus2(zj)
            mask = None
            if masked:
                mask = (j * BLOCK + col) < row
                cost = jnp.where(mask, cost, 0.0)
            subs.append((zj, cost, mask))
        stacked = jnp.concatenate([_bf(cost) for _, cost, _ in subs], axis=0)
        within = jnp.dot(stacked, later, preferred_element_type=F32)
        atts = [None] * SPAN_BLOCKS
        for j in reversed(range(SPAN_BLOCKS)):
            zj, cost, mask = subs[j]
            w = within[j * n_rows:(j + 1) * n_rows]
            att = jnp.exp2(zj - cost - w - run)
            if masked:
                att = jnp.where(mask, att, 0.0)
            atts[j] = _bf(att)
            run = run + jnp.sum(cost, axis=1, keepdims=True)
        contrib = jnp.dot(jnp.concatenate(atts, axis=1), v_ref[0, pl.ds(st, SPAN), sl],
                          preferred_element_type=F32)
        return contrib, run

    lanes = [slice(h * HEAD_DIM, (h + 1) * HEAD_DIM) for h in range(nh)]

    def step(span, rows, carry, masked):
        out = []
        for sl, (acc, run) in zip(lanes, carry):
            contrib, run = span_tile(q_ref[0, rows, sl], sl, span, run, masked)
            out.append((acc + contrib, run))
        return tuple(out)

    init = tuple((jnp.zeros((SPAN, HEAD_DIM), F32), jnp.zeros((SPAN, 1), F32)) for _ in range(nh))
    diag = step(qs, slice(None), init, True)

    group = SPAN // SB_ROW_GROUPS
    outs = [[] for _ in range(nh)]
    for g in range(SB_ROW_GROUPS):
        rows = slice(g * group, (g + 1) * group)

        def more(state):
            i, carry = state
            least = functools.reduce(jnp.minimum, [jnp.min(run) for _, run in carry])
            return (i < qs) & (least < SB_UNDERFLOW_LOG2)

        def body(state, rows=rows):
            i, carry = state
            return i + 1, step(qs - 1 - i, rows, carry, False)

        start = tuple((acc[rows], run[rows]) for acc, run in diag)
        _, carry = lax.while_loop(more, body, (jnp.int32(0), start))
        for h, (acc, _) in enumerate(carry):
            outs[h].append(acc)
    for sl, parts in zip(lanes, outs):
        o_ref[0, :, sl] = jnp.concatenate(parts, axis=0).astype(o_ref.dtype)


def _stick_breaking(qkv, col0):
    bsz, s, _ = qkv.shape
    nhp = GROUP_WIDTH // LANES
    later = jnp.tril(jnp.ones((BLOCK, BLOCK), F32), -1).astype(BF16)
    return pl.pallas_call(
        _sb_kernel,
        grid=(bsz, nhp, s // SPAN),
        in_specs=[
            pl.BlockSpec((1, SPAN, LANES), lambda b, h, i: (b, i, col0 + h)),
            pl.BlockSpec((1, s, LANES), lambda b, h, i: (b, 0, col0 + nhp + h)),
            pl.BlockSpec((1, s, LANES), lambda b, h, i: (b, 0, col0 + 2 * nhp + h)),
            pl.BlockSpec((BLOCK, BLOCK), lambda b, h, i: (0, 0)),
        ],
        out_specs=pl.BlockSpec((1, SPAN, LANES), lambda b, h, i: (b, i, h)),
        out_shape=jax.ShapeDtypeStruct((bsz, s, GROUP_WIDTH), BF16),
        compiler_params=_params(("parallel", "parallel", "arbitrary")),
        name="stick_breaking",
    )(qkv, qkv, qkv, later)


def _diff_kernel(q_ref, k_ref, v_ref, vec_ref, lam_ref, subln_ref, o_ref, bias_scr, *, lam_init):
    qs = pl.program_id(2)
    causal = _iota2((SPAN, SPAN), 1) <= _iota2((SPAN, SPAN), 0)
    halves = [slice(c * DIFF_HALF, (c + 1) * DIFF_HALF) for c in range(2)]
    q = [q_ref[0, :, sl] for sl in halves]

    @pl.when((pl.program_id(1) == 0) & (qs == 0))
    def _():
        for dist in range(bias_scr.shape[0]):
            vec = jnp.broadcast_to(vec_ref[0, dist][0:1, :], (SPAN, 2 * SPAN))
            bias_scr[dist] = pltpu.roll(vec, 0, 1, stride=1, stride_axis=0)[:, :SPAN]

    def step(span, carry, masked):
        st = pl.multiple_of(span * SPAN, SPAN)
        bias = bias_scr[qs - span]
        v = v_ref[0, pl.ds(st, SPAN), :]
        out = []
        for c in range(2):
            m, l, acc = carry[c]
            s = lax.dot_general(q[c], k_ref[0, pl.ds(st, SPAN), halves[c]], (((1,), (1,)), ((), ())),
                                preferred_element_type=F32) + bias
            if masked:
                s = jnp.where(causal, s, NEG_BIG)
            m_new = jnp.maximum(m, jnp.max(s, axis=1, keepdims=True))
            alpha = jnp.exp2(m - m_new)
            p = jnp.exp2(s - m_new)
            l = alpha * l + jnp.sum(p, axis=1, keepdims=True)
            acc = alpha * acc + jnp.dot(_bf(p), v, preferred_element_type=F32)
            out.append((m_new, l, acc))
        return tuple(out)

    def init():
        return (jnp.full((SPAN, 1), NEG_BIG, F32), jnp.zeros((SPAN, 1), F32),
                jnp.zeros((SPAN, DIFF_VDIM), F32))

    carry = step(qs, (init(), init()), True)
    carry = lax.fori_loop(0, qs, lambda i, c: step(i, c, False), carry)

    lp = lam_ref[...]
    lam = (jnp.exp(jnp.sum(lp[0:1] * lp[1:2], axis=1, keepdims=True))
           - jnp.exp(jnp.sum(lp[2:3] * lp[3:4], axis=1, keepdims=True)) + lam_init)
    (_, l0, a0), (_, l1, a1) = carry
    o = a0 / l0 - lam * (a1 / l1)
    ms = jnp.mean(o * o, axis=-1, keepdims=True)
    o = o * lax.rsqrt(ms + SUBLN_EPS) * subln_ref[...] * (1.0 - lam_init)
    o_ref[0] = o.astype(o_ref.dtype)


def _diff_attention(qkv, col0, bias_vecs, lam_params, subln, lam_init):
    bsz, s, _ = qkv.shape
    ns = s // SPAN
    return pl.pallas_call(
        functools.partial(_diff_kernel, lam_init=lam_init),
        grid=(H_D, bsz, ns),
        in_specs=[
            pl.BlockSpec((1, SPAN, LANES), lambda h, b, i: (b, i, col0 + h)),
            pl.BlockSpec((1, s, LANES), lambda h, b, i: (b, 0, col0 + H_D + h)),
            pl.BlockSpec((1, s, LANES), lambda h, b, i: (b, 0, col0 + 2 * H_D + h)),
            pl.BlockSpec((1, ns, SUBLANES, 2 * SPAN), lambda h, b, i: (h, 0, 0, 0)),
            pl.BlockSpec((4, DIFF_HALF), lambda h, b, i: (0, 0)),
            pl.BlockSpec((1, DIFF_VDIM), lambda h, b, i: (0, 0)),
        ],
        out_specs=pl.BlockSpec((1, SPAN, LANES), lambda h, b, i: (b, i, h)),
        out_shape=jax.ShapeDtypeStruct((bsz, s, GROUP_WIDTH), BF16),
        scratch_shapes=[pltpu.VMEM((ns, SPAN, SPAN), F32)],
        compiler_params=_params(("arbitrary", "arbitrary", "arbitrary")),
        name="diff_attention",
    )(qkv, qkv, qkv, bias_vecs, lam_params, subln)


def _dil_kernel(q_ref, k_ref, v_ref, b1_ref, b4_ref, b16_ref, y_ref, acc_scr, m_scr, l_scr):
    s_len = q_ref.shape[1]
    first_head = _iota2((BLOCK, LANES), 1) < HEAD_DIM
    bias_refs = (b1_ref, b4_ref, b16_ref)

    def stack_heads(t):
        return jnp.concatenate([jnp.where(first_head, t, 0.0), jnp.where(first_head, 0.0, t)], axis=0)

    def unstack(t):
        return jnp.where(first_head, t[:BLOCK], t[BLOCK:])

    for pi, (window, dil) in enumerate(DILATIONS):
        n_back = window // dil
        nc = s_len // dil // BLOCK
        has_prev = nc > 1
        assert not has_prev or nc % DIL_TILES_PER_STEP == 0
        width = 2 * BLOCK if has_prev else BLOCK
        qi = _iota2((2 * BLOCK, width), 0) & (BLOCK - 1)
        ki = _iota2((2 * BLOCK, width), 1)
        steps = qi + (BLOCK if has_prev else 0) - ki
        band = (steps >= 0) & (steps <= n_back)
        bias = bias_refs[pi][0]
        if not has_prev:
            bias = bias[:, BLOCK:]

        def body(it, carry, dil=dil, nc=nc, has_prev=has_prev, band=band, bias=bias, pi=pi, ki=ki):
            idx0 = it * DIL_TILES_PER_STEP
            tiles = []
            for u in range(DIL_TILES_PER_STEP):
                r, c = (idx0 + u) // nc, (idx0 + u) % nc
                tiles.append((pl.ds(r + dil * BLOCK * c, BLOCK, stride=dil), c))
            kb = [_bf(k_ref[0, rows, :]) for rows, _ in tiles]
            vb = [_bf(v_ref[0, rows, :]) for rows, _ in tiles]
            if has_prev:
                r0, c0 = idx0 // nc, idx0 % nc
                prows = pl.ds(r0 + dil * BLOCK * jnp.maximum(c0 - 1, 0), BLOCK, stride=dil)
                kb = [_bf(k_ref[0, prows, :])] + kb
                vb = [_bf(v_ref[0, prows, :])] + vb
            loads = []
            for u, (rows, c) in enumerate(tiles):
                q = _bf(stack_heads(q_ref[0, rows, :]))
                if has_prev:
                    k = jnp.concatenate([kb[u], kb[u + 1]], axis=0)
                    v = jnp.concatenate([vb[u], vb[u + 1]], axis=0)
                else:
                    k, v = kb[u], vb[u]
                loads.append((rows, c, q, k, v))
            ss = [lax.dot_general(q, k, (((1,), (1,)), ((), ())), preferred_element_type=F32) + bias
                  for _, _, q, k, _ in loads]
            if has_prev:
                ss = [jnp.where(band & (ki >= jnp.where(c > 0, 0, BLOCK)), s, NEG_BIG)
                      for s, (_, c, _, _, _) in zip(ss, loads)]
            else:
                ss = [jnp.where(band, s, NEG_BIG) for s in ss]
            ms = [jnp.max(s, axis=1, keepdims=True) for s in ss]
            ps = [jnp.exp2(s - m) for s, m in zip(ss, ms)]
            ls = [jnp.sum(p, axis=1, keepdims=True) for p in ps]
            os_ = [jnp.dot(_bf(p), v, preferred_element_type=F32) for p, (_, _, _, _, v) in zip(ps, loads)]
            for (rows, _, _, _, _), m, l, o in zip(loads, ms, ls, os_):
                acc_scr[pi, rows, :] = unstack(o)
                m_scr[pi, rows, :] = unstack(jnp.broadcast_to(m, (2 * BLOCK, LANES)))
                l_scr[pi, rows, :] = unstack(jnp.broadcast_to(l, (2 * BLOCK, LANES)))
            return carry

        lax.fori_loop(0, s_len // BLOCK // DIL_TILES_PER_STEP, body, 0)

    tm = 2 * BLOCK

    def merge(i, carry):
        rows = pl.ds(pl.multiple_of(i * tm, tm), tm)
        m1, m2, m3 = m_scr[0, rows, :], m_scr[1, rows, :], m_scr[2, rows, :]
        m = jnp.maximum(jnp.maximum(m1, m2), m3)
        e1, e2, e3 = jnp.exp2(m1 - m), jnp.exp2(m2 - m), jnp.exp2(m3 - m)
        num = e1 * acc_scr[0, rows, :] + e2 * acc_scr[1, rows, :] + e3 * acc_scr[2, rows, :]
        den = e1 * l_scr[0, rows, :] + e2 * l_scr[1, rows, :] + e3 * l_scr[2, rows, :]
        y_ref[0, rows, :] = (num / den).astype(y_ref.dtype)
        return carry

    lax.fori_loop(0, s_len // tm, merge, 0)


def _dilated_attention(qkv_c, biases):
    bsz, s, _ = qkv_c.shape
    nhp = GROUP_WIDTH // LANES
    bspec = pl.BlockSpec((1, 2 * BLOCK, 2 * BLOCK), lambda b, h: (h, 0, 0))
    scr = pltpu.VMEM((len(DILATIONS), s, LANES), F32)
    return pl.pallas_call(
        _dil_kernel,
        grid=(bsz, nhp),
        in_specs=[
            pl.BlockSpec((1, s, LANES), lambda b, h: (b, 0, h)),
            pl.BlockSpec((1, s, LANES), lambda b, h: (b, 0, nhp + h)),
            pl.BlockSpec((1, s, LANES), lambda b, h: (b, 0, 2 * nhp + h)),
            bspec, bspec, bspec,
        ],
        out_specs=pl.BlockSpec((1, s, LANES), lambda b, h: (b, 0, h)),
        out_shape=jax.ShapeDtypeStruct((bsz, s, GROUP_WIDTH), BF16),
        scratch_shapes=[scr, scr, scr],
        compiler_params=_params(("parallel", "parallel")),
        name="dilated_attention",
    )(qkv_c, qkv_c, qkv_c, *biases)


def _unit_lower_inverses(ns):
    size_all = 2 * CHUNK
    row = _iota2((size_all, size_all), 0)
    col = _iota2((size_all, size_all), 1)
    eye = jnp.where(row == col, 1.0, 0.0)
    base = 8
    same8 = (row >> 3) == (col >> 3)
    n8 = [jnp.where(same8, n, 0.0) for n in ns]
    n2 = [_mm(a, a) for a in n8]
    n4 = [_mm(a, a) for a in n2]
    ts = [eye + a for a in n8]
    ts = [t + _mm(t, b) for t, b in zip(ts, n2)]
    ts = [t + _mm(t, b) for t, b in zip(ts, n4)]
    size = 2 * base
    while size <= CHUNK:
        half = size // 2
        shift = size.bit_length() - 1
        off = (((row >> shift) == (col >> shift)) & ((row & (size - 1)) >= half)
               & ((col & (size - 1)) < half))
        us = [_mm(t, jnp.where(off, n, 0.0)) for t, n in zip(ts, ns)]
        ts = [t + _mm(u, t) for t, u in zip(ts, us)]
        size *= 2
    return ts


def _rwkv_kernel(*refs, has_vres):
    if has_vres:
        (pa_ref, mu_ref, w0_ref, wup_ref, a0_ref, aup_ref, gup_ref, kk_ref, ka_ref, rk_ref,
         lnw_ref, lnb_ref, seg_ref, cum_ref, vf_ref, vup_ref, vb_ref,
         y_ref, prev_scr, s_scr) = refs
    else:
        (pa_ref, mu_ref, w0_ref, wup_ref, a0_ref, aup_ref, gup_ref, kk_ref, ka_ref, rk_ref,
         lnw_ref, lnb_ref, seg_ref, cum_ref,
         y_ref, vf_out_ref, prev_scr, s_scr) = refs
    c = GROUP_WIDTH
    tt = pa_ref.shape[1]

    @pl.when(pl.program_id(1) == 0)
    def _():
        prev_scr[...] = jnp.zeros_like(prev_scr)
        s_scr[...] = jnp.zeros_like(s_scr)

    seg = seg_ref[...]

    def seg_sum(t):
        return jnp.concatenate([jnp.dot(_bf(t[:, p * LANES:(p + 1) * LANES]), seg, preferred_element_type=F32)
                                for p in range(c // LANES)], axis=1)

    pa = pa_ref[0]
    rolled = pltpu.roll(pa, 1, 0)
    first = _iota2(pa.shape, 0) == 0
    prev = jnp.where(first, prev_scr[...], rolled)
    prev_scr[...] = pa[tt - 1:tt, :]
    sh = pa + (prev - pa) * mu_ref[...]
    r = sh[:, 0:c]
    k = sh[:, c:2 * c]
    v = sh[:, 2 * c:3 * c]
    w_lo = sh[:, A_OFF_W:A_OFF_W + LORA_PAD]
    a_lo = sh[:, A_OFF_A:A_OFF_A + LORA_PAD]
    g_lo = sh[:, A_OFF_G:A_OFF_G + G_PAD]
    if has_vres:
        pv = sh[:, A_VRES_TILE:A_VRES_TILE + LANES]
        v = v + (vf_ref[0] - v) * _sigmoid(vb_ref[...] + _mm(pv, vup_ref[...]))
    else:
        vf_out_ref[0] = v

    w_log = -_softplus(-(w0_ref[...] + _mm(jnp.tanh(w_lo), wup_ref[...]))) - 0.5
    lw = -jnp.exp(w_log)
    a = _sigmoid(a0_ref[...] + _mm(a_lo, aup_ref[...]))
    g = _mm(_sigmoid(g_lo), gup_ref[...])
    kkr = k * kk_ref[...]
    kk = kkr * lax.rsqrt(jnp.maximum(seg_sum(kkr * kkr), 1e-24))
    k = k * (1.0 + (a - 1.0) * ka_ref[...])

    cum = _mm_split_rhs(cum_ref[...], lw, 2)
    eg = jnp.exp(cum)
    eig = jnp.exp(-cum)
    at = -kk * jnp.exp(cum - lw)
    rt = r * eg
    bt = kk * a * eig
    kt = k * eig

    pair_rows = 2 * CHUNK
    first_head = _iota2((CHUNK, LANES), 1) < HEAD_DIM
    row = _iota2((pair_rows, pair_rows), 0)
    col = _iota2((pair_rows, pair_rows), 1)
    strict = (row & (CHUNK - 1)) > (col & (CHUNK - 1))
    lower = (row & (CHUNK - 1)) >= (col & (CHUNK - 1))
    eye = row == col

    def stack(t):
        return jnp.concatenate([jnp.where(first_head, t, 0.0), jnp.where(first_head, 0.0, t)], axis=0)

    probs = [(ci, p) for ci in range(tt // CHUNK) for p in range(c // LANES)]
    cut = lambda t, ci, p: t[ci * CHUNK:(ci + 1) * CHUNK, p * LANES:(p + 1) * LANES]
    g_end = [cut(eg, ci, p)[CHUNK - 1:CHUNK, :] for ci, p in probs]
    at_p = [stack(cut(at, ci, p)) for ci, p in probs]
    rt_p = [stack(cut(rt, ci, p)) for ci, p in probs]
    bt_p = [stack(cut(bt, ci, p)) for ci, p in probs]
    kt_p = [stack(cut(kt, ci, p)) for ci, p in probs]
    v_p = [stack(cut(v, ci, p)) for ci, p in probs]
    bg_p = [b_ * ge for b_, ge in zip(bt_p, g_end)]
    kg_p = [k_ * ge for k_, ge in zip(kt_p, g_end)]
    grams = [_mm_nt(jnp.concatenate([a_, r_], axis=0), jnp.concatenate([b_, k_], axis=0))
             for a_, r_, b_, k_ in zip(at_p, rt_p, bt_p, kt_p)]
    a_ab = [jnp.where(strict, gm[:pair_rows, :pair_rows], 0.0) for gm in grams]
    a_ak = [jnp.where(strict, gm[:pair_rows, pair_rows:], 0.0) for gm in grams]
    a_rb = [jnp.where(lower, gm[pair_rows:, :pair_rows], 0.0) for gm in grams]
    a_rk = [jnp.where(lower, gm[pair_rows:, pair_rows:], 0.0) for gm in grams]
    both_v = [_mm(jnp.concatenate([a_, r_], axis=0), v_) for a_, r_, v_ in zip(a_ak, a_rk, v_p)]
    akv = [x_[:pair_rows] for x_ in both_v]
    rkv = [x_[pair_rows:] for x_ in both_v]
    ts = _unit_lower_inverses(a_ab)
    pq = [_mm(t, jnp.concatenate([a_, x_], axis=1)) for t, a_, x_ in zip(ts, at_p, akv)]
    ps = [x_[:, :LANES] for x_ in pq]
    qs = [x_[:, LANES:] for x_ in pq]
    rb_pq = [_mm(a_, x_) for a_, x_ in zip(a_rb, pq)]
    r2 = [r_ + x_[:, :LANES] for r_, x_ in zip(rt_p, rb_pq)]
    y0 = [x_[:, LANES:] + k_ for x_, k_ in zip(rb_pq, rkv)]
    ms = [jnp.where(eye, jnp.broadcast_to(ge, (pair_rows, LANES)), 0.0) + _mm_tn(p_, b_)
          for ge, p_, b_ in zip(g_end, ps, bg_p)]
    zs = [_mm_tn(jnp.concatenate([q_, v_], axis=0), jnp.concatenate([b_, k_], axis=0))
          for q_, b_, v_, k_ in zip(qs, bg_p, v_p, kg_p)]

    state = [s_scr[p] for p in range(c // LANES)]
    y_rows = []
    for ci in range(tt // CHUNK):
        y_pairs = []
        for p in range(c // LANES):
            i = ci * (c // LANES) + p
            y_st = _mm_nt(r2[i], state[p]) + y0[i]
            y_pairs.append(y_st[:CHUNK] + y_st[CHUNK:])
            state[p] = _mm(state[p], ms[i]) + zs[i]
        y_rows.append(jnp.concatenate(y_pairs, axis=1))
    for p in range(c // LANES):
        s_scr[p] = state[p]
    y = jnp.concatenate(y_rows, axis=0)

    mu = seg_sum(y) * (1.0 / HEAD_DIM)
    d = y - mu
    var = seg_sum(d * d) * (1.0 / HEAD_DIM)
    yn = d * lax.rsqrt(var + RWKV_LN_EPS) * lnw_ref[...] + lnb_ref[...]
    bonus = seg_sum(r * k * rk_ref[...]) * v
    y_ref[0] = ((yn + bonus) * g).astype(y_ref.dtype)


def _rwkv7(pa, prm, v_first):
    bsz, s, _ = pa.shape
    na = prm["mu"].shape[1]
    c = GROUP_WIDTH
    tt = RWKV_TILE
    has_vres = v_first is not None
    full = lambda shape: pl.BlockSpec(shape, lambda b, t: (0,) * len(shape))
    tile = lambda w: pl.BlockSpec((1, tt, w), lambda b, t: (b, t, 0))
    idx = jnp.arange(tt)
    cum_ones = ((idx[:, None] >= idx[None, :]) & (idx[:, None] // CHUNK == idx[None, :] // CHUNK)).astype(BF16)
    in_specs = [tile(na), full((1, na)), full((1, c)), full((LORA_PAD, c)), full((1, c)),
                full((LORA_PAD, c)), full((G_PAD, c)), full((1, c)), full((1, c)), full((1, c)),
                full((1, c)), full((1, c)), full((LANES, LANES)), full((tt, tt))]
    args = [pa, prm["mu"], prm["w0"], prm["w_up"], prm["a0"], prm["a_up"], prm["g_up"],
            prm["k_k"], prm["k_a"], prm["r_k"], prm["ln_w"], prm["ln_b"], prm["seg"], cum_ones]
    y_shape = jax.ShapeDtypeStruct((bsz, s, c), BF16)
    if has_vres:
        in_specs += [tile(c), full((LORA_PAD, c)), full((1, c))]
        args += [v_first, prm["vres_up"], prm["vres_bias"]]
        out_specs, out_shape = tile(c), y_shape
    else:
        out_specs = [tile(c), tile(c)]
        out_shape = [y_shape, jax.ShapeDtypeStruct((bsz, s, c), F32)]
    res = pl.pallas_call(
        functools.partial(_rwkv_kernel, has_vres=has_vres),
        grid=(bsz, s // tt),
        in_specs=in_specs,
        out_specs=out_specs,
        out_shape=out_shape,
        scratch_shapes=[pltpu.VMEM((1, na), F32), pltpu.VMEM((c // LANES, LANES, LANES), F32)],
        compiler_params=_params(("parallel", "arbitrary")),
        name="rwkv7_vres" if has_vres else "rwkv7",
    )(*args)
    if has_vres:
        return res, v_first
    return res[0], res[1]


def _rel_bucket(dist):
    max_exact = NUM_BUCKETS // 2
    d_f = jnp.maximum(dist, 1).astype(F32)
    large = max_exact + (jnp.log(d_f / max_exact) / math.log(MAX_DISTANCE / max_exact)
                         * (NUM_BUCKETS - max_exact)).astype(jnp.int32)
    large = jnp.minimum(large, NUM_BUCKETS - 1)
    return jnp.where(dist < max_exact, dist, large)


def _toeplitz(u, n_rows, n_cols):
    length = n_rows + n_cols - 1
    g = jnp.roll(jnp.flip(u, axis=-1), -(n_rows - 1), axis=-1)
    flat = jnp.tile(g, (1,) * (u.ndim - 1) + (n_rows,))[..., :n_rows * (length - 1)]
    return flat.reshape(u.shape[:-1] + (n_rows, length - 1))[..., :n_cols]


def _diff_bias_vectors(rel_tab, ns):
    n = jnp.arange(2 * SPAN)
    rel = jnp.where(n < SPAN, -n, 2 * SPAN - n)
    dist = jnp.arange(ns)[:, None] * SPAN + rel[None, :]
    u = rel_tab[_rel_bucket(jnp.maximum(dist, 0))] * LOG2E
    u = jnp.transpose(u, (2, 0, 1)).astype(F32)
    return jnp.broadcast_to(u[:, :, None, :], (H_D, ns, SUBLANES, 2 * SPAN))


def _band_bias(rel_tab, dil):
    steps = jnp.arange(3 * BLOCK - 1) - (BLOCK - 1)
    u = rel_tab[_rel_bucket(jnp.maximum(steps, 0) * dil)] * LOG2E
    t = _toeplitz(jnp.transpose(u, (1, 0)).astype(F32), BLOCK, 2 * BLOCK)
    return t.reshape(N_HEADS * HEAD_DIM // LANES, 2 * BLOCK, 2 * BLOCK)


def _pad_axis(t, size, axis):
    pad = [(0, 0)] * t.ndim
    pad[axis] = (0, size - t.shape[axis])
    return jnp.pad(t, pad)


def _layout_mixer_a(t, vres=None):
    c = GROUP_WIDTH
    o_w, o_a, o_g = 3 * c, 3 * c + W_LORA, 3 * c + W_LORA + A_LORA
    gate = t[..., o_g:N_A_COLS]
    if vres is not None:
        gate = jnp.concatenate([gate, vres], axis=-1)
    parts = [t[..., :o_w], _pad_axis(t[..., o_w:o_a], LORA_PAD, -1),
             _pad_axis(t[..., o_a:o_g], LORA_PAD, -1), _pad_axis(gate, G_PAD, -1)]
    return jnp.concatenate(parts, axis=-1)


def _scale_queries(w_qkv, scale):
    return jnp.concatenate([w_qkv[:, :GROUP_WIDTH] * scale, w_qkv[:, GROUP_WIDTH:]], axis=1)


def kernel(x, c, w_ada, b_ada, norm_gain, w_in, w_out, rel_bias, rwkv_mu, rwkv_w0, rwkv_w_up,
           rwkv_a0, rwkv_a_up, rwkv_g_up, rwkv_k_k, rwkv_k_a, rwkv_r_k, rwkv_ln_w, rwkv_ln_b,
           vres_down, vres_mu, vres_up, vres_bias, diff_lambda, diff_subln, ffn_w13, ffn_w2,
           final_gain):
    out_dtype = x.dtype
    bsz, s, d = x.shape
    cw = GROUP_WIDTH
    x = x.astype(F32)
    row = lambda t: t.reshape(1, -1).astype(F32)

    mod = _modulation(c.astype(F32), w_ada, b_ada)
    rel_tab_c = rel_bias[:, :N_HEADS].astype(F32)
    rel_tab_d = rel_bias[:, N_HEADS:].astype(F32)
    band_biases = [_band_bias(rel_tab_c, dil) for _, dil in DILATIONS]
    diff_bias = _diff_bias_vectors(rel_tab_d, s // SPAN)
    lane = jnp.arange(LANES)
    seg = (lane[:, None] // HEAD_DIM == lane[None, :] // HEAD_DIM).astype(BF16)
    logit_scale = LOG2E * HEAD_DIM ** -0.5

    v_first = None
    for l in range(DEPTH):
        sh1, sc1, g1, sh2, sc2, g2 = [mod[l, :, None, i * d:(i + 1) * d] for i in range(6)]
        gain1, gain2 = row(norm_gain[l, 0]), row(norm_gain[l, 1])
        vres_w = vres_down[l - 1] if l > 0 else None
        w_l = w_in[l].astype(F32)
        w_a = _layout_mixer_a(w_l[:, :N_A_COLS], vres_w)
        o_b, o_c, o_d = N_A_COLS, N_A_COLS + 3 * cw, N_A_COLS + 6 * cw
        w_all = _bf(jnp.concatenate([w_a, _scale_queries(w_l[:, o_b:o_c], logit_scale),
                                     _scale_queries(w_l[:, o_d:N_IN], logit_scale),
                                     _scale_queries(w_l[:, o_c:o_d], logit_scale)], axis=1))
        mu = _layout_mixer_a(rwkv_mu[l], vres_mu[l - 1] if l > 0 else None)

        pa, qkv_bd, qkv_c = _norm_proj(x, gain1, sc1, sh1, w_all, (NA, 6 * cw, 3 * cw), (F32, BF16, F32))

        prm = {
            "mu": row(mu), "w0": row(rwkv_w0[l]), "a0": row(rwkv_a0[l]),
            "w_up": _bf(_pad_axis(rwkv_w_up[l], LORA_PAD, 0)),
            "a_up": _bf(_pad_axis(rwkv_a_up[l], LORA_PAD, 0)),
            "g_up": _bf(_pad_axis(rwkv_g_up[l], G_PAD, 0)),
            "k_k": row(rwkv_k_k[l]), "k_a": row(rwkv_k_a[l]), "r_k": row(rwkv_r_k[l]),
            "ln_w": row(rwkv_ln_w[l]), "ln_b": row(rwkv_ln_b[l]), "seg": seg,
        }
        if l > 0:
            prm["vres_up"] = _bf(jnp.pad(vres_up[l - 1], ((VRES_LANE0, LANES - VRES_LANE0 - V_LORA), (0, 0))))
            prm["vres_bias"] = row(vres_bias[l - 1])
        y_a, v_first = _rwkv7(pa, prm, v_first)

        y_b = _stick_breaking(qkv_bd, 0)
        y_c = _dilated_attention(qkv_c, band_biases)
        lam_init = 0.8 - 0.6 * math.exp(-0.3 * l)
        y_d = _diff_attention(qkv_bd, 3 * cw // LANES, diff_bias, diff_lambda[l].astype(F32),
                              row(diff_subln[l]), lam_init)

        x = _out_proj(y_a, y_b, y_c, y_d, _bf(w_out[l]), x, g1)
        x = _ffn(x, gain2, sc2, sh2, g2, _bf(ffn_w13[l]), _bf(ffn_w2[l]), row(final_gain),
                 final=(l == DEPTH - 1))
    return x.astype(out_dtype)
```

```python
import functools
import math

import jax
import jax.numpy as jnp
from jax import lax
from jax.experimental import pallas as pl
from jax.experimental.pallas import tpu as pltpu

F32 = jnp.float32
BF16 = jnp.bfloat16

D_MODEL = 2048
DEPTH = 2
HEAD_DIM = 64
GROUP_WIDTH = D_MODEL // 4
N_HEADS = GROUP_WIDTH // HEAD_DIM
DIFF_HALF = HEAD_DIM
DIFF_VDIM = 2 * DIFF_HALF
H_D = GROUP_WIDTH // DIFF_VDIM
W_LORA = max(32, int(round(D_MODEL ** 0.5 * 1.8 / 32)) * 32)
A_LORA = max(32, int(round(D_MODEL ** 0.5 * 1.8 / 32)) * 32)
V_LORA = max(32, int(round(D_MODEL ** 0.5 * 1.3 / 32)) * 32)
G_LORA = max(32, int(round(D_MODEL ** 0.8 / 32)) * 32)
N_A_COLS = 3 * GROUP_WIDTH + W_LORA + A_LORA + G_LORA
N_IN = N_A_COLS + 9 * GROUP_WIDTH
D_FF = ((8 * D_MODEL + 3 * 256 - 1) // (3 * 256)) * 256
DILATIONS = ((128, 1), (512, 4), (2048, 16))
BLOCK = 128
NUM_BUCKETS = 32
MAX_DISTANCE = 2048
NORM_EPS = 1e-6
RWKV_LN_EPS = 64e-5
SUBLN_EPS = 1e-5
LOG2E = math.log2(math.e)

LANES = 128
SUBLANES = 8
VMEM_LIMIT_BYTES = 56 * 1024 * 1024
PROJ_VMEM_LIMIT_BYTES = 60 * 1024 * 1024

LORA_PAD = 128
G_PAD = 512
A_OFF_W = 3 * GROUP_WIDTH
A_OFF_A = A_OFF_W + LORA_PAD
A_OFF_G = A_OFF_A + LORA_PAD
A_VRES_TILE = A_OFF_G + (G_LORA // LANES) * LANES
VRES_LANE0 = G_LORA % LANES
assert G_LORA + V_LORA <= G_PAD and VRES_LANE0 + V_LORA <= LANES
PROJ_TN = 768
FIRST_STEP_ROWS = 256
CHUNK = 64
RWKV_TILE = 256
SPAN_BLOCKS = 4
SPAN = SPAN_BLOCKS * BLOCK
DIL_TILES_PER_STEP = 4
NEG_BIG = -1e30
SB_ROW_GROUPS = 2
SB_UNDERFLOW_LOG2 = 160.0


def _bf(x):
    return x.astype(BF16)


def _mm(a, b):
    return jnp.dot(_bf(a), _bf(b), preferred_element_type=F32)


def _mm_nt(a, b):
    return lax.dot_general(_bf(a), _bf(b), (((1,), (1,)), ((), ())), preferred_element_type=F32)


def _mm_tn(a, b):
    return lax.dot_general(_bf(a), _bf(b), (((0,), (0,)), ((), ())), preferred_element_type=F32)


def _split_bf16(x, n):
    parts = []
    rem = x
    for i in range(n):
        p = _bf(rem)
        parts.append(p)
        if i + 1 < n:
            rem = rem - p.astype(F32)
    return parts


def _mm_split_lhs(a, b_bf16, n):
    out = None
    for p in _split_bf16(a, n):
        t = jnp.dot(p, b_bf16, preferred_element_type=F32)
        out = t if out is None else out + t
    return out


def _mm_split_rhs(a_bf16, b, n):
    out = None
    for p in _split_bf16(b, n):
        t = jnp.dot(a_bf16, p, preferred_element_type=F32)
        out = t if out is None else out + t
    return out


def _softplus(x):
    return jnp.maximum(x, 0.0) + jnp.log1p(jnp.exp(-jnp.abs(x)))


def _softplus2(x):
    sign_bit = jnp.uint32(0x80000000)
    neg_abs = lax.bitcast_convert_type(lax.bitcast_convert_type(x, jnp.uint32) | sign_bit, F32)
    return jnp.maximum(x, 0.0) + jnp.log2(1.0 + jnp.exp2(neg_abs))


def _sigmoid(x):
    return jax.nn.sigmoid(x)


def _iota2(shape, dim):
    return lax.broadcasted_iota(jnp.int32, shape, dim)


def _params(sem, vmem=VMEM_LIMIT_BYTES):
    return pltpu.CompilerParams(dimension_semantics=sem, vmem_limit_bytes=vmem)


def _mod_kernel(c_ref, w_ref, b_ref, o_ref):
    c = c_ref[...]
    cond = c * _sigmoid(c)
    o_ref[0] = _mm(cond, w_ref[0]) + b_ref[0]


def _modulation(c, w_ada, b_ada):
    depth, d, n = w_ada.shape
    bsz = c.shape[0]
    tn = 1024
    return pl.pallas_call(
        _mod_kernel,
        grid=(depth, n // tn),
        in_specs=[
            pl.BlockSpec((bsz, d), lambda l, j: (0, 0)),
            pl.BlockSpec((1, d, tn), lambda l, j: (l, 0, j)),
            pl.BlockSpec((1, 1, tn), lambda l, j: (l, 0, j)),
        ],
        out_specs=pl.BlockSpec((1, bsz, tn), lambda l, j: (l, 0, j)),
        out_shape=jax.ShapeDtypeStruct((depth, bsz, n), F32),
        compiler_params=_params(("parallel", "parallel")),
        name="adaln_mod",
    )(c, w_ada, b_ada.reshape(depth, 1, n))


def _norm_mod(x, gain, sc, sh):
    ms = jnp.mean(x * x, axis=-1, keepdims=True)
    return x * lax.rsqrt(ms + NORM_EPS) * gain * (1.0 + sc) + sh


def _proj_kernel(x_ref, gain_ref, sc_ref, sh_ref, *rest, splits):
    n_out = len(splits)
    w_refs, o_refs, h_scr = rest[:n_out], rest[n_out:2 * n_out], rest[-1]
    j = pl.program_id(2)
    tm = h_scr.shape[0]

    @pl.when(j == 0)
    def _():
        for g in range(tm // FIRST_STEP_ROWS):
            rows = slice(g * FIRST_STEP_ROWS, (g + 1) * FIRST_STEP_ROWS)
            h = _bf(_norm_mod(x_ref[0, rows, :], gain_ref[...], sc_ref[0], sh_ref[0]))
            h_scr[rows, :] = h
            o_refs[0][0, rows, :] = jnp.dot(h, w_refs[0][...], preferred_element_type=F32).astype(o_refs[0].dtype)

    lo = 0
    for w_ref, o_ref, n_tiles in zip(w_refs, o_refs, splits):
        @pl.when((j >= max(lo, 1)) & (j < lo + n_tiles))
        def _(w_ref=w_ref, o_ref=o_ref):
            o_ref[0] = jnp.dot(h_scr[...], w_ref[...], preferred_element_type=F32).astype(o_ref.dtype)
        lo += n_tiles


def _norm_proj(x, gain, sc, sh, ws, dtypes):
    bsz, s, d = x.shape
    tm, tn = 1024, PROJ_TN
    splits = tuple(w.shape[1] // tn for w in ws)
    starts = [sum(splits[:k]) for k in range(len(splits))]
    tile_of = lambda k, j: jnp.clip(j - starts[k], 0, splits[k] - 1)
    return pl.pallas_call(
        functools.partial(_proj_kernel, splits=splits),
        grid=(bsz, s // tm, sum(splits)),
        in_specs=[
            pl.BlockSpec((1, tm, d), lambda b, i, j: (b, i, 0)),
            pl.BlockSpec((1, d), lambda b, i, j: (0, 0)),
            pl.BlockSpec((1, 1, d), lambda b, i, j: (b, 0, 0)),
            pl.BlockSpec((1, 1, d), lambda b, i, j: (b, 0, 0)),
        ] + [pl.BlockSpec((d, tn), lambda b, i, j, k=k: (0, tile_of(k, j))) for k in range(len(ws))],
        out_specs=[pl.BlockSpec((1, tm, tn), lambda b, i, j, k=k: (b, i, tile_of(k, j)))
                   for k in range(len(ws))],
        out_shape=[jax.ShapeDtypeStruct((bsz, s, w.shape[1]), dt) for w, dt in zip(ws, dtypes)],
        scratch_shapes=[pltpu.VMEM((tm, d), BF16)],
        compiler_params=_params(("parallel", "parallel", "arbitrary"), PROJ_VMEM_LIMIT_BYTES),
        name="in_proj",
    )(x, gain, sc, sh, *ws)


def _outproj_kernel(ya_ref, yb_ref, yc_ref, yd_ref, w_ref, x_ref, g_ref, o_ref):
    c = GROUP_WIDTH
    acc = jnp.dot(ya_ref[0], w_ref[0:c, :], preferred_element_type=F32)
    acc += jnp.dot(yb_ref[0], w_ref[c:2 * c, :], preferred_element_type=F32)
    acc += jnp.dot(yc_ref[0], w_ref[2 * c:3 * c, :], preferred_element_type=F32)
    acc += jnp.dot(yd_ref[0], w_ref[3 * c:4 * c, :], preferred_element_type=F32)
    o_ref[0] = x_ref[0] + g_ref[0] * acc


def _out_proj(ya, yb, yc, yd, w, x, g1):
    bsz, s, d = x.shape
    c = GROUP_WIDTH
    tm, tn = 512, d
    yspec = pl.BlockSpec((1, tm, c), lambda b, i, j: (b, i, 0))
    return pl.pallas_call(
        _outproj_kernel,
        grid=(bsz, s // tm, d // tn),
        in_specs=[
            yspec, yspec, yspec, yspec,
            pl.BlockSpec((4 * c, tn), lambda b, i, j: (0, j)),
            pl.BlockSpec((1, tm, tn), lambda b, i, j: (b, i, j)),
            pl.BlockSpec((1, 1, tn), lambda b, i, j: (b, 0, j)),
        ],
        out_specs=pl.BlockSpec((1, tm, tn), lambda b, i, j: (b, i, j)),
        out_shape=jax.ShapeDtypeStruct((bsz, s, d), F32),
        compiler_params=_params(("parallel", "parallel", "arbitrary")),
        name="out_proj",
    )(ya, yb, yc, yd, w, x, g1)


def _ffn_kernel(x_ref, gain_ref, sc_ref, sh_ref, g_ref, w1_ref, w3_ref, w2_ref, fg_ref,
                o_ref, h_scr, *, final):
    f = pl.program_id(2)
    tm = h_scr.shape[0]
    acc_scr = o_ref.at[0]

    def chunk(h):
        gate = jnp.dot(h, w1_ref[...], preferred_element_type=F32)
        up = jnp.dot(h, w3_ref[...], preferred_element_type=F32)
        act = gate * _sigmoid(gate) * up
        return jnp.dot(_bf(act), w2_ref[...], preferred_element_type=F32)

    @pl.when(f == 0)
    def _():
        for g in range(tm // FIRST_STEP_ROWS):
            rows = slice(g * FIRST_STEP_ROWS, (g + 1) * FIRST_STEP_ROWS)
            h = _bf(_norm_mod(x_ref[0, rows, :], gain_ref[...], sc_ref[0], sh_ref[0]))
            h_scr[rows, :] = h
            acc_scr[rows, :] = chunk(h)

    @pl.when(f > 0)
    def _():
        acc_scr[...] += chunk(h_scr[...])

    @pl.when(f == pl.num_programs(2) - 1)
    def _():
        xn = x_ref[0] + g_ref[0] * acc_scr[...]
        if final:
            ms = jnp.mean(xn * xn, axis=-1, keepdims=True)
            xn = xn * lax.rsqrt(ms + NORM_EPS) * fg_ref[...]
        o_ref[0] = xn


def _ffn(x, gain, sc, sh, g2, w13, w2, final_gain, final):
    bsz, s, d = x.shape
    dff = w2.shape[0]
    tm, tf = 512, 512
    nf = dff // tf
    vec = pl.BlockSpec((1, 1, d), lambda b, i, f: (b, 0, 0))
    return pl.pallas_call(
        functools.partial(_ffn_kernel, final=final),
        grid=(bsz, s // tm, nf),
        in_specs=[
            pl.BlockSpec((1, tm, d), lambda b, i, f: (b, i, 0)),
            pl.BlockSpec((1, d), lambda b, i, f: (0, 0)),
            vec, vec, vec,
            pl.BlockSpec((d, tf), lambda b, i, f: (0, f)),
            pl.BlockSpec((d, tf), lambda b, i, f: (0, f + nf)),
            pl.BlockSpec((tf, d), lambda b, i, f: (f, 0)),
            pl.BlockSpec((1, d), lambda b, i, f: (0, 0)),
        ],
        out_specs=pl.BlockSpec((1, tm, d), lambda b, i, f: (b, i, 0)),
        out_shape=jax.ShapeDtypeStruct((bsz, s, d), F32),
        scratch_shapes=[pltpu.VMEM((tm, d), BF16)],
        compiler_params=_params(("parallel", "parallel", "arbitrary")),
        name="ffn_final" if final else "ffn",
    )(x, gain, sc, sh, g2, w13, w13, w2, final_gain)


def _sb_kernel(q_ref, k_ref, v_ref, later_ref, o_ref):
    qs = pl.program_id(2)
    nh = LANES // HEAD_DIM
    later = later_ref[...]
    row = _iota2((SPAN, BLOCK), 0)
    col = _iota2((SPAN, BLOCK), 1)

    def span_tile(q, sl, span, run, masked):
        st = pl.multiple_of(span * SPAN, SPAN)
        n_rows = q.shape[0]
        z = lax.dot_general(q, k_ref[0, pl.ds(st, SPAN), sl], (((1,), (1,)), ((), ())),
                            preferred_element_type=F32)
        subs = []
        for j in range(SPAN_BLOCKS):
            zj = z[:, j * BLOCK:(j + 1) * BLOCK]
            cost = _softplus2(zj)
            mask = None
            if masked:
                mask = (j * BLOCK + col) < row
                cost = jnp.where(mask, cost, 0.0)
            subs.append((zj, cost, mask))
        stacked = jnp.concatenate([_bf(cost) for _, cost, _ in subs], axis=0)
        within = jnp.dot(stacked, later, preferred_element_type=F32)
        atts = [None] * SPAN_BLOCKS
        for j in reversed(range(SPAN_BLOCKS)):
            zj, cost, mask = subs[j]
            w = within[j * n_rows:(j + 1) * n_rows]
            att = jnp.exp2(zj - cost - w - run)
            if masked:
                att = jnp.where(mask, att, 0.0)
            atts[j] = _bf(att)
            run = run + jnp.sum(cost, axis=1, keepdims=True)
        contrib = jnp.dot(jnp.concatenate(atts, axis=1), v_ref[0, pl.ds(st, SPAN), sl],
                          preferred_element_type=F32)
        return contrib, run

    lanes = [slice(h * HEAD_DIM, (h + 1) * HEAD_DIM) for h in range(nh)]

    def step(span, rows, carry, masked):
        out = []
        for sl, (acc, run) in zip(lanes, carry):
            contrib, run = span_tile(q_ref[0, rows, sl], sl, span, run, masked)
            out.append((acc + contrib, run))
        return tuple(out)

    init = tuple((jnp.zeros((SPAN, HEAD_DIM), F32), jnp.zeros((SPAN, 1), F32)) for _ in range(nh))
    diag = step(qs, slice(None), init, True)

    group = SPAN // SB_ROW_GROUPS
    outs = [[] for _ in range(nh)]
    for g in range(SB_ROW_GROUPS):
        rows = slice(g * group, (g + 1) * group)

        def more(state):
            i, carry = state
            least = functools.reduce(jnp.minimum, [jnp.min(run) for _, run in carry])
            return (i < qs) & (least < SB_UNDERFLOW_LOG2)

        def body(state, rows=rows):
            i, carry = state
            return i + 1, step(qs - 1 - i, rows, carry, False)

        start = tuple((acc[rows], run[rows]) for acc, run in diag)
        _, carry = lax.while_loop(more, body, (jnp.int32(0), start))
        for h, (acc, _) in enumerate(carry):
            outs[h].append(acc)
    for sl, parts in zip(lanes, outs):
        o_ref[0, :, sl] = jnp.concatenate(parts, axis=0).astype(o_ref.dtype)


def _stick_breaking(qkv, col0):
    bsz, s, _ = qkv.shape
    nhp = GROUP_WIDTH // LANES
    later = jnp.tril(jnp.ones((BLOCK, BLOCK), F32), -1).astype(BF16)
    return pl.pallas_call(
        _sb_kernel,
        grid=(bsz, nhp, s // SPAN),
        in_specs=[
            pl.BlockSpec((1, SPAN, LANES), lambda b, h, i: (b, i, col0 + h)),
            pl.BlockSpec((1, s, LANES), lambda b, h, i: (b, 0, col0 + nhp + h)),
            pl.BlockSpec((1, s, LANES), lambda b, h, i: (b, 0, col0 + 2 * nhp + h)),
            pl.BlockSpec((BLOCK, BLOCK), lambda b, h, i: (0, 0)),
        ],
        out_specs=pl.BlockSpec((1, SPAN, LANES), lambda b, h, i: (b, i, h)),
        out_shape=jax.ShapeDtypeStruct((bsz, s, GROUP_WIDTH), BF16),
        compiler_params=_params(("parallel", "parallel", "arbitrary")),
        name="stick_breaking",
    )(qkv, qkv, qkv, later)


def _diff_kernel(q_ref, k_ref, v_ref, vec_ref, lam_ref, subln_ref, o_ref, bias_scr, *, lam_init):
    qs = pl.program_id(2)
    causal = _iota2((SPAN, SPAN), 1) <= _iota2((SPAN, SPAN), 0)
    halves = [slice(c * DIFF_HALF, (c + 1) * DIFF_HALF) for c in range(2)]
    q = [q_ref[0, :, sl] for sl in halves]

    @pl.when((pl.program_id(1) == 0) & (qs == 0))
    def _():
        for dist in range(bias_scr.shape[0]):
            vec = jnp.broadcast_to(vec_ref[0, dist][0:1, :], (SPAN, 2 * SPAN))
            bias_scr[dist] = pltpu.roll(vec, 0, 1, stride=1, stride_axis=0)[:, :SPAN]

    def step(span, carry, masked):
        st = pl.multiple_of(span * SPAN, SPAN)
        bias = bias_scr[qs - span]
        v = v_ref[0, pl.ds(st, SPAN), :]
        out = []
        for c in range(2):
            m, l, acc = carry[c]
            s = lax.dot_general(q[c], k_ref[0, pl.ds(st, SPAN), halves[c]], (((1,), (1,)), ((), ())),
                                preferred_element_type=F32) + bias
            if masked:
                s = jnp.where(causal, s, NEG_BIG)
            m_new = jnp.maximum(m, jnp.max(s, axis=1, keepdims=True))
            alpha = jnp.exp2(m - m_new)
            p = jnp.exp2(s - m_new)
            l = alpha * l + jnp.sum(p, axis=1, keepdims=True)
            acc = alpha * acc + jnp.dot(_bf(p), v, preferred_element_type=F32)
            out.append((m_new, l, acc))
        return tuple(out)

    def init():
        return (jnp.full((SPAN, 1), NEG_BIG, F32), jnp.zeros((SPAN, 1), F32),
                jnp.zeros((SPAN, DIFF_VDIM), F32))

    carry = step(qs, (init(), init()), True)
    carry = lax.fori_loop(0, qs, lambda i, c: step(i, c, False), carry)

    lp = lam_ref[...]
    lam = (jnp.exp(jnp.sum(lp[0:1] * lp[1:2], axis=1, keepdims=True))
           - jnp.exp(jnp.sum(lp[2:3] * lp[3:4], axis=1, keepdims=True)) + lam_init)
    (_, l0, a0), (_, l1, a1) = carry
    o = a0 / l0 - lam * (a1 / l1)
    ms = jnp.mean(o * o, axis=-1, keepdims=True)
    o = o * lax.rsqrt(ms + SUBLN_EPS) * subln_ref[...] * (1.0 - lam_init)
    o_ref[0] = o.astype(o_ref.dtype)


def _diff_attention(qkv, col0, bias_vecs, lam_params, subln, lam_init):
    bsz, s, _ = qkv.shape
    ns = s // SPAN
    return pl.pallas_call(
        functools.partial(_diff_kernel, lam_init=lam_init),
        grid=(H_D, bsz, ns),
        in_specs=[
            pl.BlockSpec((1, SPAN, LANES), lambda h, b, i: (b, i, col0 + h)),
            pl.BlockSpec((1, s, LANES), lambda h, b, i: (b, 0, col0 + H_D + h)),
            pl.BlockSpec((1, s, LANES), lambda h, b, i: (b, 0, col0 + 2 * H_D + h)),
            pl.BlockSpec((1, ns, SUBLANES, 2 * SPAN), lambda h, b, i: (h, 0, 0, 0)),
            pl.BlockSpec((4, DIFF_HALF), lambda h, b, i: (0, 0)),
            pl.BlockSpec((1, DIFF_VDIM), lambda h, b, i: (0, 0)),
        ],
        out_specs=pl.BlockSpec((1, SPAN, LANES), lambda h, b, i: (b, i, h)),
        out_shape=jax.ShapeDtypeStruct((bsz, s, GROUP_WIDTH), BF16),
        scratch_shapes=[pltpu.VMEM((ns, SPAN, SPAN), F32)],
        compiler_params=_params(("arbitrary", "arbitrary", "arbitrary")),
        name="diff_attention",
    )(qkv, qkv, qkv, bias_vecs, lam_params, subln)


def _dil_kernel(q_ref, k_ref, v_ref, b1_ref, b4_ref, b16_ref, y_ref, acc_scr, m_scr, l_scr):
    s_len = q_ref.shape[1]
    first_head = _iota2((BLOCK, LANES), 1) < HEAD_DIM
    bias_refs = (b1_ref, b4_ref, b16_ref)

    def stack_heads(t):
        return jnp.concatenate([jnp.where(first_head, t, 0.0), jnp.where(first_head, 0.0, t)], axis=0)

    def unstack(t):
        return jnp.where(first_head, t[:BLOCK], t[BLOCK:])

    for pi, (window, dil) in enumerate(DILATIONS):
        n_back = window // dil
        nc = s_len // dil // BLOCK
        has_prev = nc > 1
        assert not has_prev or nc % DIL_TILES_PER_STEP == 0
        width = 2 * BLOCK if has_prev else BLOCK
        qi = _iota2((2 * BLOCK, width), 0) & (BLOCK - 1)
        ki = _iota2((2 * BLOCK, width), 1)
        steps = qi + (BLOCK if has_prev else 0) - ki
        band = (steps >= 0) & (steps <= n_back)
        bias = bias_refs[pi][0]
        if not has_prev:
            bias = bias[:, BLOCK:]

        def body(it, carry, dil=dil, nc=nc, has_prev=has_prev, band=band, bias=bias, pi=pi, ki=ki):
            idx0 = it * DIL_TILES_PER_STEP
            tiles = []
            for u in range(DIL_TILES_PER_STEP):
                r, c = (idx0 + u) // nc, (idx0 + u) % nc
                tiles.append((pl.ds(r + dil * BLOCK * c, BLOCK, stride=dil), c))
            kb = [_bf(k_ref[0, rows, :]) for rows, _ in tiles]
            vb = [_bf(v_ref[0, rows, :]) for rows, _ in tiles]
            if has_prev:
                r0, c0 = idx0 // nc, idx0 % nc
                prows = pl.ds(r0 + dil * BLOCK * jnp.maximum(c0 - 1, 0), BLOCK, stride=dil)
                kb = [_bf(k_ref[0, prows, :])] + kb
                vb = [_bf(v_ref[0, prows, :])] + vb
            loads = []
            for u, (rows, c) in enumerate(tiles):
                q = _bf(stack_heads(q_ref[0, rows, :]))
                if has_prev:
                    k = jnp.concatenate([kb[u], kb[u + 1]], axis=0)
                    v = jnp.concatenate([vb[u], vb[u + 1]], axis=0)
                else:
                    k, v = kb[u], vb[u]
                loads.append((rows, c, q, k, v))
            ss = [lax.dot_general(q, k, (((1,), (1,)), ((), ())), preferred_element_type=F32) + bias
                  for _, _, q, k, _ in loads]
            if has_prev:
                ss = [jnp.where(band & (ki >= jnp.where(c > 0, 0, BLOCK)), s, NEG_BIG)
                      for s, (_, c, _, _, _) in zip(ss, loads)]
            else:
                ss = [jnp.where(band, s, NEG_BIG) for s in ss]
            ms = [jnp.max(s, axis=1, keepdims=True) for s in ss]
            ps = [jnp.exp2(s - m) for s, m in zip(ss, ms)]
            ls = [jnp.sum(p, axis=1, keepdims=True) for p in ps]
            os_ = [jnp.dot(_bf(p), v, preferred_element_type=F32) for p, (_, _, _, _, v) in zip(ps, loads)]
            for (rows, _, _, _, _), m, l, o in zip(loads, ms, ls, os_):
                acc_scr[pi, rows, :] = unstack(o)
                m_scr[pi, rows, :] = unstack(jnp.broadcast_to(m, (2 * BLOCK, LANES)))
                l_scr[pi, rows, :] = unstack(jnp.broadcast_to(l, (2 * BLOCK, LANES)))
            return carry

        lax.fori_loop(0, s_len // BLOCK // DIL_TILES_PER_STEP, body, 0)

    tm = 2 * BLOCK

    def merge(i, carry):
        rows = pl.ds(pl.multiple_of(i * tm, tm), tm)
        m1, m2, m3 = m_scr[0, rows, :], m_scr[1, rows, :], m_scr[2, rows, :]
        m = jnp.maximum(jnp.maximum(m1, m2), m3)
        e1, e2, e3 = jnp.exp2(m1 - m), jnp.exp2(m2 - m), jnp.exp2(m3 - m)
        num = e1 * acc_scr[0, rows, :] + e2 * acc_scr[1, rows, :] + e3 * acc_scr[2, rows, :]
        den = e1 * l_scr[0, rows, :] + e2 * l_scr[1, rows, :] + e3 * l_scr[2, rows, :]
        y_ref[0, rows, :] = (num / den).astype(y_ref.dtype)
        return carry

    lax.fori_loop(0, s_len // tm, merge, 0)


def _dilated_attention(qkv_c, biases):
    bsz, s, _ = qkv_c.shape
    nhp = GROUP_WIDTH // LANES
    bspec = pl.BlockSpec((1, 2 * BLOCK, 2 * BLOCK), lambda b, h: (h, 0, 0))
    scr = pltpu.VMEM((len(DILATIONS), s, LANES), F32)
    return pl.pallas_call(
        _dil_kernel,
        grid=(bsz, nhp),
        in_specs=[
            pl.BlockSpec((1, s, LANES), lambda b, h: (b, 0, h)),
            pl.BlockSpec((1, s, LANES), lambda b, h: (b, 0, nhp + h)),
            pl.BlockSpec((1, s, LANES), lambda b, h: (b, 0, 2 * nhp + h)),
            bspec, bspec, bspec,
        ],
        out_specs=pl.BlockSpec((1, s, LANES), lambda b, h: (b, 0, h)),
        out_shape=jax.ShapeDtypeStruct((bsz, s, GROUP_WIDTH), BF16),
        scratch_shapes=[scr, scr, scr],
        compiler_params=_params(("parallel", "parallel")),
        name="dilated_attention",
    )(qkv_c, qkv_c, qkv_c, *biases)


def _unit_lower_inverses(ns):
    size_all = 2 * CHUNK
    row = _iota2((size_all, size_all), 0)
    col = _iota2((size_all, size_all), 1)
    eye = jnp.where(row == col, 1.0, 0.0)
    base = 8
    same8 = (row >> 3) == (col >> 3)
    n8 = [jnp.where(same8, n, 0.0) for n in ns]
    n2 = [_mm(a, a) for a in n8]
    n4 = [_mm(a, a) for a in n2]
    ts = [eye + a for a in n8]
    ts = [t + _mm(t, b) for t, b in zip(ts, n2)]
    ts = [t + _mm(t, b) for t, b in zip(ts, n4)]
    size = 2 * base
    while size <= CHUNK:
        half = size // 2
        shift = size.bit_length() - 1
        off = (((row >> shift) == (col >> shift)) & ((row & (size - 1)) >= half)
               & ((col & (size - 1)) < half))
        us = [_mm(t, jnp.where(off, n, 0.0)) for t, n in zip(ts, ns)]
        ts = [t + _mm(u, t) for t, u in zip(ts, us)]
        size *= 2
    return ts


def _rwkv_kernel(*refs, has_vres):
    if has_vres:
        (pa_ref, mu_ref, w0_ref, wup_ref, a0_ref, aup_ref, gup_ref, kk_ref, ka_ref, rk_ref,
         lnw_ref, lnb_ref, seg_ref, cum_ref, vf_ref, vup_ref, vb_ref,
         y_ref, prev_scr, s_scr) = refs
    else:
        (pa_ref, mu_ref, w0_ref, wup_ref, a0_ref, aup_ref, gup_ref, kk_ref, ka_ref, rk_ref,
         lnw_ref, lnb_ref, seg_ref, cum_ref,
         y_ref, vf_out_ref, prev_scr, s_scr) = refs
    c = GROUP_WIDTH
    tt = pa_ref.shape[1]

    @pl.when(pl.program_id(1) == 0)
    def _():
        prev_scr[...] = jnp.zeros_like(prev_scr)
        s_scr[...] = jnp.zeros_like(s_scr)

    seg = seg_ref[...]

    def seg_sum(t):
        return jnp.concatenate([jnp.dot(_bf(t[:, p * LANES:(p + 1) * LANES]), seg, preferred_element_type=F32)
                                for p in range(c // LANES)], axis=1)

    pa = pa_ref[0]
    rolled = pltpu.roll(pa, 1, 0)
    first = _iota2(pa.shape, 0) == 0
    prev = jnp.where(first, prev_scr[...], rolled)
    prev_scr[...] = pa[tt - 1:tt, :]
    sh = pa + (prev - pa) * mu_ref[...]
    r = sh[:, 0:c]
    k = sh[:, c:2 * c]
    v = sh[:, 2 * c:3 * c]
    w_lo = sh[:, A_OFF_W:A_OFF_W + LORA_PAD]
    a_lo = sh[:, A_OFF_A:A_OFF_A + LORA_PAD]
    g_lo = sh[:, A_OFF_G:A_OFF_G + G_PAD]
    if has_vres:
        pv = sh[:, A_VRES_TILE:A_VRES_TILE + LANES]
        v = v + (vf_ref[0] - v) * _sigmoid(vb_ref[...] + _mm(pv, vup_ref[...]))
    else:
        vf_out_ref[0] = v

    w_log = -_softplus(-(w0_ref[...] + _mm(jnp.tanh(w_lo), wup_ref[...]))) - 0.5
    lw = -jnp.exp(w_log)
    a = _sigmoid(a0_ref[...] + _mm(a_lo, aup_ref[...]))
    g = _mm(_sigmoid(g_lo), gup_ref[...])
    kkr = k * kk_ref[...]
    kk = kkr * lax.rsqrt(jnp.maximum(seg_sum(kkr * kkr), 1e-24))
    k = k * (1.0 + (a - 1.0) * ka_ref[...])

    cum = _mm_split_rhs(cum_ref[...], lw, 2)
    eg = jnp.exp(cum)
    eig = jnp.exp(-cum)
    at = -kk * jnp.exp(cum - lw)
    rt = r * eg
    bt = kk * a * eig
    kt = k * eig

    pair_rows = 2 * CHUNK
    first_head = _iota2((CHUNK, LANES), 1) < HEAD_DIM
    row = _iota2((pair_rows, pair_rows), 0)
    col = _iota2((pair_rows, pair_rows), 1)
    strict = (row & (CHUNK - 1)) > (col & (CHUNK - 1))
    lower = (row & (CHUNK - 1)) >= (col & (CHUNK - 1))
    eye = row == col

    def stack(t):
        return jnp.concatenate([jnp.where(first_head, t, 0.0), jnp.where(first_head, 0.0, t)], axis=0)

    probs = [(ci, p) for ci in range(tt // CHUNK) for p in range(c // LANES)]
    cut = lambda t, ci, p: t[ci * CHUNK:(ci + 1) * CHUNK, p * LANES:(p + 1) * LANES]
    g_end = [cut(eg, ci, p)[CHUNK - 1:CHUNK, :] for ci, p in probs]
    at_p = [stack(cut(at, ci, p)) for ci, p in probs]
    rt_p = [stack(cut(rt, ci, p)) for ci, p in probs]
    bt_p = [stack(cut(bt, ci, p)) for ci, p in probs]
    kt_p = [stack(cut(kt, ci, p)) for ci, p in probs]
    v_p = [stack(cut(v, ci, p)) for ci, p in probs]
    bg_p = [b_ * ge for b_, ge in zip(bt_p, g_end)]
    kg_p = [k_ * ge for k_, ge in zip(kt_p, g_end)]
    grams = [_mm_nt(jnp.concatenate([a_, r_], axis=0), jnp.concatenate([b_, k_], axis=0))
             for a_, r_, b_, k_ in zip(at_p, rt_p, bt_p, kt_p)]
    a_ab = [jnp.where(strict, gm[:pair_rows, :pair_rows], 0.0) for gm in grams]
    a_ak = [jnp.where(strict, gm[:pair_rows, pair_rows:], 0.0) for gm in grams]
    a_rb = [jnp.where(lower, gm[pair_rows:, :pair_rows], 0.0) for gm in grams]
    a_rk = [jnp.where(lower, gm[pair_rows:, pair_rows:], 0.0) for gm in grams]
    both_v = [_mm(jnp.concatenate([a_, r_], axis=0), v_) for a_, r_, v_ in zip(a_ak, a_rk, v_p)]
    akv = [x_[:pair_rows] for x_ in both_v]
    rkv = [x_[pair_rows:] for x_ in both_v]
    ts = _unit_lower_inverses(a_ab)
    pq = [_mm(t, jnp.concatenate([a_, x_], axis=1)) for t, a_, x_ in zip(ts, at_p, akv)]
    ps = [x_[:, :LANES] for x_ in pq]
    qs = [x_[:, LANES:] for x_ in pq]
    rb_pq = [_mm(a_, x_) for a_, x_ in zip(a_rb, pq)]
    r2 = [r_ + x_[:, :LANES] for r_, x_ in zip(rt_p, rb_pq)]
    y0 = [x_[:, LANES:] + k_ for x_, k_ in zip(rb_pq, rkv)]
    ms = [jnp.where(eye, jnp.broadcast_to(ge, (pair_rows, LANES)), 0.0) + _mm_tn(p_, b_)
          for ge, p_, b_ in zip(g_end, ps, bg_p)]
    zs = [_mm_tn(jnp.concatenate([q_, v_], axis=0), jnp.concatenate([b_, k_], axis=0))
          for q_, b_, v_, k_ in zip(qs, bg_p, v_p, kg_p)]

    state = [s_scr[p] for p in range(c // LANES)]
    y_rows = []
    for ci in range(tt // CHUNK):
        y_pairs = []
        for p in range(c // LANES):
            i = ci * (c // LANES) + p
            y_st = _mm_nt(r2[i], state[p]) + y0[i]
            y_pairs.append(y_st[:CHUNK] + y_st[CHUNK:])
            state[p] = _mm(state[p], ms[i]) + zs[i]
        y_rows.append(jnp.concatenate(y_pairs, axis=1))
    for p in range(c // LANES):
        s_scr[p] = state[p]
    y = jnp.concatenate(y_rows, axis=0)

    mu = seg_sum(y) * (1.0 / HEAD_DIM)
    d = y - mu
    var = seg_sum(d * d) * (1.0 / HEAD_DIM)
    yn = d * lax.rsqrt(var + RWKV_LN_EPS) * lnw_ref[...] + lnb_ref[...]
    bonus = seg_sum(r * k * rk_ref[...]) * v
    y_ref[0] = ((yn + bonus) * g).astype(y_ref.dtype)


def _rwkv7(pa, prm, v_first):
    bsz, s, _ = pa.shape
    na = prm["mu"].shape[1]
    c = GROUP_WIDTH
    tt = RWKV_TILE
    has_vres = v_first is not None
    full = lambda shape: pl.BlockSpec(shape, lambda b, t: (0,) * len(shape))
    tile = lambda w: pl.BlockSpec((1, tt, w), lambda b, t: (b, t, 0))
    idx = jnp.arange(tt)
    cum_ones = ((idx[:, None] >= idx[None, :]) & (idx[:, None] // CHUNK == idx[None, :] // CHUNK)).astype(BF16)
    in_specs = [tile(na), full((1, na)), full((1, c)), full((LORA_PAD, c)), full((1, c)),
                full((LORA_PAD, c)), full((G_PAD, c)), full((1, c)), full((1, c)), full((1, c)),
                full((1, c)), full((1, c)), full((LANES, LANES)), full((tt, tt))]
    args = [pa, prm["mu"], prm["w0"], prm["w_up"], prm["a0"], prm["a_up"], prm["g_up"],
            prm["k_k"], prm["k_a"], prm["r_k"], prm["ln_w"], prm["ln_b"], prm["seg"], cum_ones]
    y_shape = jax.ShapeDtypeStruct((bsz, s, c), BF16)
    if has_vres:
        in_specs += [tile(c), full((LORA_PAD, c)), full((1, c))]
        args += [v_first, prm["vres_up"], prm["vres_bias"]]
        out_specs, out_shape = tile(c), y_shape
    else:
        out_specs = [tile(c), tile(c)]
        out_shape = [y_shape, jax.ShapeDtypeStruct((bsz, s, c), F32)]
    res = pl.pallas_call(
        functools.partial(_rwkv_kernel, has_vres=has_vres),
        grid=(bsz, s // tt),
        in_specs=in_specs,
        out_specs=out_specs,
        out_shape=out_shape,
        scratch_shapes=[pltpu.VMEM((1, na), F32), pltpu.VMEM((c // LANES, LANES, LANES), F32)],
        compiler_params=_params(("parallel", "arbitrary")),
        name="rwkv7_vres" if has_vres else "rwkv7",
    )(*args)
    if has_vres:
        return res, v_first
    return res[0], res[1]


def _rel_bucket(dist):
    max_exact = NUM_BUCKETS // 2
    d_f = jnp.maximum(dist, 1).astype(F32)
    large = max_exact + (jnp.log(d_f / max_exact) / math.log(MAX_DISTANCE / max_exact)
                         * (NUM_BUCKETS - max_exact)).astype(jnp.int32)
    large = jnp.minimum(large, NUM_BUCKETS - 1)
    return jnp.where(dist < max_exact, dist, large)


def _toeplitz(u, n_rows, n_cols):
    length = n_rows + n_cols - 1
    g = jnp.roll(jnp.flip(u, axis=-1), -(n_rows - 1), axis=-1)
    flat = jnp.tile(g, (1,) * (u.ndim - 1) + (n_rows,))[..., :n_rows * (length - 1)]
    return flat.reshape(u.shape[:-1] + (n_rows, length - 1))[..., :n_cols]


def _diff_bias_vectors(rel_tab, ns):
    n = jnp.arange(2 * SPAN)
    rel = jnp.where(n < SPAN, -n, 2 * SPAN - n)
    dist = jnp.arange(ns)[:, None] * SPAN + rel[None, :]
    u = rel_tab[_rel_bucket(jnp.maximum(dist, 0))] * LOG2E
    u = jnp.transpose(u, (2, 0, 1)).astype(F32)
    return jnp.broadcast_to(u[:, :, None, :], (H_D, ns, SUBLANES, 2 * SPAN))


def _band_bias(rel_tab, dil):
    steps = jnp.arange(3 * BLOCK - 1) - (BLOCK - 1)
    u = rel_tab[_rel_bucket(jnp.maximum(steps, 0) * dil)] * LOG2E
    t = _toeplitz(jnp.transpose(u, (1, 0)).astype(F32), BLOCK, 2 * BLOCK)
    return t.reshape(N_HEADS * HEAD_DIM // LANES, 2 * BLOCK, 2 * BLOCK)


def _pad_axis(t, size, axis):
    pad = [(0, 0)] * t.ndim
    pad[axis] = (0, size - t.shape[axis])
    return jnp.pad(t, pad)


def _layout_mixer_a(t, vres=None):
    c = GROUP_WIDTH
    o_w, o_a, o_g = 3 * c, 3 * c + W_LORA, 3 * c + W_LORA + A_LORA
    gate = t[..., o_g:N_A_COLS]
    if vres is not None:
        gate = jnp.concatenate([gate, vres], axis=-1)
    parts = [t[..., :o_w], _pad_axis(t[..., o_w:o_a], LORA_PAD, -1),
             _pad_axis(t[..., o_a:o_g], LORA_PAD, -1), _pad_axis(gate, G_PAD, -1)]
    return jnp.concatenate(parts, axis=-1)


def _scale_queries(w_qkv, scale):
    return jnp.concatenate([w_qkv[:, :GROUP_WIDTH] * scale, w_qkv[:, GROUP_WIDTH:]], axis=1)


def kernel(x, c, w_ada, b_ada, norm_gain, w_in, w_out, rel_bias, rwkv_mu, rwkv_w0, rwkv_w_up,
           rwkv_a0, rwkv_a_up, rwkv_g_up, rwkv_k_k, rwkv_k_a, rwkv_r_k, rwkv_ln_w, rwkv_ln_b,
           vres_down, vres_mu, vres_up, vres_bias, diff_lambda, diff_subln, ffn_w13, ffn_w2,
           final_gain):
    out_dtype = x.dtype
    bsz, s, d = x.shape
    cw = GROUP_WIDTH
    x = x.astype(F32)
    row = lambda t: t.reshape(1, -1).astype(F32)

    mod = _modulation(c.astype(F32), w_ada, b_ada)
    rel_tab_c = rel_bias[:, :N_HEADS].astype(F32)
    rel_tab_d = rel_bias[:, N_HEADS:].astype(F32)
    band_biases = [_band_bias(rel_tab_c, dil) for _, dil in DILATIONS]
    diff_bias = _diff_bias_vectors(rel_tab_d, s // SPAN)
    lane = jnp.arange(LANES)
    seg = (lane[:, None] // HEAD_DIM == lane[None, :] // HEAD_DIM).astype(BF16)
    logit_scale = LOG2E * HEAD_DIM ** -0.5

    v_first = None
    for l in range(DEPTH):
        sh1, sc1, g1, sh2, sc2, g2 = [mod[l, :, None, i * d:(i + 1) * d] for i in range(6)]
        gain1, gain2 = row(norm_gain[l, 0]), row(norm_gain[l, 1])
        vres_w = vres_down[l - 1] if l > 0 else None
        w_l = w_in[l].astype(F32)
        o_b, o_c, o_d = N_A_COLS, N_A_COLS + 3 * cw, N_A_COLS + 6 * cw
        w_a = _bf(_layout_mixer_a(w_l[:, :N_A_COLS], vres_w))
        w_bd = _bf(jnp.concatenate([_scale_queries(w_l[:, o_b:o_c], logit_scale),
                                    _scale_queries(w_l[:, o_d:N_IN], logit_scale)], axis=1))
        w_c = _bf(_scale_queries(w_l[:, o_c:o_d], logit_scale))
        mu = _layout_mixer_a(rwkv_mu[l], vres_mu[l - 1] if l > 0 else None)

        pa, qkv_bd, qkv_c = _norm_proj(x, gain1, sc1, sh1, (w_a, w_bd, w_c), (F32, BF16, F32))

        prm = {
            "mu": row(mu), "w0": row(rwkv_w0[l]), "a0": row(rwkv_a0[l]),
            "w_up": _bf(_pad_axis(rwkv_w_up[l], LORA_PAD, 0)),
            "a_up": _bf(_pad_axis(rwkv_a_up[l], LORA_PAD, 0)),
            "g_up": _bf(_pad_axis(rwkv_g_up[l], G_PAD, 0)),
            "k_k": row(rwkv_k_k[l]), "k_a": row(rwkv_k_a[l]), "r_k": row(rwkv_r_k[l]),
            "ln_w": row(rwkv_ln_w[l]), "ln_b": row(rwkv_ln_b[l]), "seg": seg,
        }
        if l > 0:
            prm["vres_up"] = _bf(jnp.pad(vres_up[l - 1], ((VRES_LANE0, LANES - VRES_LANE0 - V_LORA), (0, 0))))
            prm["vres_bias"] = row(vres_bias[l - 1])
        y_a, v_first = _rwkv7(pa, prm, v_first)

        y_b = _stick_breaking(qkv_bd, 0)
        y_c = _dilated_attention(qkv_c, band_biases)
        lam_init = 0.8 - 0.6 * math.exp(-0.3 * l)
        y_d = _diff_attention(qkv_bd, 3 * cw // LANES, diff_bias, diff_lambda[l].astype(F32),
                              row(diff_subln[l]), lam_init)

        x = _out_proj(y_a, y_b, y_c, y_d, _bf(w_out[l]), x, g1)
        x = _ffn(x, gain2, sc2, sh2, g2, _bf(ffn_w13[l]), _bf(ffn_w2[l]), row(final_gain),
                 final=(l == DEPTH - 1))
    return x.astype(out_dtype)
```

```python
import functools
import math

import jax
import jax.numpy as jnp
from jax import lax
from jax.experimental import pallas as pl
from jax.experimental.pallas import tpu as pltpu

F32 = jnp.float32
BF16 = jnp.bfloat16

D_MODEL = 2048
DEPTH = 2
HEAD_DIM = 64
GROUP_WIDTH = D_MODEL // 4
N_HEADS = GROUP_WIDTH // HEAD_DIM
DIFF_HALF = HEAD_DIM
DIFF_VDIM = 2 * DIFF_HALF
H_D = GROUP_WIDTH // DIFF_VDIM
W_LORA = max(32, int(round(D_MODEL ** 0.5 * 1.8 / 32)) * 32)
A_LORA = max(32, int(round(D_MODEL ** 0.5 * 1.8 / 32)) * 32)
V_LORA = max(32, int(round(D_MODEL ** 0.5 * 1.3 / 32)) * 32)
G_LORA = max(32, int(round(D_MODEL ** 0.8 / 32)) * 32)
N_A_COLS = 3 * GROUP_WIDTH + W_LORA + A_LORA + G_LORA
N_IN = N_A_COLS + 9 * GROUP_WIDTH
D_FF = ((8 * D_MODEL + 3 * 256 - 1) // (3 * 256)) * 256
DILATIONS = ((128, 1), (512, 4), (2048, 16))
BLOCK = 128
NUM_BUCKETS = 32
MAX_DISTANCE = 2048
NORM_EPS = 1e-6
RWKV_LN_EPS = 64e-5
SUBLN_EPS = 1e-5
LOG2E = math.log2(math.e)

LANES = 128
SUBLANES = 8
VMEM_LIMIT_BYTES = 56 * 1024 * 1024

LORA_PAD = 128
G_PAD = 512
A_OFF_W = 3 * GROUP_WIDTH
A_OFF_A = A_OFF_W + LORA_PAD
A_OFF_G = A_OFF_A + LORA_PAD
A_VRES_TILE = A_OFF_G + (G_LORA // LANES) * LANES
VRES_LANE0 = G_LORA % LANES
assert G_LORA + V_LORA <= G_PAD and VRES_LANE0 + V_LORA <= LANES
PROJ_TN = 768
FIRST_STEP_ROWS = 256
CHUNK = 64
RWKV_TILE = 256
SPAN_BLOCKS = 4
SPAN = SPAN_BLOCKS * BLOCK
DIL_TILES_PER_STEP = 4
NEG_BIG = -1e30
SB_ROW_GROUPS = 2
SB_UNDERFLOW_LOG2 = 160.0


def _bf(x):
    return x.astype(BF16)


def _mm(a, b):
    return jnp.dot(_bf(a), _bf(b), preferred_element_type=F32)


def _mm_nt(a, b):
    return lax.dot_general(_bf(a), _bf(b), (((1,), (1,)), ((), ())), preferred_element_type=F32)


def _mm_tn(a, b):
    return lax.dot_general(_bf(a), _bf(b), (((0,), (0,)), ((), ())), preferred_element_type=F32)


def _split_bf16(x, n):
    parts = []
    rem = x
    for i in range(n):
        p = _bf(rem)
        parts.append(p)
        if i + 1 < n:
            rem = rem - p.astype(F32)
    return parts


def _mm_split_lhs(a, b_bf16, n):
    out = None
    for p in _split_bf16(a, n):
        t = jnp.dot(p, b_bf16, preferred_element_type=F32)
        out = t if out is None else out + t
    return out


def _mm_split_rhs(a_bf16, b, n):
    out = None
    for p in _split_bf16(b, n):
        t = jnp.dot(a_bf16, p, preferred_element_type=F32)
        out = t if out is None else out + t
    return out


def _softplus(x):
    return jnp.maximum(x, 0.0) + jnp.log1p(jnp.exp(-jnp.abs(x)))


def _softplus2(x):
    sign_bit = jnp.uint32(0x80000000)
    neg_abs = lax.bitcast_convert_type(lax.bitcast_convert_type(x, jnp.uint32) | sign_bit, F32)
    return jnp.maximum(x, 0.0) + jnp.log2(1.0 + jnp.exp2(neg_abs))


def _sigmoid(x):
    return jax.nn.sigmoid(x)


def _iota2(shape, dim):
    return lax.broadcasted_iota(jnp.int32, shape, dim)


def _params(sem, vmem=VMEM_LIMIT_BYTES):
    return pltpu.CompilerParams(dimension_semantics=sem, vmem_limit_bytes=vmem)


def _mod_kernel(c_ref, w_ref, b_ref, o_ref):
    c = c_ref[...]
    cond = c * _sigmoid(c)
    o_ref[0] = _mm(cond, w_ref[0]) + b_ref[0]


def _modulation(c, w_ada, b_ada):
    depth, d, n = w_ada.shape
    bsz = c.shape[0]
    tn = 1024
    return pl.pallas_call(
        _mod_kernel,
        grid=(depth, n // tn),
        in_specs=[
            pl.BlockSpec((bsz, d), lambda l, j: (0, 0)),
            pl.BlockSpec((1, d, tn), lambda l, j: (l, 0, j)),
            pl.BlockSpec((1, 1, tn), lambda l, j: (l, 0, j)),
        ],
        out_specs=pl.BlockSpec((1, bsz, tn), lambda l, j: (l, 0, j)),
        out_shape=jax.ShapeDtypeStruct((depth, bsz, n), F32),
        compiler_params=_params(("parallel", "parallel")),
        name="adaln_mod",
    )(c, w_ada, b_ada.reshape(depth, 1, n))


def _norm_mod(x, gain, sc, sh):
    ms = jnp.mean(x * x, axis=-1, keepdims=True)
    return x * lax.rsqrt(ms + NORM_EPS) * gain * (1.0 + sc) + sh


def _proj_kernel(x_ref, vec_ref, w_ref, *rest, splits):
    o_refs, h_scr = rest[:-1], rest[-1]
    j = pl.program_id(2)
    tm = h_scr.shape[0]

    @pl.when(j == 0)
    def _():
        for g in range(tm // FIRST_STEP_ROWS):
            rows = slice(g * FIRST_STEP_ROWS, (g + 1) * FIRST_STEP_ROWS)
            v = vec_ref[0]
            h = _bf(_norm_mod(x_ref[0, rows, :], v[0:1], v[1:2], v[2:3]))
            h_scr[rows, :] = h
            o_refs[0][0, rows, :] = jnp.dot(h, w_ref[...], preferred_element_type=F32).astype(o_refs[0].dtype)

    lo = 0
    for o_ref, n_tiles in zip(o_refs, splits):
        @pl.when((j >= max(lo, 1)) & (j < lo + n_tiles))
        def _(o_ref=o_ref):
            o_ref[0] = jnp.dot(h_scr[...], w_ref[...], preferred_element_type=F32).astype(o_ref.dtype)
        lo += n_tiles


def _row_pack(rows, bsz):
    d = rows[0].shape[-1]
    rows = [jnp.broadcast_to(r.reshape(-1, 1, d), (bsz, 1, d)) for r in rows]
    rows.append(jnp.zeros((bsz, SUBLANES - len(rows), d), F32))
    return jnp.concatenate(rows, axis=1)


def _norm_proj(x, gain, sc, sh, w, widths, dtypes):
    bsz, s, d = x.shape
    tm, tn = 1024, PROJ_TN
    splits = tuple(n // tn for n in widths)
    starts = [sum(splits[:k]) for k in range(len(splits))]

    def out_spec(k):
        return pl.BlockSpec((1, tm, tn),
                            lambda b, i, j: (b, i, jnp.clip(j - starts[k], 0, splits[k] - 1)))

    return pl.pallas_call(
        functools.partial(_proj_kernel, splits=splits),
        grid=(bsz, s // tm, sum(splits)),
        in_specs=[
            pl.BlockSpec((1, tm, d), lambda b, i, j: (b, i, 0)),
            pl.BlockSpec((1, SUBLANES, d), lambda b, i, j: (b, 0, 0)),
            pl.BlockSpec((d, tn), lambda b, i, j: (0, j)),
        ],
        out_specs=[out_spec(k) for k in range(len(widths))],
        out_shape=[jax.ShapeDtypeStruct((bsz, s, n), dt) for n, dt in zip(widths, dtypes)],
        scratch_shapes=[pltpu.VMEM((tm, d), BF16)],
        compiler_params=_params(("parallel", "parallel", "arbitrary")),
        name="in_proj",
    )(x, _row_pack([gain, sc, sh], bsz), w)


def _outproj_kernel(ya_ref, yb_ref, yc_ref, yd_ref, w_ref, x_ref, g_ref, o_ref):
    c = GROUP_WIDTH
    acc = jnp.dot(ya_ref[0], w_ref[0:c, :], preferred_element_type=F32)
    acc += jnp.dot(yb_ref[0], w_ref[c:2 * c, :], preferred_element_type=F32)
    acc += jnp.dot(yc_ref[0], w_ref[2 * c:3 * c, :], preferred_element_type=F32)
    acc += jnp.dot(yd_ref[0], w_ref[3 * c:4 * c, :], preferred_element_type=F32)
    o_ref[0] = x_ref[0] + g_ref[0] * acc


def _out_proj(ya, yb, yc, yd, w, x, g1):
    bsz, s, d = x.shape
    c = GROUP_WIDTH
    tm, tn = 512, d
    yspec = pl.BlockSpec((1, tm, c), lambda b, i, j: (b, i, 0))
    return pl.pallas_call(
        _outproj_kernel,
        grid=(bsz, s // tm, d // tn),
        in_specs=[
            yspec, yspec, yspec, yspec,
            pl.BlockSpec((4 * c, tn), lambda b, i, j: (0, j)),
            pl.BlockSpec((1, tm, tn), lambda b, i, j: (b, i, j)),
            pl.BlockSpec((1, 1, tn), lambda b, i, j: (b, 0, j)),
        ],
        out_specs=pl.BlockSpec((1, tm, tn), lambda b, i, j: (b, i, j)),
        out_shape=jax.ShapeDtypeStruct((bsz, s, d), F32),
        compiler_params=_params(("parallel", "parallel", "arbitrary")),
        name="out_proj",
    )(ya, yb, yc, yd, w, x, g1)


def _ffn_kernel(x_ref, vec_ref, w1_ref, w3_ref, w2_ref, o_ref, h_scr, *, final):
    f = pl.program_id(2)
    tm = h_scr.shape[0]
    acc_scr = o_ref.at[0]
    vec = vec_ref[0]

    def chunk(h):
        gate = jnp.dot(h, w1_ref[...], preferred_element_type=F32)
        up = jnp.dot(h, w3_ref[...], preferred_element_type=F32)
        act = gate * _sigmoid(gate) * up
        return jnp.dot(_bf(act), w2_ref[...], preferred_element_type=F32)

    @pl.when(f == 0)
    def _():
        for g in range(tm // FIRST_STEP_ROWS):
            rows = slice(g * FIRST_STEP_ROWS, (g + 1) * FIRST_STEP_ROWS)
            h = _bf(_norm_mod(x_ref[0, rows, :], vec[0:1], vec[1:2], vec[2:3]))
            h_scr[rows, :] = h
            acc_scr[rows, :] = chunk(h)

    @pl.when(f > 0)
    def _():
        acc_scr[...] += chunk(h_scr[...])

    @pl.when(f == pl.num_programs(2) - 1)
    def _():
        xn = x_ref[0] + vec[3:4] * acc_scr[...]
        if final:
            ms = jnp.mean(xn * xn, axis=-1, keepdims=True)
            xn = xn * lax.rsqrt(ms + NORM_EPS) * vec[4:5]
        o_ref[0] = xn


def _ffn(x, gain, sc, sh, g2, w13, w2, final_gain, final):
    bsz, s, d = x.shape
    dff = w2.shape[0]
    tm, tf = 512, 512
    nf = dff // tf
    return pl.pallas_call(
        functools.partial(_ffn_kernel, final=final),
        grid=(bsz, s // tm, nf),
        in_specs=[
            pl.BlockSpec((1, tm, d), lambda b, i, f: (b, i, 0)),
            pl.BlockSpec((1, SUBLANES, d), lambda b, i, f: (b, 0, 0)),
            pl.BlockSpec((d, tf), lambda b, i, f: (0, f)),
            pl.BlockSpec((d, tf), lambda b, i, f: (0, f + nf)),
            pl.BlockSpec((tf, d), lambda b, i, f: (f, 0)),
        ],
        out_specs=pl.BlockSpec((1, tm, d), lambda b, i, f: (b, i, 0)),
        out_shape=jax.ShapeDtypeStruct((bsz, s, d), F32),
        scratch_shapes=[pltpu.VMEM((tm, d), BF16)],
        compiler_params=_params(("parallel", "parallel", "arbitrary")),
        name="ffn_final" if final else "ffn",
    )(x, _row_pack([gain, sc, sh, g2, final_gain], bsz), w13, w13, w2)


def _sb_kernel(q_ref, k_ref, v_ref, later_ref, o_ref):
    qs = pl.program_id(2)
    nh = LANES // HEAD_DIM
    later = later_ref[...]
    row = _iota2((SPAN, BLOCK), 0)
    col = _iota2((SPAN, BLOCK), 1)

    def span_tile(q, sl, span, run, masked):
        st = pl.multiple_of(span * SPAN, SPAN)
        n_rows = q.shape[0]
        z = lax.dot_general(q, k_ref[0, pl.ds(st, SPAN), sl], (((1,), (1,)), ((), ())),
                            preferred_element_type=F32)
        subs = []
        for j in range(SPAN_BLOCKS):
            zj = z[:, j * BLOCK:(j + 1) * BLOCK]
            cost = _softplus2(zj)
            mask = None
            if masked:
                mask = (j * BLOCK + col) < row
                cost = jnp.where(mask, cost, 0.0)
            subs.append((zj, cost, mask))
        stacked = jnp.concatenate([_bf(cost) for _, cost, _ in subs], axis=0)
        within = jnp.dot(stacked, later, preferred_element_type=F32)
        atts = [None] * SPAN_BLOCKS
        for j in reversed(range(SPAN_BLOCKS)):
            zj, cost, mask = subs[j]
            w = within[j * n_rows:(j + 1) * n_rows]
            att = jnp.exp2(zj - cost - w - run)
            if masked:
                att = jnp.where(mask, att, 0.0)
            atts[j] = _bf(att)
            run = run + jnp.sum(cost, axis=1, keepdims=True)
        contrib = jnp.dot(jnp.concatenate(atts, axis=1), v_ref[0, pl.ds(st, SPAN), sl],
                          preferred_element_type=F32)
        return contrib, run

    lanes = [slice(h * HEAD_DIM, (h + 1) * HEAD_DIM) for h in range(nh)]

    def step(span, rows, carry, masked):
        out = []
        for sl, (acc, run) in zip(lanes, carry):
            contrib, run = span_tile(q_ref[0, rows, sl], sl, span, run, masked)
            out.append((acc + contrib, run))
        return tuple(out)

    init = tuple((jnp.zeros((SPAN, HEAD_DIM), F32), jnp.zeros((SPAN, 1), F32)) for _ in range(nh))
    diag = step(qs, slice(None), init, True)

    group = SPAN // SB_ROW_GROUPS
    outs = [[] for _ in range(nh)]
    for g in range(SB_ROW_GROUPS):
        rows = slice(g * group, (g + 1) * group)

        def more(state):
            i, carry = state
            least = functools.reduce(jnp.minimum, [jnp.min(run) for _, run in carry])
            return (i < qs) & (least < SB_UNDERFLOW_LOG2)

        def body(state, rows=rows):
            i, carry = state
            return i + 1, step(qs - 1 - i, rows, carry, False)

        start = tuple((acc[rows], run[rows]) for acc, run in diag)
        _, carry = lax.while_loop(more, body, (jnp.int32(0), start))
        for h, (acc, _) in enumerate(carry):
            outs[h].append(acc)
    for sl, parts in zip(lanes, outs):
        o_ref[0, :, sl] = jnp.concatenate(parts, axis=0).astype(o_ref.dtype)


def _stick_breaking(qkv, col0):
    bsz, s, _ = qkv.shape
    nhp = GROUP_WIDTH // LANES
    later = jnp.tril(jnp.ones((BLOCK, BLOCK), F32), -1).astype(BF16)
    return pl.pallas_call(
        _sb_kernel,
        grid=(bsz, nhp, s // SPAN),
        in_specs=[
            pl.BlockSpec((1, SPAN, LANES), lambda b, h, i: (b, i, col0 + h)),
            pl.BlockSpec((1, s, LANES), lambda b, h, i: (b, 0, col0 + nhp + h)),
            pl.BlockSpec((1, s, LANES), lambda b, h, i: (b, 0, col0 + 2 * nhp + h)),
            pl.BlockSpec((BLOCK, BLOCK), lambda b, h, i: (0, 0)),
        ],
        out_specs=pl.BlockSpec((1, SPAN, LANES), lambda b, h, i: (b, i, h)),
        out_shape=jax.ShapeDtypeStruct((bsz, s, GROUP_WIDTH), BF16),
        compiler_params=_params(("parallel", "parallel", "arbitrary")),
        name="stick_breaking",
    )(qkv, qkv, qkv, later)


def _diff_kernel(q_ref, k_ref, v_ref, vec_ref, lam_ref, subln_ref, o_ref, bias_scr, *, lam_init):
    qs = pl.program_id(2)
    causal = _iota2((SPAN, SPAN), 1) <= _iota2((SPAN, SPAN), 0)
    halves = [slice(c * DIFF_HALF, (c + 1) * DIFF_HALF) for c in range(2)]
    q = [q_ref[0, :, sl] for sl in halves]

    @pl.when((pl.program_id(1) == 0) & (qs == 0))
    def _():
        for dist in range(bias_scr.shape[0]):
            vec = jnp.broadcast_to(vec_ref[0, dist][0:1, :], (SPAN, 2 * SPAN))
            bias_scr[dist] = pltpu.roll(vec, 0, 1, stride=1, stride_axis=0)[:, :SPAN]

    def step(span, carry, masked):
        st = pl.multiple_of(span * SPAN, SPAN)
        bias = bias_scr[qs - span]
        v = v_ref[0, pl.ds(st, SPAN), :]
        out = []
        for c in range(2):
            m, l, acc = carry[c]
            s = lax.dot_general(q[c], k_ref[0, pl.ds(st, SPAN), halves[c]], (((1,), (1,)), ((), ())),
                                preferred_element_type=F32) + bias
            if masked:
                s = jnp.where(causal, s, NEG_BIG)
            m_new = jnp.maximum(m, jnp.max(s, axis=1, keepdims=True))
            alpha = jnp.exp2(m - m_new)
            p = jnp.exp2(s - m_new)
            l = alpha * l + jnp.sum(p, axis=1, keepdims=True)
            acc = alpha * acc + jnp.dot(_bf(p), v, preferred_element_type=F32)
            out.append((m_new, l, acc))
        return tuple(out)

    def init():
        return (jnp.full((SPAN, 1), NEG_BIG, F32), jnp.zeros((SPAN, 1), F32),
                jnp.zeros((SPAN, DIFF_VDIM), F32))

    carry = step(qs, (init(), init()), True)
    carry = lax.fori_loop(0, qs, lambda i, c: step(i, c, False), carry)

    lp = lam_ref[...]
    lam = (jnp.exp(jnp.sum(lp[0:1] * lp[1:2], axis=1, keepdims=True))
           - jnp.exp(jnp.sum(lp[2:3] * lp[3:4], axis=1, keepdims=True)) + lam_init)
    (_, l0, a0), (_, l1, a1) = carry
    o = a0 / l0 - lam * (a1 / l1)
    ms = jnp.mean(o * o, axis=-1, keepdims=True)
    o = o * lax.rsqrt(ms + SUBLN_EPS) * subln_ref[...] * (1.0 - lam_init)
    o_ref[0] = o.astype(o_ref.dtype)


def _diff_attention(qkv, col0, bias_vecs, lam_params, subln, lam_init):
    bsz, s, _ = qkv.shape
    ns = s // SPAN
    return pl.pallas_call(
        functools.partial(_diff_kernel, lam_init=lam_init),
        grid=(H_D, bsz, ns),
        in_specs=[
            pl.BlockSpec((1, SPAN, LANES), lambda h, b, i: (b, i, col0 + h)),
            pl.BlockSpec((1, s, LANES), lambda h, b, i: (b, 0, col0 + H_D + h)),
            pl.BlockSpec((1, s, LANES), lambda h, b, i: (b, 0, col0 + 2 * H_D + h)),
            pl.BlockSpec((1, ns, SUBLANES, 2 * SPAN), lambda h, b, i: (h, 0, 0, 0)),
            pl.BlockSpec((4, DIFF_HALF), lambda h, b, i: (0, 0)),
            pl.BlockSpec((1, DIFF_VDIM), lambda h, b, i: (0, 0)),
        ],
        out_specs=pl.BlockSpec((1, SPAN, LANES), lambda h, b, i: (b, i, h)),
        out_shape=jax.ShapeDtypeStruct((bsz, s, GROUP_WIDTH), BF16),
        scratch_shapes=[pltpu.VMEM((ns, SPAN, SPAN), F32)],
        compiler_params=_params(("arbitrary", "arbitrary", "arbitrary")),
        name="diff_attention",
    )(qkv, qkv, qkv, bias_vecs, lam_params, subln)


def _dil_kernel(q_ref, k_ref, v_ref, b1_ref, b4_ref, b16_ref, y_ref, acc_scr, m_scr, l_scr):
    s_len = q_ref.shape[1]
    first_head = _iota2((BLOCK, LANES), 1) < HEAD_DIM
    bias_refs = (b1_ref, b4_ref, b16_ref)

    def stack_heads(t):
        return jnp.concatenate([jnp.where(first_head, t, 0.0), jnp.where(first_head, 0.0, t)], axis=0)

    def unstack(t):
        return jnp.where(first_head, t[:BLOCK], t[BLOCK:])

    for pi, (window, dil) in enumerate(DILATIONS):
        n_back = window // dil
        nc = s_len // dil // BLOCK
        has_prev = nc > 1
        assert not has_prev or nc % DIL_TILES_PER_STEP == 0
        width = 2 * BLOCK if has_prev else BLOCK
        qi = _iota2((2 * BLOCK, width), 0) & (BLOCK - 1)
        ki = _iota2((2 * BLOCK, width), 1)
        steps = qi + (BLOCK if has_prev else 0) - ki
        band = (steps >= 0) & (steps <= n_back)
        bias = bias_refs[pi][0]
        if not has_prev:
            bias = bias[:, BLOCK:]

        def body(it, carry, dil=dil, nc=nc, has_prev=has_prev, band=band, bias=bias, pi=pi, ki=ki):
            idx0 = it * DIL_TILES_PER_STEP
            tiles = []
            for u in range(DIL_TILES_PER_STEP):
                r, c = (idx0 + u) // nc, (idx0 + u) % nc
                tiles.append((pl.ds(r + dil * BLOCK * c, BLOCK, stride=dil), c))
            kb = [_bf(k_ref[0, rows, :]) for rows, _ in tiles]
            vb = [_bf(v_ref[0, rows, :]) for rows, _ in tiles]
            if has_prev:
                r0, c0 = idx0 // nc, idx0 % nc
                prows = pl.ds(r0 + dil * BLOCK * jnp.maximum(c0 - 1, 0), BLOCK, stride=dil)
                kb = [_bf(k_ref[0, prows, :])] + kb
                vb = [_bf(v_ref[0, prows, :])] + vb
            loads = []
            for u, (rows, c) in enumerate(tiles):
                q = _bf(stack_heads(q_ref[0, rows, :]))
                if has_prev:
                    k = jnp.concatenate([kb[u], kb[u + 1]], axis=0)
                    v = jnp.concatenate([vb[u], vb[u + 1]], axis=0)
                else:
                    k, v = kb[u], vb[u]
                loads.append((rows, c, q, k, v))
            ss = [lax.dot_general(q, k, (((1,), (1,)), ((), ())), preferred_element_type=F32) + bias
                  for _, _, q, k, _ in loads]
            if has_prev:
                ss = [jnp.where(band & (ki >= jnp.where(c > 0, 0, BLOCK)), s, NEG_BIG)
                      for s, (_, c, _, _, _) in zip(ss, loads)]
            else:
                ss = [jnp.where(band, s, NEG_BIG) for s in ss]
            ms = [jnp.max(s, axis=1, keepdims=True) for s in ss]
            ps = [jnp.exp2(s - m) for s, m in zip(ss, ms)]
            ls = [jnp.sum(p, axis=1, keepdims=True) for p in ps]
            os_ = [jnp.dot(_bf(p), v, preferred_element_type=F32) for p, (_, _, _, _, v) in zip(ps, loads)]
            for (rows, _, _, _, _), m, l, o in zip(loads, ms, ls, os_):
                acc_scr[pi, rows, :] = unstack(o)
                m_scr[pi, rows, :] = unstack(jnp.broadcast_to(m, (2 * BLOCK, LANES)))
                l_scr[pi, rows, :] = unstack(jnp.broadcast_to(l, (2 * BLOCK, LANES)))
            return carry

        lax.fori_loop(0, s_len // BLOCK // DIL_TILES_PER_STEP, body, 0)

    tm = 2 * BLOCK

    def merge(i, carry):
        rows = pl.ds(pl.multiple_of(i * tm, tm), tm)
        m1, m2, m3 = m_scr[0, rows, :], m_scr[1, rows, :], m_scr[2, rows, :]
        m = jnp.maximum(jnp.maximum(m1, m2), m3)
        e1, e2, e3 = jnp.exp2(m1 - m), jnp.exp2(m2 - m), jnp.exp2(m3 - m)
        num = e1 * acc_scr[0, rows, :] + e2 * acc_scr[1, rows, :] + e3 * acc_scr[2, rows, :]
        den = e1 * l_scr[0, rows, :] + e2 * l_scr[1, rows, :] + e3 * l_scr[2, rows, :]
        y_ref[0, rows, :] = (num / den).astype(y_ref.dtype)
        return carry

    lax.fori_loop(0, s_len // tm, merge, 0)


def _dilated_attention(qkv_c, biases):
    bsz, s, _ = qkv_c.shape
    nhp = GROUP_WIDTH // LANES
    bspec = pl.BlockSpec((1, 2 * BLOCK, 2 * BLOCK), lambda b, h: (h, 0, 0))
    scr = pltpu.VMEM((len(DILATIONS), s, LANES), F32)
    return pl.pallas_call(
        _dil_kernel,
        grid=(bsz, nhp),
        in_specs=[
            pl.BlockSpec((1, s, LANES), lambda b, h: (b, 0, h)),
            pl.BlockSpec((1, s, LANES), lambda b, h: (b, 0, nhp + h)),
            pl.BlockSpec((1, s, LANES), lambda b, h: (b, 0, 2 * nhp + h)),
            bspec, bspec, bspec,
        ],
        out_specs=pl.BlockSpec((1, s, LANES), lambda b, h: (b, 0, h)),
        out_shape=jax.ShapeDtypeStruct((bsz, s, GROUP_WIDTH), BF16),
        scratch_shapes=[scr, scr, scr],
        compiler_params=_params(("parallel", "parallel")),
        name="dilated_attention",
    )(qkv_c, qkv_c, qkv_c, *biases)


def _unit_lower_inverses(ns):
    size_all = 2 * CHUNK
    row = _iota2((size_all, size_all), 0)
    col = _iota2((size_all, size_all), 1)
    eye = jnp.where(row == col, 1.0, 0.0)
    base = 8
    same8 = (row >> 3) == (col >> 3)
    n8 = [jnp.where(same8, n, 0.0) for n in ns]
    n2 = [_mm(a, a) for a in n8]
    n4 = [_mm(a, a) for a in n2]
    ts = [eye + a for a in n8]
    ts = [t + _mm(t, b) for t, b in zip(ts, n2)]
    ts = [t + _mm(t, b) for t, b in zip(ts, n4)]
    size = 2 * base
    while size <= CHUNK:
        half = size // 2
        shift = size.bit_length() - 1
        off = (((row >> shift) == (col >> shift)) & ((row & (size - 1)) >= half)
               & ((col & (size - 1)) < half))
        us = [_mm(t, jnp.where(off, n, 0.0)) for t, n in zip(ts, ns)]
        ts = [t + _mm(u, t) for t, u in zip(ts, us)]
        size *= 2
    return ts


def _rwkv_kernel(*refs, has_vres):
    if has_vres:
        (pa_ref, mu_ref, w0_ref, wup_ref, a0_ref, aup_ref, gup_ref, kk_ref, ka_ref, rk_ref,
         lnw_ref, lnb_ref, seg_ref, cum_ref, vf_ref, vup_ref, vb_ref,
         y_ref, prev_scr, s_scr) = refs
    else:
        (pa_ref, mu_ref, w0_ref, wup_ref, a0_ref, aup_ref, gup_ref, kk_ref, ka_ref, rk_ref,
         lnw_ref, lnb_ref, seg_ref, cum_ref,
         y_ref, vf_out_ref, prev_scr, s_scr) = refs
    c = GROUP_WIDTH
    tt = pa_ref.shape[1]

    @pl.when(pl.program_id(1) == 0)
    def _():
        prev_scr[...] = jnp.zeros_like(prev_scr)
        s_scr[...] = jnp.zeros_like(s_scr)

    seg = seg_ref[...]

    def seg_sum(t):
        return jnp.concatenate([jnp.dot(_bf(t[:, p * LANES:(p + 1) * LANES]), seg, preferred_element_type=F32)
                                for p in range(c // LANES)], axis=1)

    pa = pa_ref[0]
    rolled = pltpu.roll(pa, 1, 0)
    first = _iota2(pa.shape, 0) == 0
    prev = jnp.where(first, prev_scr[...], rolled)
    prev_scr[...] = pa[tt - 1:tt, :]
    sh = pa + (prev - pa) * mu_ref[...]
    r = sh[:, 0:c]
    k = sh[:, c:2 * c]
    v = sh[:, 2 * c:3 * c]
    w_lo = sh[:, A_OFF_W:A_OFF_W + LORA_PAD]
    a_lo = sh[:, A_OFF_A:A_OFF_A + LORA_PAD]
    g_lo = sh[:, A_OFF_G:A_OFF_G + G_PAD]
    if has_vres:
        pv = sh[:, A_VRES_TILE:A_VRES_TILE + LANES]
        v = v + (vf_ref[0] - v) * _sigmoid(vb_ref[...] + _mm(pv, vup_ref[...]))
    else:
        vf_out_ref[0] = v

    w_log = -_softplus(-(w0_ref[...] + _mm(jnp.tanh(w_lo), wup_ref[...]))) - 0.5
    lw = -jnp.exp(w_log)
    a = _sigmoid(a0_ref[...] + _mm(a_lo, aup_ref[...]))
    g = _mm(_sigmoid(g_lo), gup_ref[...])
    kkr = k * kk_ref[...]
    kk = kkr * lax.rsqrt(jnp.maximum(seg_sum(kkr * kkr), 1e-24))
    k = k * (1.0 + (a - 1.0) * ka_ref[...])

    cum = _mm_split_rhs(cum_ref[...], lw, 2)
    eg = jnp.exp(cum)
    eig = jnp.exp(-cum)
    at = -kk * jnp.exp(cum - lw)
    rt = r * eg
    bt = kk * a * eig
    kt = k * eig

    pair_rows = 2 * CHUNK
    first_head = _iota2((CHUNK, LANES), 1) < HEAD_DIM
    row = _iota2((pair_rows, pair_rows), 0)
    col = _iota2((pair_rows, pair_rows), 1)
    strict = (row & (CHUNK - 1)) > (col & (CHUNK - 1))
    lower = (row & (CHUNK - 1)) >= (col & (CHUNK - 1))
    eye = row == col

    def stack(t):
        return jnp.concatenate([jnp.where(first_head, t, 0.0), jnp.where(first_head, 0.0, t)], axis=0)

    probs = [(ci, p) for ci in range(tt // CHUNK) for p in range(c // LANES)]
    cut = lambda t, ci, p: t[ci * CHUNK:(ci + 1) * CHUNK, p * LANES:(p + 1) * LANES]
    g_end = [cut(eg, ci, p)[CHUNK - 1:CHUNK, :] for ci, p in probs]
    at_p = [stack(cut(at, ci, p)) for ci, p in probs]
    rt_p = [stack(cut(rt, ci, p)) for ci, p in probs]
    bt_p = [stack(cut(bt, ci, p)) for ci, p in probs]
    kt_p = [stack(cut(kt, ci, p)) for ci, p in probs]
    v_p = [stack(cut(v, ci, p)) for ci, p in probs]
    bg_p = [b_ * ge for b_, ge in zip(bt_p, g_end)]
    kg_p = [k_ * ge for k_, ge in zip(kt_p, g_end)]
    grams = [_mm_nt(jnp.concatenate([a_, r_], axis=0), jnp.concatenate([b_, k_], axis=0))
             for a_, r_, b_, k_ in zip(at_p, rt_p, bt_p, kt_p)]
    a_ab = [jnp.where(strict, gm[:pair_rows, :pair_rows], 0.0) for gm in grams]
    a_ak = [jnp.where(strict, gm[:pair_rows, pair_rows:], 0.0) for gm in grams]
    a_rb = [jnp.where(lower, gm[pair_rows:, :pair_rows], 0.0) for gm in grams]
    a_rk = [jnp.where(lower, gm[pair_rows:, pair_rows:], 0.0) for gm in grams]
    both_v = [_mm(jnp.concatenate([a_, r_], axis=0), v_) for a_, r_, v_ in zip(a_ak, a_rk, v_p)]
    akv = [x_[:pair_rows] for x_ in both_v]
    rkv = [x_[pair_rows:] for x_ in both_v]
    ts = _unit_lower_inverses(a_ab)
    pq = [_mm(t, jnp.concatenate([a_, x_], axis=1)) for t, a_, x_ in zip(ts, at_p, akv)]
    ps = [x_[:, :LANES] for x_ in pq]
    qs = [x_[:, LANES:] for x_ in pq]
    rb_pq = [_mm(a_, x_) for a_, x_ in zip(a_rb, pq)]
    r2 = [r_ + x_[:, :LANES] for r_, x_ in zip(rt_p, rb_pq)]
    y0 = [x_[:, LANES:] + k_ for x_, k_ in zip(rb_pq, rkv)]
    ms = [jnp.where(eye, jnp.broadcast_to(ge, (pair_rows, LANES)), 0.0) + _mm_tn(p_, b_)
          for ge, p_, b_ in zip(g_end, ps, bg_p)]
    zs = [_mm_tn(jnp.concatenate([q_, v_], axis=0), jnp.concatenate([b_, k_], axis=0))
          for q_, b_, v_, k_ in zip(qs, bg_p, v_p, kg_p)]

    state = [s_scr[p] for p in range(c // LANES)]
    y_rows = []
    for ci in range(tt // CHUNK):
        y_pairs = []
        for p in range(c // LANES):
            i = ci * (c // LANES) + p
            y_st = _mm_nt(r2[i], state[p]) + y0[i]
            y_pairs.append(y_st[:CHUNK] + y_st[CHUNK:])
            state[p] = _mm(state[p], ms[i]) + zs[i]
        y_rows.append(jnp.concatenate(y_pairs, axis=1))
    for p in range(c // LANES):
        s_scr[p] = state[p]
    y = jnp.concatenate(y_rows, axis=0)

    mu = seg_sum(y) * (1.0 / HEAD_DIM)
    d = y - mu
    var = seg_sum(d * d) * (1.0 / HEAD_DIM)
    yn = d * lax.rsqrt(var + RWKV_LN_EPS) * lnw_ref[...] + lnb_ref[...]
    bonus = seg_sum(r * k * rk_ref[...]) * v
    y_ref[0] = ((yn + bonus) * g).astype(y_ref.dtype)


def _rwkv7(pa, prm, v_first):
    bsz, s, _ = pa.shape
    na = prm["mu"].shape[1]
    c = GROUP_WIDTH
    tt = RWKV_TILE
    has_vres = v_first is not None
    full = lambda shape: pl.BlockSpec(shape, lambda b, t: (0,) * len(shape))
    tile = lambda w: pl.BlockSpec((1, tt, w), lambda b, t: (b, t, 0))
    idx = jnp.arange(tt)
    cum_ones = ((idx[:, None] >= idx[None, :]) & (idx[:, None] // CHUNK == idx[None, :] // CHUNK)).astype(BF16)
    in_specs = [tile(na), full((1, na)), full((1, c)), full((LORA_PAD, c)), full((1, c)),
                full((LORA_PAD, c)), full((G_PAD, c)), full((1, c)), full((1, c)), full((1, c)),
                full((1, c)), full((1, c)), full((LANES, LANES)), full((tt, tt))]
    args = [pa, prm["mu"], prm["w0"], prm["w_up"], prm["a0"], prm["a_up"], prm["g_up"],
            prm["k_k"], prm["k_a"], prm["r_k"], prm["ln_w"], prm["ln_b"], prm["seg"], cum_ones]
    y_shape = jax.ShapeDtypeStruct((bsz, s, c), BF16)
    if has_vres:
        in_specs += [tile(c), full((LORA_PAD, c)), full((1, c))]
        args += [v_first, prm["vres_up"], prm["vres_bias"]]
        out_specs, out_shape = tile(c), y_shape
    else:
        out_specs = [tile(c), tile(c)]
        out_shape = [y_shape, jax.ShapeDtypeStruct((bsz, s, c), F32)]
    res = pl.pallas_call(
        functools.partial(_rwkv_kernel, has_vres=has_vres),
        grid=(bsz, s // tt),
        in_specs=in_specs,
        out_specs=out_specs,
        out_shape=out_shape,
        scratch_shapes=[pltpu.VMEM((1, na), F32), pltpu.VMEM((c // LANES, LANES, LANES), F32)],
        compiler_params=_params(("parallel", "arbitrary")),
        name="rwkv7_vres" if has_vres else "rwkv7",
    )(*args)
    if has_vres:
        return res, v_first
    return res[0], res[1]


def _rel_bucket(dist):
    max_exact = NUM_BUCKETS // 2
    d_f = jnp.maximum(dist, 1).astype(F32)
    large = max_exact + (jnp.log(d_f / max_exact) / math.log(MAX_DISTANCE / max_exact)
                         * (NUM_BUCKETS - max_exact)).astype(jnp.int32)
    large = jnp.minimum(large, NUM_BUCKETS - 1)
    return jnp.where(dist < max_exact, dist, large)


def _toeplitz(u, n_rows, n_cols):
    length = n_rows + n_cols - 1
    g = jnp.roll(jnp.flip(u, axis=-1), -(n_rows - 1), axis=-1)
    flat = jnp.tile(g, (1,) * (u.ndim - 1) + (n_rows,))[..., :n_rows * (length - 1)]
    return flat.reshape(u.shape[:-1] + (n_rows, length - 1))[..., :n_cols]


def _diff_bias_vectors(rel_tab, ns):
    n = jnp.arange(2 * SPAN)
    rel = jnp.where(n < SPAN, -n, 2 * SPAN - n)
    dist = jnp.arange(ns)[:, None] * SPAN + rel[None, :]
    u = rel_tab[_rel_bucket(jnp.maximum(dist, 0))] * LOG2E
    u = jnp.transpose(u, (2, 0, 1)).astype(F32)
    return jnp.broadcast_to(u[:, :, None, :], (H_D, ns, SUBLANES, 2 * SPAN))


def _band_bias(rel_tab, dil):
    steps = jnp.arange(3 * BLOCK - 1) - (BLOCK - 1)
    u = rel_tab[_rel_bucket(jnp.maximum(steps, 0) * dil)] * LOG2E
    t = _toeplitz(jnp.transpose(u, (1, 0)).astype(F32), BLOCK, 2 * BLOCK)
    return t.reshape(N_HEADS * HEAD_DIM // LANES, 2 * BLOCK, 2 * BLOCK)


def _pad_axis(t, size, axis):
    pad = [(0, 0)] * t.ndim
    pad[axis] = (0, size - t.shape[axis])
    return jnp.pad(t, pad)


def _layout_mixer_a(t, vres=None):
    c = GROUP_WIDTH
    o_w, o_a, o_g = 3 * c, 3 * c + W_LORA, 3 * c + W_LORA + A_LORA
    gate = t[..., o_g:N_A_COLS]
    if vres is not None:
        gate = jnp.concatenate([gate, vres], axis=-1)
    parts = [t[..., :o_w], _pad_axis(t[..., o_w:o_a], LORA_PAD, -1),
             _pad_axis(t[..., o_a:o_g], LORA_PAD, -1), _pad_axis(gate, G_PAD, -1)]
    return jnp.concatenate(parts, axis=-1)


def _scale_queries(w_qkv, scale):
    return jnp.concatenate([w_qkv[:, :GROUP_WIDTH] * scale, w_qkv[:, GROUP_WIDTH:]], axis=1)


def kernel(x, c, w_ada, b_ada, norm_gain, w_in, w_out, rel_bias, rwkv_mu, rwkv_w0, rwkv_w_up,
           rwkv_a0, rwkv_a_up, rwkv_g_up, rwkv_k_k, rwkv_k_a, rwkv_r_k, rwkv_ln_w, rwkv_ln_b,
           vres_down, vres_mu, vres_up, vres_bias, diff_lambda, diff_subln, ffn_w13, ffn_w2,
           final_gain):
    out_dtype = x.dtype
    bsz, s, d = x.shape
    cw = GROUP_WIDTH
    x = x.astype(F32)
    row = lambda t: t.reshape(1, -1).astype(F32)

    mod = _modulation(c.astype(F32), w_ada, b_ada)
    rel_tab_c = rel_bias[:, :N_HEADS].astype(F32)
    rel_tab_d = rel_bias[:, N_HEADS:].astype(F32)
    band_biases = [_band_bias(rel_tab_c, dil) for _, dil in DILATIONS]
    diff_bias = _diff_bias_vectors(rel_tab_d, s // SPAN)
    lane = jnp.arange(LANES)
    seg = (lane[:, None] // HEAD_DIM == lane[None, :] // HEAD_DIM).astype(BF16)
    logit_scale = LOG2E * HEAD_DIM ** -0.5

    v_first = None
    for l in range(DEPTH):
        sh1, sc1, g1, sh2, sc2, g2 = [mod[l, :, None, i * d:(i + 1) * d] for i in range(6)]
        gain1, gain2 = row(norm_gain[l, 0]), row(norm_gain[l, 1])
        vres_w = vres_down[l - 1] if l > 0 else None
        w_l = w_in[l].astype(F32)
        o_b, o_c, o_d = N_A_COLS, N_A_COLS + 3 * cw, N_A_COLS + 6 * cw
        w_a = _layout_mixer_a(w_l[:, :N_A_COLS], vres_w)
        w_all = _bf(jnp.concatenate([w_a, _scale_queries(w_l[:, o_b:o_c], logit_scale),
                                     _scale_queries(w_l[:, o_d:N_IN], logit_scale),
                                     _scale_queries(w_l[:, o_c:o_d], logit_scale)], axis=1))
        mu = _layout_mixer_a(rwkv_mu[l], vres_mu[l - 1] if l > 0 else None)

        pa, qkv_bd, qkv_c = _norm_proj(x, gain1, sc1, sh1, w_all, (w_a.shape[1], 6 * cw, 3 * cw),
                                       (F32, BF16, F32))

        prm = {
            "mu": row(mu), "w0": row(rwkv_w0[l]), "a0": row(rwkv_a0[l]),
            "w_up": _bf(_pad_axis(rwkv_w_up[l], LORA_PAD, 0)),
            "a_up": _bf(_pad_axis(rwkv_a_up[l], LORA_PAD, 0)),
            "g_up": _bf(_pad_axis(rwkv_g_up[l], G_PAD, 0)),
            "k_k": row(rwkv_k_k[l]), "k_a": row(rwkv_k_a[l]), "r_k": row(rwkv_r_k[l]),
            "ln_w": row(rwkv_ln_w[l]), "ln_b": row(rwkv_ln_b[l]), "seg": seg,
        }
        if l > 0:
            prm["vres_up"] = _bf(jnp.pad(vres_up[l - 1], ((VRES_LANE0, LANES - VRES_LANE0 - V_LORA), (0, 0))))
            prm["vres_bias"] = row(vres_bias[l - 1])
        y_a, v_first = _rwkv7(pa, prm, v_first)

        y_b = _stick_breaking(qkv_bd, 0)
        y_c = _dilated_attention(qkv_c, band_biases)
        lam_init = 0.8 - 0.6 * math.exp(-0.3 * l)
        y_d = _diff_attention(qkv_bd, 3 * cw // LANES, diff_bias, diff_lambda[l].astype(F32),
                              row(diff_subln[l]), lam_init)

        x = _out_proj(y_a, y_b, y_c, y_d, _bf(w_out[l]), x, g1)
        x = _ffn(x, gain2, sc2, sh2, g2, _bf(ffn_w13[l]), _bf(ffn_w2[l]), row(final_gain),
                 final=(l == DEPTH - 1))
    return x.astype(out_dtype)
```

```python
import functools
import math

import jax
import jax.numpy as jnp
from jax import lax
from jax.experimental import pallas as pl
from jax.experimental.pallas import tpu as pltpu

F32 = jnp.float32
BF16 = jnp.bfloat16

D_MODEL = 2048
DEPTH = 2
HEAD_DIM = 64
GROUP_WIDTH = D_MODEL // 4
N_HEADS = GROUP_WIDTH // HEAD_DIM
DIFF_HALF = HEAD_DIM
DIFF_VDIM = 2 * DIFF_HALF
H_D = GROUP_WIDTH // DIFF_VDIM
W_LORA = max(32, int(round(D_MODEL ** 0.5 * 1.8 / 32)) * 32)
A_LORA = max(32, int(round(D_MODEL ** 0.5 * 1.8 / 32)) * 32)
V_LORA = max(32, int(round(D_MODEL ** 0.5 * 1.3 / 32)) * 32)
G_LORA = max(32, int(round(D_MODEL ** 0.8 / 32)) * 32)
N_A_COLS = 3 * GROUP_WIDTH + W_LORA + A_LORA + G_LORA
N_IN = N_A_COLS + 9 * GROUP_WIDTH
D_FF = ((8 * D_MODEL + 3 * 256 - 1) // (3 * 256)) * 256
DILATIONS = ((128, 1), (512, 4), (2048, 16))
BLOCK = 128
NUM_BUCKETS = 32
MAX_DISTANCE = 2048
NORM_EPS = 1e-6
RWKV_LN_EPS = 64e-5
SUBLN_EPS = 1e-5
LOG2E = math.log2(math.e)

LANES = 128
SUBLANES = 8
VMEM_LIMIT_BYTES = 56 * 1024 * 1024

LORA_PAD = 128
G_PAD = 512
A_OFF_W = 3 * GROUP_WIDTH
A_OFF_A = A_OFF_W + LORA_PAD
A_OFF_G = A_OFF_A + LORA_PAD
A_VRES_TILE = A_OFF_G + (G_LORA // LANES) * LANES
VRES_LANE0 = G_LORA % LANES
assert G_LORA + V_LORA <= G_PAD and VRES_LANE0 + V_LORA <= LANES
PROJ_TN = 768
FIRST_STEP_ROWS = 256
CHUNK = 64
RWKV_TILE = 512
SPAN_BLOCKS = 4
SPAN = SPAN_BLOCKS * BLOCK
DIL_TILES_PER_STEP = 4
NEG_BIG = -1e30
SB_ROW_GROUPS = 2
SB_UNDERFLOW_LOG2 = 160.0


def _bf(x):
    return x.astype(BF16)


def _mm(a, b):
    return jnp.dot(_bf(a), _bf(b), preferred_element_type=F32)


def _mm_nt(a, b):
    return lax.dot_general(_bf(a), _bf(b), (((1,), (1,)), ((), ())), preferred_element_type=F32)


def _mm_tn(a, b):
    return lax.dot_general(_bf(a), _bf(b), (((0,), (0,)), ((), ())), preferred_element_type=F32)


def _split_bf16(x, n):
    parts = []
    rem = x
    for i in range(n):
        p = _bf(rem)
        parts.append(p)
        if i + 1 < n:
            rem = rem - p.astype(F32)
    return parts


def _mm_split_rhs(a_bf16, b, n):
    out = None
    for p in _split_bf16(b, n):
        t = jnp.dot(a_bf16, p, preferred_element_type=F32)
        out = t if out is None else out + t
    return out


def _softplus(x):
    return jnp.maximum(x, 0.0) + jnp.log1p(jnp.exp(-jnp.abs(x)))


def _softplus2(x):
    sign_bit = jnp.uint32(0x80000000)
    neg_abs = lax.bitcast_convert_type(lax.bitcast_convert_type(x, jnp.uint32) | sign_bit, F32)
    return jnp.maximum(x, 0.0) + jnp.log2(1.0 + jnp.exp2(neg_abs))


def _sigmoid(x):
    return jax.nn.sigmoid(x)


def _iota2(shape, dim):
    return lax.broadcasted_iota(jnp.int32, shape, dim)


def _params(sem, vmem=VMEM_LIMIT_BYTES):
    return pltpu.CompilerParams(dimension_semantics=sem, vmem_limit_bytes=vmem)


def _mod_kernel(c_ref, w_ref, b_ref, o_ref):
    c = c_ref[...]
    cond = c * _sigmoid(c)
    o_ref[0] = _mm(cond, w_ref[0]) + b_ref[0]


def _modulation(c, w_ada, b_ada):
    depth, d, n = w_ada.shape
    bsz = c.shape[0]
    tn = 1024
    return pl.pallas_call(
        _mod_kernel,
        grid=(depth, n // tn),
        in_specs=[
            pl.BlockSpec((bsz, d), lambda l, j: (0, 0)),
            pl.BlockSpec((1, d, tn), lambda l, j: (l, 0, j)),
            pl.BlockSpec((1, 1, tn), lambda l, j: (l, 0, j)),
        ],
        out_specs=pl.BlockSpec((1, bsz, tn), lambda l, j: (l, 0, j)),
        out_shape=jax.ShapeDtypeStruct((depth, bsz, n), F32),
        compiler_params=_params(("parallel", "parallel")),
        name="adaln_mod",
    )(c, w_ada, b_ada.reshape(depth, 1, n))


def _norm_mod(x, gain, sc, sh):
    ms = jnp.mean(x * x, axis=-1, keepdims=True)
    return x * lax.rsqrt(ms + NORM_EPS) * gain * (1.0 + sc) + sh


def _proj_kernel(x_ref, vec_ref, w_ref, *rest, splits):
    o_refs, h_scr = rest[:-1], rest[-1]
    j = pl.program_id(2)
    tm = h_scr.shape[0]

    @pl.when(j == 0)
    def _():
        for g in range(tm // FIRST_STEP_ROWS):
            rows = slice(g * FIRST_STEP_ROWS, (g + 1) * FIRST_STEP_ROWS)
            v = vec_ref[0]
            h = _bf(_norm_mod(x_ref[0, rows, :], v[0:1], v[1:2], v[2:3]))
            h_scr[rows, :] = h
            o_refs[0][0, rows, :] = jnp.dot(h, w_ref[...], preferred_element_type=F32).astype(o_refs[0].dtype)

    lo = 0
    for o_ref, n_tiles in zip(o_refs, splits):
        @pl.when((j >= max(lo, 1)) & (j < lo + n_tiles))
        def _(o_ref=o_ref):
            o_ref[0] = jnp.dot(h_scr[...], w_ref[...], preferred_element_type=F32).astype(o_ref.dtype)
        lo += n_tiles


def _row_pack(rows, bsz):
    d = rows[0].shape[-1]
    rows = [jnp.broadcast_to(r.reshape(-1, 1, d), (bsz, 1, d)) for r in rows]
    rows.append(jnp.zeros((bsz, SUBLANES - len(rows), d), F32))
    return jnp.concatenate(rows, axis=1)


def _norm_proj(x, gain, sc, sh, w, widths, dtypes):
    bsz, s, d = x.shape
    tm, tn = 1024, PROJ_TN
    splits = tuple(n // tn for n in widths)
    starts = [sum(splits[:k]) for k in range(len(splits))]

    def out_spec(k):
        return pl.BlockSpec((1, tm, tn),
                            lambda b, i, j: (b, i, jnp.clip(j - starts[k], 0, splits[k] - 1)))

    return pl.pallas_call(
        functools.partial(_proj_kernel, splits=splits),
        grid=(bsz, s // tm, sum(splits)),
        in_specs=[
            pl.BlockSpec((1, tm, d), lambda b, i, j: (b, i, 0)),
            pl.BlockSpec((1, SUBLANES, d), lambda b, i, j: (b, 0, 0)),
            pl.BlockSpec((d, tn), lambda b, i, j: (0, j)),
        ],
        out_specs=[out_spec(k) for k in range(len(widths))],
        out_shape=[jax.ShapeDtypeStruct((bsz, s, n), dt) for n, dt in zip(widths, dtypes)],
        scratch_shapes=[pltpu.VMEM((tm, d), BF16)],
        compiler_params=_params(("parallel", "parallel", "arbitrary")),
        name="in_proj",
    )(x, _row_pack([gain, sc, sh], bsz), w)


def _outproj_kernel(ya_ref, yb_ref, yc_ref, yd_ref, w_ref, x_ref, g_ref, o_ref):
    c = GROUP_WIDTH
    acc = jnp.dot(ya_ref[0], w_ref[0:c, :], preferred_element_type=F32)
    acc += jnp.dot(yb_ref[0], w_ref[c:2 * c, :], preferred_element_type=F32)
    acc += jnp.dot(yc_ref[0], w_ref[2 * c:3 * c, :], preferred_element_type=F32)
    acc += jnp.dot(yd_ref[0], w_ref[3 * c:4 * c, :], preferred_element_type=F32)
    o_ref[0] = x_ref[0] + g_ref[0] * acc


def _out_proj(ya, yb, yc, yd, w, x, g1):
    bsz, s, d = x.shape
    c = GROUP_WIDTH
    tm, tn = 512, d
    yspec = pl.BlockSpec((1, tm, c), lambda b, i, j: (b, i, 0))
    return pl.pallas_call(
        _outproj_kernel,
        grid=(bsz, s // tm, d // tn),
        in_specs=[
            yspec, yspec, yspec, yspec,
            pl.BlockSpec((4 * c, tn), lambda b, i, j: (0, j)),
            pl.BlockSpec((1, tm, tn), lambda b, i, j: (b, i, j)),
            pl.BlockSpec((1, 1, tn), lambda b, i, j: (b, 0, j)),
        ],
        out_specs=pl.BlockSpec((1, tm, tn), lambda b, i, j: (b, i, j)),
        out_shape=jax.ShapeDtypeStruct((bsz, s, d), F32),
        compiler_params=_params(("parallel", "parallel", "arbitrary")),
        name="out_proj",
    )(ya, yb, yc, yd, w, x, g1)


def _ffn_kernel(x_ref, vec_ref, w1_ref, w3_ref, w2_ref, o_ref, h_scr, *, final):
    f = pl.program_id(2)
    tm = h_scr.shape[0]
    acc_scr = o_ref.at[0]
    vec = vec_ref[0]

    def chunk(h):
        gate = jnp.dot(h, w1_ref[...], preferred_element_type=F32)
        up = jnp.dot(h, w3_ref[...], preferred_element_type=F32)
        act = gate * _sigmoid(gate) * up
        return jnp.dot(_bf(act), w2_ref[...], preferred_element_type=F32)

    @pl.when(f == 0)
    def _():
        for g in range(tm // FIRST_STEP_ROWS):
            rows = slice(g * FIRST_STEP_ROWS, (g + 1) * FIRST_STEP_ROWS)
            h = _bf(_norm_mod(x_ref[0, rows, :], vec[0:1], vec[1:2], vec[2:3]))
            h_scr[rows, :] = h
            acc_scr[rows, :] = chunk(h)

    @pl.when(f > 0)
    def _():
        acc_scr[...] += chunk(h_scr[...])

    @pl.when(f == pl.num_programs(2) - 1)
    def _():
        xn = x_ref[0] + vec[3:4] * acc_scr[...]
        if final:
            ms = jnp.mean(xn * xn, axis=-1, keepdims=True)
            xn = xn * lax.rsqrt(ms + NORM_EPS) * vec[4:5]
        o_ref[0] = xn


def _ffn(x, gain, sc, sh, g2, w13, w2, final_gain, final):
    bsz, s, d = x.shape
    dff = w2.shape[0]
    tm, tf = 512, 512
    nf = dff // tf
    return pl.pallas_call(
        functools.partial(_ffn_kernel, final=final),
        grid=(bsz, s // tm, nf),
        in_specs=[
            pl.BlockSpec((1, tm, d), lambda b, i, f: (b, i, 0)),
            pl.BlockSpec((1, SUBLANES, d), lambda b, i, f: (b, 0, 0)),
            pl.BlockSpec((d, tf), lambda b, i, f: (0, f)),
            pl.BlockSpec((d, tf), lambda b, i, f: (0, f + nf)),
            pl.BlockSpec((tf, d), lambda b, i, f: (f, 0)),
        ],
        out_specs=pl.BlockSpec((1, tm, d), lambda b, i, f: (b, i, 0)),
        out_shape=jax.ShapeDtypeStruct((bsz, s, d), F32),
        scratch_shapes=[pltpu.VMEM((tm, d), BF16)],
        compiler_params=_params(("parallel", "parallel", "arbitrary")),
        name="ffn_final" if final else "ffn",
    )(x, _row_pack([gain, sc, sh, g2, final_gain], bsz), w13, w13, w2)


def _sb_kernel(q_ref, k_ref, v_ref, later_ref, o_ref):
    qs = pl.program_id(2)
    nh = LANES // HEAD_DIM
    later = later_ref[...]
    row = _iota2((SPAN, BLOCK), 0)
    col = _iota2((SPAN, BLOCK), 1)

    def span_tile(q, sl, span, run, masked):
        st = pl.multiple_of(span * SPAN, SPAN)
        n_rows = q.shape[0]
        z = lax.dot_general(q, k_ref[0, pl.ds(st, SPAN), sl], (((1,), (1,)), ((), ())),
                            preferred_element_type=F32)
        subs = []
        for j in range(SPAN_BLOCKS):
            zj = z[:, j * BLOCK:(j + 1) * BLOCK]
            cost = _softplus2(zj)
            mask = None
            if masked:
                mask = (j * BLOCK + col) < row
                cost = jnp.where(mask, cost, 0.0)
            subs.append((zj, cost, mask))
        stacked = jnp.concatenate([_bf(cost) for _, cost, _ in subs], axis=0)
        within = jnp.dot(stacked, later, preferred_element_type=F32)
        atts = [None] * SPAN_BLOCKS
        for j in reversed(range(SPAN_BLOCKS)):
            zj, cost, mask = subs[j]
            w = within[j * n_rows:(j + 1) * n_rows]
            att = jnp.exp2(zj - cost - w - run)
            if masked:
                att = jnp.where(mask, att, 0.0)
            atts[j] = _bf(att)
            run = run + jnp.sum(cost, axis=1, keepdims=True)
        contrib = jnp.dot(jnp.concatenate(atts, axis=1), v_ref[0, pl.ds(st, SPAN), sl],
                          preferred_element_type=F32)
        return contrib, run

    lanes = [slice(h * HEAD_DIM, (h + 1) * HEAD_DIM) for h in range(nh)]

    def step(span, rows, carry, masked):
        out = []
        for sl, (acc, run) in zip(lanes, carry):
            contrib, run = span_tile(q_ref[0, rows, sl], sl, span, run, masked)
            out.append((acc + contrib, run))
        return tuple(out)

    init = tuple((jnp.zeros((SPAN, HEAD_DIM), F32), jnp.zeros((SPAN, 1), F32)) for _ in range(nh))
    diag = step(qs, slice(None), init, True)

    group = SPAN // SB_ROW_GROUPS
    outs = [[] for _ in range(nh)]
    for g in range(SB_ROW_GROUPS):
        rows = slice(g * group, (g + 1) * group)

        def more(state):
            i, carry = state
            least = functools.reduce(jnp.minimum, [jnp.min(run) for _, run in carry])
            return (i < qs) & (least < SB_UNDERFLOW_LOG2)

        def body(state, rows=rows):
            i, carry = state
            return i + 1, step(qs - 1 - i, rows, carry, False)

        start = tuple((acc[rows], run[rows]) for acc, run in diag)
        _, carry = lax.while_loop(more, body, (jnp.int32(0), start))
        for h, (acc, _) in enumerate(carry):
            outs[h].append(acc)
    for sl, parts in zip(lanes, outs):
        o_ref[0, :, sl] = jnp.concatenate(parts, axis=0).astype(o_ref.dtype)


def _stick_breaking(qkv, col0):
    bsz, s, _ = qkv.shape
    nhp = GROUP_WIDTH // LANES
    later = jnp.tril(jnp.ones((BLOCK, BLOCK), F32), -1).astype(BF16)
    return pl.pallas_call(
        _sb_kernel,
        grid=(bsz, nhp, s // SPAN),
        in_specs=[
            pl.BlockSpec((1, SPAN, LANES), lambda b, h, i: (b, i, col0 + h)),
            pl.BlockSpec((1, s, LANES), lambda b, h, i: (b, 0, col0 + nhp + h)),
            pl.BlockSpec((1, s, LANES), lambda b, h, i: (b, 0, col0 + 2 * nhp + h)),
            pl.BlockSpec((BLOCK, BLOCK), lambda b, h, i: (0, 0)),
        ],
        out_specs=pl.BlockSpec((1, SPAN, LANES), lambda b, h, i: (b, i, h)),
        out_shape=jax.ShapeDtypeStruct((bsz, s, GROUP_WIDTH), BF16),
        compiler_params=_params(("parallel", "parallel", "arbitrary")),
        name="stick_breaking",
    )(qkv, qkv, qkv, later)


def _diff_kernel(q_ref, k_ref, v_ref, vec_ref, lam_ref, subln_ref, o_ref, bias_scr, *, lam_init):
    qs = pl.program_id(2)
    causal = _iota2((SPAN, SPAN), 1) <= _iota2((SPAN, SPAN), 0)
    halves = [slice(c * DIFF_HALF, (c + 1) * DIFF_HALF) for c in range(2)]
    q = [q_ref[0, :, sl] for sl in halves]

    @pl.when((pl.program_id(1) == 0) & (qs == 0))
    def _():
        for dist in range(bias_scr.shape[0]):
            vec = jnp.broadcast_to(vec_ref[0, dist][0:1, :], (SPAN, 2 * SPAN))
            bias_scr[dist] = pltpu.roll(vec, 0, 1, stride=1, stride_axis=0)[:, :SPAN]

    def step(span, carry, masked):
        st = pl.multiple_of(span * SPAN, SPAN)
        bias = bias_scr[qs - span]
        v = v_ref[0, pl.ds(st, SPAN), :]
        out = []
        for c in range(2):
            m, l, acc = carry[c]
            s = lax.dot_general(q[c], k_ref[0, pl.ds(st, SPAN), halves[c]], (((1,), (1,)), ((), ())),
                                preferred_element_type=F32) + bias
            if masked:
                s = jnp.where(causal, s, NEG_BIG)
            m_new = jnp.maximum(m, jnp.max(s, axis=1, keepdims=True))
            alpha = jnp.exp2(m - m_new)
            p = jnp.exp2(s - m_new)
            l = alpha * l + jnp.sum(p, axis=1, keepdims=True)
            acc = alpha * acc + jnp.dot(_bf(p), v, preferred_element_type=F32)
            out.append((m_new, l, acc))
        return tuple(out)

    def init():
        return (jnp.full((SPAN, 1), NEG_BIG, F32), jnp.zeros((SPAN, 1), F32),
                jnp.zeros((SPAN, DIFF_VDIM), F32))

    carry = step(qs, (init(), init()), True)
    carry = lax.fori_loop(0, qs, lambda i, c: step(i, c, False), carry)

    lp = lam_ref[...]
    lam = (jnp.exp(jnp.sum(lp[0:1] * lp[1:2], axis=1, keepdims=True))
           - jnp.exp(jnp.sum(lp[2:3] * lp[3:4], axis=1, keepdims=True)) + lam_init)
    (_, l0, a0), (_, l1, a1) = carry
    o = a0 / l0 - lam * (a1 / l1)
    ms = jnp.mean(o * o, axis=-1, keepdims=True)
    o = o * lax.rsqrt(ms + SUBLN_EPS) * subln_ref[...] * (1.0 - lam_init)
    o_ref[0] = o.astype(o_ref.dtype)


def _diff_attention(qkv, col0, bias_vecs, lam_params, subln, lam_init):
    bsz, s, _ = qkv.shape
    ns = s // SPAN
    return pl.pallas_call(
        functools.partial(_diff_kernel, lam_init=lam_init),
        grid=(H_D, bsz, ns),
        in_specs=[
            pl.BlockSpec((1, SPAN, LANES), lambda h, b, i: (b, i, col0 + h)),
            pl.BlockSpec((1, s, LANES), lambda h, b, i: (b, 0, col0 + H_D + h)),
            pl.BlockSpec((1, s, LANES), lambda h, b, i: (b, 0, col0 + 2 * H_D + h)),
            pl.BlockSpec((1, ns, SUBLANES, 2 * SPAN), lambda h, b, i: (h, 0, 0, 0)),
            pl.BlockSpec((4, DIFF_HALF), lambda h, b, i: (0, 0)),
            pl.BlockSpec((1, DIFF_VDIM), lambda h, b, i: (0, 0)),
        ],
        out_specs=pl.BlockSpec((1, SPAN, LANES), lambda h, b, i: (b, i, h)),
        out_shape=jax.ShapeDtypeStruct((bsz, s, GROUP_WIDTH), BF16),
        scratch_shapes=[pltpu.VMEM((ns, SPAN, SPAN), F32)],
        compiler_params=_params(("arbitrary", "arbitrary", "arbitrary")),
        name="diff_attention",
    )(qkv, qkv, qkv, bias_vecs, lam_params, subln)


def _dil_kernel(q_ref, k_ref, v_ref, b1_ref, b4_ref, b16_ref, y_ref, acc_scr, m_scr, l_scr):
    s_len = q_ref.shape[1]
    first_head = _iota2((BLOCK, LANES), 1) < HEAD_DIM
    bias_refs = (b1_ref, b4_ref, b16_ref)

    def stack_heads(t):
        return jnp.concatenate([jnp.where(first_head, t, 0.0), jnp.where(first_head, 0.0, t)], axis=0)

    def unstack(t):
        return jnp.where(first_head, t[:BLOCK], t[BLOCK:])

    for pi, (window, dil) in enumerate(DILATIONS):
        n_back = window // dil
        nc = s_len // dil // BLOCK
        has_prev = nc > 1
        assert not has_prev or nc % DIL_TILES_PER_STEP == 0
        width = 2 * BLOCK if has_prev else BLOCK
        qi = _iota2((2 * BLOCK, width), 0) & (BLOCK - 1)
        ki = _iota2((2 * BLOCK, width), 1)
        steps = qi + (BLOCK if has_prev else 0) - ki
        band = (steps >= 0) & (steps <= n_back)
        bias = bias_refs[pi][0]
        if not has_prev:
            bias = bias[:, BLOCK:]

        def body(it, carry, dil=dil, nc=nc, has_prev=has_prev, band=band, bias=bias, pi=pi, ki=ki):
            idx0 = it * DIL_TILES_PER_STEP
            tiles = []
            for u in range(DIL_TILES_PER_STEP):
                r, c = (idx0 + u) // nc, (idx0 + u) % nc
                tiles.append((pl.ds(r + dil * BLOCK * c, BLOCK, stride=dil), c))
            kb = [_bf(k_ref[0, rows, :]) for rows, _ in tiles]
            vb = [_bf(v_ref[0, rows, :]) for rows, _ in tiles]
            if has_prev:
                r0, c0 = idx0 // nc, idx0 % nc
                prows = pl.ds(r0 + dil * BLOCK * jnp.maximum(c0 - 1, 0), BLOCK, stride=dil)
                kb = [_bf(k_ref[0, prows, :])] + kb
                vb = [_bf(v_ref[0, prows, :])] + vb
            loads = []
            for u, (rows, c) in enumerate(tiles):
                q = _bf(stack_heads(q_ref[0, rows, :]))
                if has_prev:
                    k = jnp.concatenate([kb[u], kb[u + 1]], axis=0)
                    v = jnp.concatenate([vb[u], vb[u + 1]], axis=0)
                else:
                    k, v = kb[u], vb[u]
                loads.append((rows, c, q, k, v))
            ss = [lax.dot_general(q, k, (((1,), (1,)), ((), ())), preferred_element_type=F32) + bias
                  for _, _, q, k, _ in loads]
            if has_prev:
                ss = [jnp.where(band & (ki >= jnp.where(c > 0, 0, BLOCK)), s, NEG_BIG)
                      for s, (_, c, _, _, _) in zip(ss, loads)]
            else:
                ss = [jnp.where(band, s, NEG_BIG) for s in ss]
            ms = [jnp.max(s, axis=1, keepdims=True) for s in ss]
            ps = [jnp.exp2(s - m) for s, m in zip(ss, ms)]
            ls = [jnp.sum(p, axis=1, keepdims=True) for p in ps]
            os_ = [jnp.dot(_bf(p), v, preferred_element_type=F32) for p, (_, _, _, _, v) in zip(ps, loads)]
            for (rows, _, _, _, _), m, l, o in zip(loads, ms, ls, os_):
                acc_scr[pi, rows, :] = unstack(o)
                m_scr[pi, rows, :] = unstack(jnp.broadcast_to(m, (2 * BLOCK, LANES)))
                l_scr[pi, rows, :] = unstack(jnp.broadcast_to(l, (2 * BLOCK, LANES)))
            return carry

        lax.fori_loop(0, s_len // BLOCK // DIL_TILES_PER_STEP, body, 0)

    tm = 2 * BLOCK

    def merge(i, carry):
        rows = pl.ds(pl.multiple_of(i * tm, tm), tm)
        m1, m2, m3 = m_scr[0, rows, :], m_scr[1, rows, :], m_scr[2, rows, :]
        m = jnp.maximum(jnp.maximum(m1, m2), m3)
        e1, e2, e3 = jnp.exp2(m1 - m), jnp.exp2(m2 - m), jnp.exp2(m3 - m)
        num = e1 * acc_scr[0, rows, :] + e2 * acc_scr[1, rows, :] + e3 * acc_scr[2, rows, :]
        den = e1 * l_scr[0, rows, :] + e2 * l_scr[1, rows, :] + e3 * l_scr[2, rows, :]
        y_ref[0, rows, :] = (num / den).astype(y_ref.dtype)
        return carry

    lax.fori_loop(0, s_len // tm, merge, 0)


def _dilated_attention(qkv_c, biases):
    bsz, s, _ = qkv_c.shape
    nhp = GROUP_WIDTH // LANES
    bspec = pl.BlockSpec((1, 2 * BLOCK, 2 * BLOCK), lambda b, h: (h, 0, 0))
    scr = pltpu.VMEM((len(DILATIONS), s, LANES), F32)
    return pl.pallas_call(
        _dil_kernel,
        grid=(bsz, nhp),
        in_specs=[
            pl.BlockSpec((1, s, LANES), lambda b, h: (b, 0, h)),
            pl.BlockSpec((1, s, LANES), lambda b, h: (b, 0, nhp + h)),
            pl.BlockSpec((1, s, LANES), lambda b, h: (b, 0, 2 * nhp + h)),
            bspec, bspec, bspec,
        ],
        out_specs=pl.BlockSpec((1, s, LANES), lambda b, h: (b, 0, h)),
        out_shape=jax.ShapeDtypeStruct((bsz, s, GROUP_WIDTH), BF16),
        scratch_shapes=[scr, scr, scr],
        compiler_params=_params(("parallel", "parallel")),
        name="dilated_attention",
    )(qkv_c, qkv_c, qkv_c, *biases)


def _unit_lower_inverses(ns):
    size_all = 2 * CHUNK
    row = _iota2((size_all, size_all), 0)
    col = _iota2((size_all, size_all), 1)
    eye = jnp.where(row == col, 1.0, 0.0)
    base = 8
    same8 = (row >> 3) == (col >> 3)
    n8 = [jnp.where(same8, n, 0.0) for n in ns]
    n2 = [_mm(a, a) for a in n8]
    n4 = [_mm(a, a) for a in n2]
    ts = [eye + a for a in n8]
    ts = [t + _mm(t, b) for t, b in zip(ts, n2)]
    ts = [t + _mm(t, b) for t, b in zip(ts, n4)]
    size = 2 * base
    while size <= CHUNK:
        half = size // 2
        shift = size.bit_length() - 1
        off = (((row >> shift) == (col >> shift)) & ((row & (size - 1)) >= half)
               & ((col & (size - 1)) < half))
        us = [_mm(t, jnp.where(off, n, 0.0)) for t, n in zip(ts, ns)]
        ts = [t + _mm(u, t) for t, u in zip(ts, us)]
        size *= 2
    return ts


def _rwkv_kernel(*refs, has_vres):
    if has_vres:
        (pa_ref, mu_ref, w0_ref, wup_ref, a0_ref, aup_ref, gup_ref, kk_ref, ka_ref, rk_ref,
         lnw_ref, lnb_ref, seg_ref, cum_ref, vf_ref, vup_ref, vb_ref,
         y_ref, prev_scr, s_scr) = refs
    else:
        (pa_ref, mu_ref, w0_ref, wup_ref, a0_ref, aup_ref, gup_ref, kk_ref, ka_ref, rk_ref,
         lnw_ref, lnb_ref, seg_ref, cum_ref,
         y_ref, vf_out_ref, prev_scr, s_scr) = refs
    c = GROUP_WIDTH
    tt = pa_ref.shape[1]

    @pl.when(pl.program_id(1) == 0)
    def _():
        prev_scr[...] = jnp.zeros_like(prev_scr)
        s_scr[...] = jnp.zeros_like(s_scr)

    seg = seg_ref[...]

    def seg_sum(t):
        return jnp.concatenate([jnp.dot(_bf(t[:, p * LANES:(p + 1) * LANES]), seg, preferred_element_type=F32)
                                for p in range(c // LANES)], axis=1)

    pa = pa_ref[0]
    rolled = pltpu.roll(pa, 1, 0)
    first = _iota2(pa.shape, 0) == 0
    prev = jnp.where(first, prev_scr[...], rolled)
    prev_scr[...] = pa[tt - 1:tt, :]
    sh = pa + (prev - pa) * mu_ref[...]
    r = sh[:, 0:c]
    k = sh[:, c:2 * c]
    v = sh[:, 2 * c:3 * c]
    w_lo = sh[:, A_OFF_W:A_OFF_W + LORA_PAD]
    a_lo = sh[:, A_OFF_A:A_OFF_A + LORA_PAD]
    g_lo = sh[:, A_OFF_G:A_OFF_G + G_PAD]
    if has_vres:
        pv = sh[:, A_VRES_TILE:A_VRES_TILE + LANES]
        v = v + (vf_ref[0] - v) * _sigmoid(vb_ref[...] + _mm(pv, vup_ref[...]))
    else:
        vf_out_ref[0] = v

    w_log = -_softplus(-(w0_ref[...] + _mm(jnp.tanh(w_lo), wup_ref[...]))) - 0.5
    lw = -jnp.exp(w_log)
    a = _sigmoid(a0_ref[...] + _mm(a_lo, aup_ref[...]))
    g = _mm(_sigmoid(g_lo), gup_ref[...])
    kkr = k * kk_ref[...]
    kk = kkr * lax.rsqrt(jnp.maximum(seg_sum(kkr * kkr), 1e-24))
    k = k * (1.0 + (a - 1.0) * ka_ref[...])

    cum = _mm_split_rhs(cum_ref[...], lw, 2)
    eg = jnp.exp(cum)
    eig = jnp.exp(-cum)
    at = -kk * jnp.exp(cum - lw)
    rt = r * eg
    bt = kk * a * eig
    kt = k * eig

    pair_rows = 2 * CHUNK
    first_head = _iota2((CHUNK, LANES), 1) < HEAD_DIM
    row = _iota2((pair_rows, pair_rows), 0)
    col = _iota2((pair_rows, pair_rows), 1)
    strict = (row & (CHUNK - 1)) > (col & (CHUNK - 1))
    lower = (row & (CHUNK - 1)) >= (col & (CHUNK - 1))
    eye = row == col

    def stack(t):
        return jnp.concatenate([jnp.where(first_head, t, 0.0), jnp.where(first_head, 0.0, t)], axis=0)

    probs = [(ci, p) for ci in range(tt // CHUNK) for p in range(c // LANES)]
    cut = lambda t, ci, p: t[ci * CHUNK:(ci + 1) * CHUNK, p * LANES:(p + 1) * LANES]
    g_end = [cut(eg, ci, p)[CHUNK - 1:CHUNK, :] for ci, p in probs]
    at_p = [stack(cut(at, ci, p)) for ci, p in probs]
    rt_p = [stack(cut(rt, ci, p)) for ci, p in probs]
    bt_p = [stack(cut(bt, ci, p)) for ci, p in probs]
    kt_p = [stack(cut(kt, ci, p)) for ci, p in probs]
    v_p = [stack(cut(v, ci, p)) for ci, p in probs]
    bg_p = [b_ * ge for b_, ge in zip(bt_p, g_end)]
    kg_p = [k_ * ge for k_, ge in zip(kt_p, g_end)]
    grams = [_mm_nt(jnp.concatenate([a_, r_], axis=0), jnp.concatenate([b_, k_], axis=0))
             for a_, r_, b_, k_ in zip(at_p, rt_p, bt_p, kt_p)]
    a_ab = [jnp.where(strict, gm[:pair_rows, :pair_rows], 0.0) for gm in grams]
    a_ak = [jnp.where(strict, gm[:pair_rows, pair_rows:], 0.0) for gm in grams]
    a_rb = [jnp.where(lower, gm[pair_rows:, :pair_rows], 0.0) for gm in grams]
    a_rk = [jnp.where(lower, gm[pair_rows:, pair_rows:], 0.0) for gm in grams]
    both_v = [_mm(jnp.concatenate([a_, r_], axis=0), v_) for a_, r_, v_ in zip(a_ak, a_rk, v_p)]
    akv = [x_[:pair_rows] for x_ in both_v]
    rkv = [x_[pair_rows:] for x_ in both_v]
    ts = _unit_lower_inverses(a_ab)
    pq = [_mm(t, jnp.concatenate([a_, x_], axis=1)) for t, a_, x_ in zip(ts, at_p, akv)]
    ps = [x_[:, :LANES] for x_ in pq]
    qs = [x_[:, LANES:] for x_ in pq]
    rb_pq = [_mm(a_, x_) for a_, x_ in zip(a_rb, pq)]
    r2 = [r_ + x_[:, :LANES] for r_, x_ in zip(rt_p, rb_pq)]
    y0 = [x_[:, LANES:] + k_ for x_, k_ in zip(rb_pq, rkv)]
    ms = [jnp.where(eye, jnp.broadcast_to(ge, (pair_rows, LANES)), 0.0) + _mm_tn(p_, b_)
          for ge, p_, b_ in zip(g_end, ps, bg_p)]
    zs = [_mm_tn(jnp.concatenate([q_, v_], axis=0), jnp.concatenate([b_, k_], axis=0))
          for q_, b_, v_, k_ in zip(qs, bg_p, v_p, kg_p)]

    state = [s_scr[p] for p in range(c // LANES)]
    y_rows = []
    for ci in range(tt // CHUNK):
        y_pairs = []
        for p in range(c // LANES):
            i = ci * (c // LANES) + p
            y_st = _mm_nt(r2[i], state[p]) + y0[i]
            y_pairs.append(y_st[:CHUNK] + y_st[CHUNK:])
            state[p] = _mm(state[p], ms[i]) + zs[i]
        y_rows.append(jnp.concatenate(y_pairs, axis=1))
    for p in range(c // LANES):
        s_scr[p] = state[p]
    y = jnp.concatenate(y_rows, axis=0)

    mu = seg_sum(y) * (1.0 / HEAD_DIM)
    d = y - mu
    var = seg_sum(d * d) * (1.0 / HEAD_DIM)
    yn = d * lax.rsqrt(var + RWKV_LN_EPS) * lnw_ref[...] + lnb_ref[...]
    bonus = seg_sum(r * k * rk_ref[...]) * v
    y_ref[0] = ((yn + bonus) * g).astype(y_ref.dtype)


def _rwkv7(pa, prm, v_first):
    bsz, s, _ = pa.shape
    na = prm["mu"].shape[1]
    c = GROUP_WIDTH
    tt = RWKV_TILE
    has_vres = v_first is not None
    full = lambda shape: pl.BlockSpec(shape, lambda b, t: (0,) * len(shape))
    tile = lambda w: pl.BlockSpec((1, tt, w), lambda b, t: (b, t, 0))
    idx = jnp.arange(tt)
    cum_ones = ((idx[:, None] >= idx[None, :]) & (idx[:, None] // CHUNK == idx[None, :] // CHUNK)).astype(BF16)
    in_specs = [tile(na), full((1, na)), full((1, c)), full((LORA_PAD, c)), full((1, c)),
                full((LORA_PAD, c)), full((G_PAD, c)), full((1, c)), full((1, c)), full((1, c)),
                full((1, c)), full((1, c)), full((LANES, LANES)), full((tt, tt))]
    args = [pa, prm["mu"], prm["w0"], prm["w_up"], prm["a0"], prm["a_up"], prm["g_up"],
            prm["k_k"], prm["k_a"], prm["r_k"], prm["ln_w"], prm["ln_b"], prm["seg"], cum_ones]
    y_shape = jax.ShapeDtypeStruct((bsz, s, c), BF16)
    if has_vres:
        in_specs += [tile(c), full((LORA_PAD, c)), full((1, c))]
        args += [v_first, prm["vres_up"], prm["vres_bias"]]
        out_specs, out_shape = tile(c), y_shape
    else:
        out_specs = [tile(c), tile(c)]
        out_shape = [y_shape, jax.ShapeDtypeStruct((bsz, s, c), F32)]
    res = pl.pallas_call(
        functools.partial(_rwkv_kernel, has_vres=has_vres),
        grid=(bsz, s // tt),
        in_specs=in_specs,
        out_specs=out_specs,
        out_shape=out_shape,
        scratch_shapes=[pltpu.VMEM((1, na), F32), pltpu.VMEM((c // LANES, LANES, LANES), F32)],
        compiler_params=_params(("parallel", "arbitrary")),
        name="rwkv7_vres" if has_vres else "rwkv7",
    )(*args)
    if has_vres:
        return res, v_first
    return res[0], res[1]


def _rel_bucket(dist):
    max_exact = NUM_BUCKETS // 2
    d_f = jnp.maximum(dist, 1).astype(F32)
    large = max_exact + (jnp.log(d_f / max_exact) / math.log(MAX_DISTANCE / max_exact)
                         * (NUM_BUCKETS - max_exact)).astype(jnp.int32)
    large = jnp.minimum(large, NUM_BUCKETS - 1)
    return jnp.where(dist < max_exact, dist, large)


def _toeplitz(u, n_rows, n_cols):
    length = n_rows + n_cols - 1
    g = jnp.roll(jnp.flip(u, axis=-1), -(n_rows - 1), axis=-1)
    flat = jnp.tile(g, (1,) * (u.ndim - 1) + (n_rows,))[..., :n_rows * (length - 1)]
    return flat.reshape(u.shape[:-1] + (n_rows, length - 1))[..., :n_cols]


def _diff_bias_vectors(rel_tab, ns):
    n = jnp.arange(2 * SPAN)
    rel = jnp.where(n < SPAN, -n, 2 * SPAN - n)
    dist = jnp.arange(ns)[:, None] * SPAN + rel[None, :]
    u = rel_tab[_rel_bucket(jnp.maximum(dist, 0))] * LOG2E
    u = jnp.transpose(u, (2, 0, 1)).astype(F32)
    return jnp.broadcast_to(u[:, :, None, :], (H_D, ns, SUBLANES, 2 * SPAN))


def _band_bias(rel_tab, dil):
    steps = jnp.arange(3 * BLOCK - 1) - (BLOCK - 1)
    u = rel_tab[_rel_bucket(jnp.maximum(steps, 0) * dil)] * LOG2E
    t = _toeplitz(jnp.transpose(u, (1, 0)).astype(F32), BLOCK, 2 * BLOCK)
    return t.reshape(N_HEADS * HEAD_DIM // LANES, 2 * BLOCK, 2 * BLOCK)


def _pad_axis(t, size, axis):
    pad = [(0, 0)] * t.ndim
    pad[axis] = (0, size - t.shape[axis])
    return jnp.pad(t, pad)


def _layout_mixer_a(t, vres=None):
    c = GROUP_WIDTH
    o_w, o_a, o_g = 3 * c, 3 * c + W_LORA, 3 * c + W_LORA + A_LORA
    gate = t[..., o_g:N_A_COLS]
    if vres is not None:
        gate = jnp.concatenate([gate, vres], axis=-1)
    parts = [t[..., :o_w], _pad_axis(t[..., o_w:o_a], LORA_PAD, -1),
             _pad_axis(t[..., o_a:o_g], LORA_PAD, -1), _pad_axis(gate, G_PAD, -1)]
    return jnp.concatenate(parts, axis=-1)


def _scale_queries(w_qkv, scale):
    return jnp.concatenate([w_qkv[:, :GROUP_WIDTH] * scale, w_qkv[:, GROUP_WIDTH:]], axis=1)


def kernel(x, c, w_ada, b_ada, norm_gain, w_in, w_out, rel_bias, rwkv_mu, rwkv_w0, rwkv_w_up,
           rwkv_a0, rwkv_a_up, rwkv_g_up, rwkv_k_k, rwkv_k_a, rwkv_r_k, rwkv_ln_w, rwkv_ln_b,
           vres_down, vres_mu, vres_up, vres_bias, diff_lambda, diff_subln, ffn_w13, ffn_w2,
           final_gain):
    out_dtype = x.dtype
    bsz, s, d = x.shape
    cw = GROUP_WIDTH
    x = x.astype(F32)
    row = lambda t: t.reshape(1, -1).astype(F32)

    mod = _modulation(c.astype(F32), w_ada, b_ada)
    rel_tab_c = rel_bias[:, :N_HEADS].astype(F32)
    rel_tab_d = rel_bias[:, N_HEADS:].astype(F32)
    band_biases = [_band_bias(rel_tab_c, dil) for _, dil in DILATIONS]
    diff_bias = _diff_bias_vectors(rel_tab_d, s // SPAN)
    lane = jnp.arange(LANES)
    seg = (lane[:, None] // HEAD_DIM == lane[None, :] // HEAD_DIM).astype(BF16)
    logit_scale = LOG2E * HEAD_DIM ** -0.5

    v_first = None
    for l in range(DEPTH):
        sh1, sc1, g1, sh2, sc2, g2 = [mod[l, :, None, i * d:(i + 1) * d] for i in range(6)]
        gain1, gain2 = row(norm_gain[l, 0]), row(norm_gain[l, 1])
        vres_w = vres_down[l - 1] if l > 0 else None
        w_l = w_in[l].astype(F32)
        o_b, o_c, o_d = N_A_COLS, N_A_COLS + 3 * cw, N_A_COLS + 6 * cw
        w_a = _layout_mixer_a(w_l[:, :N_A_COLS], vres_w)
        w_all = _bf(jnp.concatenate([w_a, _scale_queries(w_l[:, o_b:o_c], logit_scale),
                                     _scale_queries(w_l[:, o_d:N_IN], logit_scale),
                                     _scale_queries(w_l[:, o_c:o_d], logit_scale)], axis=1))
        mu = _layout_mixer_a(rwkv_mu[l], vres_mu[l - 1] if l > 0 else None)

        pa, qkv_bd, qkv_c = _norm_proj(x, gain1, sc1, sh1, w_all, (w_a.shape[1], 6 * cw, 3 * cw),
                                       (F32, BF16, F32))

        prm = {
            "mu": row(mu), "w0": row(rwkv_w0[l]), "a0": row(rwkv_a0[l]),
            "w_up": _bf(_pad_axis(rwkv_w_up[l], LORA_PAD, 0)),
            "a_up": _bf(_pad_axis(rwkv_a_up[l], LORA_PAD, 0)),
            "g_up": _bf(_pad_axis(rwkv_g_up[l], G_PAD, 0)),
            "k_k": row(rwkv_k_k[l]), "k_a": row(rwkv_k_a[l]), "r_k": row(rwkv_r_k[l]),
            "ln_w": row(rwkv_ln_w[l]), "ln_b": row(rwkv_ln_b[l]), "seg": seg,
        }
        if l > 0:
            prm["vres_up"] = _bf(jnp.pad(vres_up[l - 1], ((VRES_LANE0, LANES - VRES_LANE0 - V_LORA), (0, 0))))
            prm["vres_bias"] = row(vres_bias[l - 1])
        y_a, v_first = _rwkv7(pa, prm, v_first)

        y_b = _stick_breaking(qkv_bd, 0)
        y_c = _dilated_attention(qkv_c, band_biases)
        lam_init = 0.8 - 0.6 * math.exp(-0.3 * l)
        y_d = _diff_attention(qkv_bd, 3 * cw // LANES, diff_bias, diff_lambda[l].astype(F32),
                              row(diff_subln[l]), lam_init)

        x = _out_proj(y_a, y_b, y_c, y_d, _bf(w_out[l]), x, g1)
        x = _ffn(x, gain2, sc2, sh2, g2, _bf(ffn_w13[l]), _bf(ffn_w2[l]), row(final_gain),
                 final=(l == DEPTH - 1))
    return x.astype(out_dtype)
```
